```python
import math
import jax
import jax.numpy as jnp
from jax import lax
import numpy as np

D_MODEL = 1024
BATCH = 4
SEQ = 4096
DEPTH = 2

SSD_HEADS = 8
SSD_HEAD_DIM = 64
SSD_INNER = SSD_HEADS * SSD_HEAD_DIM
SSD_GROUPS = 2
SSD_STATE = 128
SSD_CONV = 5
SSD_CHUNK = 256
SSD_CONV_DIM = SSD_INNER + 2 * SSD_GROUPS * SSD_STATE
SWA_HEADS = 4
SWA_KV_HEADS = 2
SWA_HEAD_DIM = 64
SWA_WINDOW = 128
SWA_BLOCK = 128
SWA_WIDTH = SWA_HEADS * SWA_HEAD_DIM
MLA_HEADS = 4
MLA_Q_RANK = 256
MLA_KV_RANK = 128
MLA_NOPE = 64
MLA_ROPE = 32
MLA_QK = MLA_NOPE + MLA_ROPE
MLA_V = 64
MLA_WIDTH = MLA_HEADS * MLA_V
MLA_BLOCK = 128
D_MIX = SSD_INNER + SWA_WIDTH + MLA_WIDTH
IN_SIZES = (SSD_INNER, SSD_CONV_DIM, 2 * SSD_HEADS,
            SWA_HEADS * SWA_HEAD_DIM, 2 * SWA_KV_HEADS * SWA_HEAD_DIM,
            MLA_Q_RANK, MLA_KV_RANK, MLA_ROPE)
D_IN_PROJ = sum(IN_SIZES)
D_FF = 2816
ROPE_THETA = 10000.0
EPS = 1e-6

kernel_name = 'hybrid_ssd_swa_mla_macaron_encoder'


def rms_norm(x, w):
    xf = x.astype(jnp.float32)
    y = xf * lax.rsqrt(jnp.mean(xf * xf, axis=-1, keepdims=True) + EPS)
    return (y * w.astype(jnp.float32)).astype(x.dtype)


def split_columns(t, sizes):
    idx, acc = [], 0
    for sz in sizes[:-1]:
        acc += sz
        idx.append(acc)
    return jnp.split(t, idx, axis=-1)


def rope_tables(n, dim):
    inv = 1.0 / jnp.power(ROPE_THETA, jnp.arange(0, dim, 2, dtype=jnp.float32) / dim)
    ang = jnp.arange(n, dtype=jnp.float32)[:, None] * inv[None, :]
    return jnp.cos(ang), jnp.sin(ang)


def apply_rope(x, cos, sin):
    x1, x2 = jnp.split(x, 2, axis=-1)
    c = cos[None, :, None, :]
    sn = sin[None, :, None, :]
    return jnp.concatenate([x1 * c - x2 * sn, x1 * sn + x2 * c], axis=-1).astype(x.dtype)


def swiglu(x, w_gate, w_up, w_down):
    return (jax.nn.silu(x @ w_gate) * (x @ w_up)) @ w_down


def centred_depthwise_conv(x, w, b):
    k = w.shape[0]
    y = lax.conv_general_dilated(x, w[:, None, :], window_strides=(1,),
                                 padding=[(k // 2, k // 2)],
                                 dimension_numbers=('NWC', 'WIO', 'NWC'),
                                 feature_group_count=x.shape[-1])
    return y + b


def segsum(a):
    t = a.shape[-1]
    a_rep = jnp.broadcast_to(a[..., :, None], a.shape + (t,))
    strict = jnp.tril(jnp.ones((t, t), dtype=bool), -1)
    cs = jnp.cumsum(jnp.where(strict, a_rep, 0.0), axis=-2)
    return jnp.where(jnp.tril(jnp.ones((t, t), dtype=bool)), cs, -jnp.inf)


def ssd_scan(x, dt, a_decay, bm, cm):
    b, l, h, p = x.shape
    g, n = bm.shape[-2:]
    r = h // g
    xdt = x.astype(jnp.float32) * dt[..., None]
    a = dt * a_decay
    bf = bm.astype(jnp.float32)
    cf = cm.astype(jnp.float32)
    pad = (-l) % SSD_CHUNK
    if pad:
        pw = lambda t: jnp.pad(t, [(0, 0), (0, pad)] + [(0, 0)] * (t.ndim - 2))
        xdt, a, bf, cf = pw(xdt), pw(a), pw(bf), pw(cf)
    t = SSD_CHUNK
    nc = (l + pad) // t
    xc = xdt.reshape(b, nc, t, g, r, p)
    ac = a.reshape(b, nc, t, g, r).transpose(0, 3, 4, 1, 2)
    bc = bf.reshape(b, nc, t, g, n)
    cc = cf.reshape(b, nc, t, g, n)
    a_cum = jnp.cumsum(ac, axis=-1)
    l_mat = jnp.exp(segsum(ac))
    cb = jnp.einsum('bclgn,bcsgn->bgcls', cc, bc)
    y_diag = jnp.einsum('bgcls,bgrcls,bcsgrp->bclgrp', cb, l_mat, xc)
    decay_states = jnp.exp(a_cum[..., -1:] - a_cum)
    states = jnp.einsum('bclgn,bgrcl,bclgrp->bcgrpn', bc, decay_states, xc)
    states = jnp.concatenate([jnp.zeros_like(states[:, :1]), states], axis=1)
    chunk_tot = jnp.pad(a_cum[..., -1], [(0, 0), (0, 0), (0, 0), (1, 0)])
    decay_chunk = jnp.exp(segsum(chunk_tot))
    states = jnp.einsum('bgrzc,bcgrpn->bzgrpn', decay_chunk, states)[:, :-1]
    y_off = jnp.einsum('bclgn,bcgrpn,bgrcl->bclgrp', cc, states, jnp.exp(a_cum))
    return (y_diag + y_off).reshape(b, nc * t, h, p)[:, :l]


def ssd_mixer(z, xbc, dt_raw, conv_w, conv_b, dt_bias, a_log, d_skip, norm_w):
    b, s, _ = xbc.shape
    xbc = jax.nn.silu(centred_depthwise_conv(xbc, conv_w, conv_b))
    xs, bm, cm = jnp.split(xbc, [SSD_INNER, SSD_INNER + SSD_GROUPS * SSD_STATE], axis=-1)
    xs = xs.reshape(b, s, SSD_HEADS, SSD_HEAD_DIM)
    bm = bm.reshape(b, s, SSD_GROUPS, SSD_STATE)
    cm = cm.reshape(b, s, SSD_GROUPS, SSD_STATE)
    dt = jax.nn.softplus(dt_raw.astype(jnp.float32).reshape(b, s, 2, SSD_HEADS)
                         + dt_bias.astype(jnp.float32))
    a_decay = -jnp.exp(a_log.astype(jnp.float32))
    flip = lambda t: jnp.flip(t, axis=1)
    y_fwd = ssd_scan(xs, dt[:, :, 0], a_decay[0], bm, cm)
    y_bwd = flip(ssd_scan(flip(xs), flip(dt[:, :, 1]), a_decay[1], flip(bm), flip(cm)))
    y = y_fwd + y_bwd + d_skip.astype(jnp.float32)[:, None] * xs.astype(jnp.float32)
    y = y.reshape(b, s, SSD_INNER) * jax.nn.silu(z.astype(jnp.float32))
    y = y.reshape(b, s, SSD_GROUPS, SSD_INNER // SSD_GROUPS)
    y = y * lax.rsqrt(jnp.mean(y * y, axis=-1, keepdims=True) + EPS)
    return (y.reshape(b, s, SSD_INNER) * norm_w.astype(jnp.float32)).astype(z.dtype)


def swa_mixer(q, kv, cos, sin, q_norm_w, k_norm_w, sink, out_norm_w):
    b, s, _ = q.shape
    hd, kvh, blk = SWA_HEAD_DIM, SWA_KV_HEADS, SWA_BLOCK
    rep = SWA_HEADS // kvh
    q = q.reshape(b, s, SWA_HEADS, hd)
    k, v = jnp.split(kv, 2, axis=-1)
    k = k.reshape(b, s, kvh, hd)
    v = v.reshape(b, s, kvh, hd)
    q = apply_rope(rms_norm(q, q_norm_w), cos, sin)
    k = apply_rope(rms_norm(k, k_norm_w), cos, sin)
    nb = s // blk
    qb = q.reshape(b, nb, blk, kvh, rep, hd)

    def band(t):
        tp = jnp.pad(t, [(0, 0), (blk, blk), (0, 0), (0, 0)]).reshape(b, nb + 2, blk, kvh, hd)
        return jnp.concatenate([tp[:, :-2], tp[:, 1:-1], tp[:, 2:]], axis=2)

    kb, vb = band(k), band(v)
    scores = jnp.einsum('bnqkrd,bnjkd->bnkrqj', qb, kb).astype(jnp.float32) * (hd ** -0.5)
    qpos = jnp.arange(nb)[:, None] * blk + jnp.arange(blk)[None, :]
    kpos = (jnp.arange(nb)[:, None] - 1) * blk + jnp.arange(3 * blk)[None, :]
    rel = kpos[:, None, :] - qpos[:, :, None]
    valid = (jnp.abs(rel) <= SWA_WINDOW) & (kpos[:, None, :] >= 0) & (kpos[:, None, :] < s)
    scores = jnp.where(valid[None, :, None, None], scores, -jnp.inf)
    sink_col = jnp.broadcast_to(sink.astype(jnp.float32).reshape(kvh, rep)[None, None, :, :, None, None],
                                scores.shape[:-1] + (1,))
    probs = jax.nn.softmax(jnp.concatenate([scores, sink_col], axis=-1), axis=-1)[..., :-1]
    out = jnp.einsum('bnkrqj,bnjkd->bnqkrd', probs.astype(vb.dtype), vb)
    return rms_norm(out.reshape(b, s, SWA_WIDTH), out_norm_w)


def mla_mixer(q_lat, c_kv, k_rope, cos_r, sin_r, q_lat_norm_w, w_uq, kv_norm_w, w_ukv,
              q_norm_w, k_norm_w, out_norm_w):
    b, s, _ = q_lat.shape
    h = MLA_HEADS
    q = (rms_norm(q_lat, q_lat_norm_w) @ w_uq).reshape(b, s, h, MLA_QK)
    kv = (rms_norm(c_kv, kv_norm_w) @ w_ukv).reshape(b, s, h, MLA_NOPE + MLA_V)
    k_nope, v = jnp.split(kv, [MLA_NOPE], axis=-1)
    k = jnp.concatenate([k_nope, jnp.broadcast_to(k_rope[:, :, None, :], (b, s, h, MLA_ROPE))], axis=-1)
    q = rms_norm(q, q_norm_w)
    k = rms_norm(k, k_norm_w)
    q = jnp.concatenate([q[..., :MLA_NOPE], apply_rope(q[..., MLA_NOPE:], cos_r, sin_r)], axis=-1)
    k = jnp.concatenate([k[..., :MLA_NOPE], apply_rope(k[..., MLA_NOPE:], cos_r, sin_r)], axis=-1)
    scale = MLA_QK ** -0.5
    nb = s // MLA_BLOCK
    qb = q.reshape(b, nb, MLA_BLOCK, h, MLA_QK).transpose(1, 0, 2, 3, 4)

    def attend(q_blk):
        sc = jnp.einsum('bqhd,bkhd->bhqk', q_blk, k).astype(jnp.float32) * scale
        p = jax.nn.softmax(sc, axis=-1)
        return jnp.einsum('bhqk,bkhd->bqhd', p.astype(v.dtype), v)

    out = lax.map(attend, qb)
    out = out.transpose(1, 0, 2, 3, 4).reshape(b, s, MLA_WIDTH)
    return rms_norm(out, out_norm_w)


def setup_inputs(seed: int = 0) -> dict:
    key = jax.random.key(seed)
    ks = iter(jax.random.split(key, 48))
    L = DEPTH

    def nrm(shape, scale):
        return jax.random.normal(next(ks), shape, jnp.float32) * scale

    def gain(shape):
        return 1.0 + nrm(shape, 0.02)

    x = jax.random.normal(next(ks), (BATCH, SEQ, D_MODEL), jnp.float32)
    ffn1_norm = gain((L, D_MODEL))
    ffn1_gate = nrm((L, D_MODEL, D_FF), D_MODEL ** -0.5)
    ffn1_up = nrm((L, D_MODEL, D_FF), D_MODEL ** -0.5)
    ffn1_down = nrm((L, D_FF, D_MODEL), D_FF ** -0.5)
    mix_norm = gain((L, D_MODEL))
    w_in = nrm((L, D_MODEL, D_IN_PROJ), D_MODEL ** -0.5)
    ssd_conv_w = nrm((L, SSD_CONV, SSD_CONV_DIM), SSD_CONV ** -0.5)
    ssd_conv_b = nrm((L, SSD_CONV_DIM), 0.02)
    dt0 = jnp.exp(jax.random.uniform(next(ks), (L, 2, SSD_HEADS), jnp.float32,
                                     minval=math.log(1e-3), maxval=math.log(1e-1)))
    ssd_dt_bias = dt0 + jnp.log(-jnp.expm1(-dt0))
    ssd_a_log = jnp.log(jax.random.uniform(next(ks), (L, 2, SSD_HEADS), jnp.float32,
                                           minval=1.0, maxval=16.0))
    ssd_d = gain((L, SSD_HEADS))
    ssd_norm = gain((L, SSD_INNER))
    swa_q_norm = gain((L, SWA_HEAD_DIM))
    swa_k_norm = gain((L, SWA_HEAD_DIM))
    swa_sink = nrm((L, SWA_HEADS), 0.5)
    swa_out_norm = gain((L, SWA_WIDTH))
    mla_q_lat_norm = gain((L, MLA_Q_RANK))
    mla_w_uq = nrm((L, MLA_Q_RANK, MLA_HEADS * MLA_QK), MLA_Q_RANK ** -0.5)
    mla_kv_norm = gain((L, MLA_KV_RANK))
    mla_w_ukv = nrm((L, MLA_KV_RANK, MLA_HEADS * (MLA_NOPE + MLA_V)), MLA_KV_RANK ** -0.5)
    mla_q_norm = gain((L, MLA_QK))
    mla_k_norm = gain((L, MLA_QK))
    mla_out_norm = gain((L, MLA_WIDTH))
    w_out = nrm((L, D_MIX, D_MODEL), D_MIX ** -0.5)
    ffn2_norm = gain((L, D_MODEL))
    ffn2_gate = nrm((L, D_MODEL, D_FF), D_MODEL ** -0.5)
    ffn2_up = nrm((L, D_MODEL, D_FF), D_MODEL ** -0.5)
    ffn2_down = nrm((L, D_FF, D_MODEL), D_FF ** -0.5)
    return {'x': x, 'ffn1_norm': ffn1_norm, 'ffn1_gate': ffn1_gate, 'ffn1_up': ffn1_up,
            'ffn1_down': ffn1_down, 'mix_norm': mix_norm, 'w_in': w_in,
            'ssd_conv_w': ssd_conv_w, 'ssd_conv_b': ssd_conv_b, 'ssd_dt_bias': ssd_dt_bias,
            'ssd_a_log': ssd_a_log, 'ssd_d': ssd_d, 'ssd_norm': ssd_norm,
            'swa_q_norm': swa_q_norm, 'swa_k_norm': swa_k_norm, 'swa_sink': swa_sink,
            'swa_out_norm': swa_out_norm, 'mla_q_lat_norm': mla_q_lat_norm,
            'mla_w_uq': mla_w_uq, 'mla_kv_norm': mla_kv_norm, 'mla_w_ukv': mla_w_ukv,
            'mla_q_norm': mla_q_norm, 'mla_k_norm': mla_k_norm, 'mla_out_norm': mla_out_norm,
            'w_out': w_out, 'ffn2_norm': ffn2_norm, 'ffn2_gate': ffn2_gate,
            'ffn2_up': ffn2_up, 'ffn2_down': ffn2_down}


def reference(x, ffn1_norm, ffn1_gate, ffn1_up, ffn1_down, mix_norm, w_in,
              ssd_conv_w, ssd_conv_b, ssd_dt_bias, ssd_a_log, ssd_d, ssd_norm,
              swa_q_norm, swa_k_norm, swa_sink, swa_out_norm,
              mla_q_lat_norm, mla_w_uq, mla_kv_norm, mla_w_ukv, mla_q_norm, mla_k_norm,
              mla_out_norm, w_out, ffn2_norm, ffn2_gate, ffn2_up, ffn2_down):
    s = x.shape[1]
    cos, sin = rope_tables(s, SWA_HEAD_DIM)
    cos_r, sin_r = rope_tables(s, MLA_ROPE)
    for l in range(DEPTH):
        x = x + 0.5 * swiglu(rms_norm(x, ffn1_norm[l]), ffn1_gate[l], ffn1_up[l], ffn1_down[l])
        h = rms_norm(x, mix_norm[l])
        proj = h @ w_in[l]
        z, xbc, dt_raw, swa_q, swa_kv, mla_q, mla_ckv, mla_kr = split_columns(proj, IN_SIZES)
        y_ssd = ssd_mixer(z, xbc, dt_raw, ssd_conv_w[l], ssd_conv_b[l], ssd_dt_bias[l],
                          ssd_a_log[l], ssd_d[l], ssd_norm[l])
        y_swa = swa_mixer(swa_q, swa_kv, cos, sin, swa_q_norm[l], swa_k_norm[l],
                          swa_sink[l], swa_out_norm[l])
        y_mla = mla_mixer(mla_q, mla_ckv, mla_kr, cos_r, sin_r, mla_q_lat_norm[l], mla_w_uq[l],
                          mla_kv_norm[l], mla_w_ukv[l], mla_q_norm[l], mla_k_norm[l],
                          mla_out_norm[l])
        y = jnp.concatenate([y_ssd, y_swa, y_mla], axis=-1)
        x = x + y @ w_out[l]
        x = x + 0.5 * swiglu(rms_norm(x, ffn2_norm[l]), ffn2_gate[l], ffn2_up[l], ffn2_down[l])
    return x
```

```python
import functools
import math

import jax
import jax.numpy as jnp
from jax import lax
from jax.experimental import pallas as pl
from jax.experimental.pallas import tpu as pltpu

F32 = jnp.float32
BF16 = jnp.bfloat16

LANES = 128
HALF = LANES // 2

SSD_HEADS = 8
SSD_HEAD_DIM = 64
SSD_INNER = SSD_HEADS * SSD_HEAD_DIM
SSD_GROUPS = 2
SSD_STATE = 128
SSD_CONV = 5
SSD_CHUNK = 256
SSD_CONV_DIM = SSD_INNER + 2 * SSD_GROUPS * SSD_STATE
SWA_HEADS = 4
SWA_KV_HEADS = 2
SWA_HEAD_DIM = 64
SWA_WINDOW = 128
SWA_BLOCK = 128
SWA_WIDTH = SWA_HEADS * SWA_HEAD_DIM
MLA_HEADS = 4
MLA_Q_RANK = 256
MLA_KV_RANK = 128
MLA_NOPE = 64
MLA_ROPE = 32
MLA_QK = MLA_NOPE + MLA_ROPE
MLA_V = 64
MLA_WIDTH = MLA_HEADS * MLA_V
ROPE_THETA = 10000.0
EPS = 1e-6

C_Z = 0
C_XBC = C_Z + SSD_INNER
C_SWQ = C_XBC + SSD_CONV_DIM
C_SWK = C_SWQ + SWA_WIDTH
C_SWV = C_SWK + LANES
C_MLQ = C_SWV + LANES
C_CKV = C_MLQ + MLA_Q_RANK
C_MISC = C_CKV + MLA_KV_RANK
C_END = C_MISC + LANES
CONV_HALO = 16
FFN_CHUNK = 256
NEG_INF = float("-inf")
LOG2E = math.log2(math.e)


def _rms(x, w):
    ms = jnp.mean(x * x, axis=-1, keepdims=True)
    return x * lax.rsqrt(ms + EPS) * w


def _silu(x):
    return x / (1.0 + jnp.exp(-x))


def _dot(a, b):
    return jnp.dot(a, b, preferred_element_type=F32)


def _dot_nt(a, b):
    return lax.dot_general(a, b, (((1,), (1,)), ((), ())), preferred_element_type=F32)


def _dot_tn(a, b):
    return lax.dot_general(a, b, (((0,), (0,)), ((), ())), preferred_element_type=F32)


def _lane_iota(shape):
    return lax.broadcasted_iota(jnp.int32, shape, len(shape) - 1)


def _resident(shape):
    nd = len(shape)
    return pl.BlockSpec(shape, lambda *_: (0,) * nd)


def _ffn_body(x_ref, nw_ref, wg_ref, wu_ref, wd_ref, o_ref, xn_ref, acc_ref):
    x = x_ref[...]
    xn_ref[...] = _rms(x, nw_ref[...]).astype(BF16)
    acc_ref[...] = jnp.zeros_like(acc_ref)

    def chunk(c, carry):
        xn = xn_ref[...]
        g = _dot(xn, wg_ref[c])
        u = _dot(xn, wu_ref[c])
        h = (_silu(g) * u).astype(BF16)
        acc_ref[...] += _dot(h, wd_ref[c])
        return carry

    lax.fori_loop(0, wg_ref.shape[0], chunk, 0)
    o_ref[...] = x + 0.5 * acc_ref[...]


def _ffn(x, nw, wg, wu, wd, tm):
    n, d = x.shape
    nc = wg.shape[0]
    row = pl.BlockSpec((tm, d), lambda i: (i, 0))
    return pl.pallas_call(
        _ffn_body,
        grid=(n // tm,),
        in_specs=[row, _resident((1, d)), _resident(wg.shape), _resident(wu.shape),
                  _resident(wd.shape)],
        out_specs=row,
        out_shape=jax.ShapeDtypeStruct((n, d), F32),
        scratch_shapes=[pltpu.VMEM((tm, d), BF16), pltpu.VMEM((tm, d), F32)],
        compiler_params=pltpu.CompilerParams(
            dimension_semantics=("parallel",), vmem_limit_bytes=56 * 1024 * 1024),
        name="ffn",
    )(x, nw, wg, wu, wd)


def _rope(y, c, sa, sb, shift):
    return (y * c + pltpu.roll(y, LANES - shift, 1) * sa + pltpu.roll(y, shift, 1) * sb)


def _norm_rope_pair64(x, gain, c, sa, sb):
    lo = _lane_iota(x.shape) < HALF
    x2 = x * x
    s_lo = jnp.sum(jnp.where(lo, x2, 0.0), axis=-1, keepdims=True)
    s_hi = jnp.sum(jnp.where(lo, 0.0, x2), axis=-1, keepdims=True)
    ms = jnp.where(lo, s_lo, s_hi) / float(SWA_HEAD_DIM)
    y = x * lax.rsqrt(ms + EPS) * gain
    return _rope(y, c, sa, sb, SWA_HEAD_DIM // 2)


def _norm_rope_mla(x, gain, c, sa, sb):
    ms = jnp.sum(x * x, axis=-1, keepdims=True) / float(MLA_QK)
    y = x * lax.rsqrt(ms + EPS) * gain
    return _rope(y, c, sa, sb, MLA_ROPE // 2)


def _inproj_body(x_ref, xp_ref, xnx_ref, tab_ref, w_ref, mixw_ref, convw_ref, convb_ref,
                 dtb_ref, swqg_ref, swkg_ref, qlw_ref, wuq_ref, kvw_ref, wukv_ref,
                 mqg_ref, mkg_ref,
                 z_ref, xbc_ref, dt_ref, swq_ref, swk_ref, swv_ref, mq_ref, mk_ref, mv_ref,
                 buf_ref, *, tiles_per_seq):
    i = pl.program_id(0)
    tm = x_ref.shape[0]
    pos_tile = i % tiles_per_seq
    mixw = mixw_ref[...]
    hn = _rms(x_ref[...], mixw).astype(BF16)

    z_ref[...] = _dot(hn, w_ref[:, C_Z:C_XBC]).astype(BF16)

    wx = w_ref[:, C_XBC:C_SWQ]
    hp = _rms(xp_ref[...], mixw).astype(BF16)
    hx = _rms(xnx_ref[...], mixw).astype(BF16)
    keep_prev = (pos_tile > 0).astype(F32)
    keep_next = (pos_tile < tiles_per_seq - 1).astype(F32)
    buf_ref[0:CONV_HALO, :] = _dot(hp, wx) * keep_prev
    buf_ref[CONV_HALO:CONV_HALO + tm, :] = _dot(hn, wx)
    buf_ref[CONV_HALO + tm:2 * CONV_HALO + tm, :] = _dot(hx, wx) * keep_next
    conv = jnp.broadcast_to(convb_ref[...], (tm, SSD_CONV_DIM))
    for k in range(SSD_CONV):
        off = CONV_HALO - SSD_CONV // 2 + k
        conv = conv + convw_ref[k:k + 1, :] * buf_ref[off:off + tm, :]
    xbc_ref[...] = _silu(conv).astype(BF16)

    misc = _dot(hn, w_ref[:, C_MISC:C_END])
    dtv = misc + dtb_ref[...]
    dt_ref[...] = jnp.maximum(dtv, 0.0) + jnp.log1p(jnp.exp(-jnp.abs(dtv)))

    c64, sa64, sb64 = tab_ref[0], tab_ref[1], tab_ref[2]
    for g in range(SWA_WIDTH // LANES):
        qg = _dot(hn, w_ref[:, C_SWQ + g * LANES:C_SWQ + (g + 1) * LANES])
        swq_ref[:, g * LANES:(g + 1) * LANES] = _norm_rope_pair64(
            qg, swqg_ref[...], c64, sa64, sb64).astype(BF16)
    kk = _dot(hn, w_ref[:, C_SWK:C_SWV])
    swk_ref[...] = _norm_rope_pair64(kk, swkg_ref[...], c64, sa64, sb64).astype(BF16)
    swv_ref[...] = _dot(hn, w_ref[:, C_SWV:C_MLQ]).astype(BF16)

    cm, sam, sbm = tab_ref[3], tab_ref[4], tab_ref[5]
    ql = _dot(hn, w_ref[:, C_MLQ:C_CKV])
    q = _dot(_rms(ql, qlw_ref[...]).astype(BF16), wuq_ref[...])
    ckv = _dot(hn, w_ref[:, C_CKV:C_MISC])
    kv = _dot(_rms(ckv, kvw_ref[...]).astype(BF16), wukv_ref[...])
    lane = _lane_iota(misc.shape)
    kr = jnp.where((lane >= MLA_NOPE) & (lane < MLA_QK), misc, 0.0)
    for h in range(MLA_HEADS):
        sl = slice(h * LANES, (h + 1) * LANES)
        mq_ref[:, sl] = _norm_rope_mla(q[:, sl], mqg_ref[...], cm, sam, sbm).astype(BF16)
        mk_ref[:, sl] = _norm_rope_mla(kv[:, sl] + kr, mkg_ref[...], cm, sam, sbm).astype(BF16)
    mv_ref[...] = kv[:, MLA_HEADS * LANES:].astype(BF16)


def _in_proj(x, tabs, p, seq, tm):
    n, d = x.shape
    tiles_per_seq = seq // tm
    halo_per_tile = tm // CONV_HALO
    n_halo = n // CONV_HALO
    row = lambda w: pl.BlockSpec((tm, w), lambda i: (i, 0))
    in_specs = [
        row(d),
        pl.BlockSpec((CONV_HALO, d), lambda i: (jnp.maximum(i * halo_per_tile - 1, 0), 0)),
        pl.BlockSpec((CONV_HALO, d),
                     lambda i: (jnp.minimum((i + 1) * halo_per_tile, n_halo - 1), 0)),
        pl.BlockSpec((6, tm, LANES), lambda i: (0, i % tiles_per_seq, 0)),
    ]
    names = ("w_in", "mix_norm", "conv_w", "conv_b", "dt_bias", "swa_q_gain", "swa_k_gain",
             "mla_qlat_norm", "w_uq", "mla_kv_norm", "w_ukv", "mla_q_gain", "mla_k_gain")
    args = [p[k] for k in names]
    in_specs += [_resident(a.shape) for a in args]
    widths = (SSD_INNER, SSD_CONV_DIM, LANES, SWA_WIDTH, LANES, LANES,
              MLA_HEADS * LANES, MLA_HEADS * LANES, MLA_HEADS * LANES)
    dts = (BF16, BF16, F32, BF16, BF16, BF16, BF16, BF16, BF16)
    return pl.pallas_call(
        functools.partial(_inproj_body, tiles_per_seq=tiles_per_seq),
        grid=(n // tm,),
        in_specs=in_specs,
        out_specs=[row(w) for w in widths],
        out_shape=[jax.ShapeDtypeStruct((n, w), t) for w, t in zip(widths, dts)],
        scratch_shapes=[pltpu.VMEM((tm + 2 * CONV_HALO, SSD_CONV_DIM), F32)],
        compiler_params=pltpu.CompilerParams(
            dimension_semantics=("parallel",), vmem_limit_bytes=48 * 1024 * 1024),
        name="in_proj",
    )(x, x, x, tabs, *args)


def _pair_cols(v, h0):
    t = v.shape[0]
    lo = _lane_iota((t, LANES)) < HALF
    return jnp.where(lo, jnp.broadcast_to(v[:, h0:h0 + 1], (t, LANES)),
                     jnp.broadcast_to(v[:, h0 + 1:h0 + 2], (t, LANES)))


def _ssd_body(xbc_ref, dt_ref, z_ref, alog_ref, dskip_ref, nw_ref, o_ref,
              yacc_ref, sf_ref, sb_ref, *, n_chunks):
    pas = pl.program_id(1)
    j = pl.program_id(2)
    t = xbc_ref.shape[0]
    nh = SSD_HEADS
    pairs_per_group = nh // SSD_GROUPS // 2

    lane = _lane_iota((1, LANES))
    avec = jnp.where(lane < 2 * nh, -jnp.exp(alog_ref[...]), 0.0)
    dt = dt_ref[...]
    a = dt * avec
    row = lax.broadcasted_iota(jnp.int32, (t, t), 0)
    col = lax.broadcasted_iota(jnp.int32, (t, t), 1)
    tri = (col <= row).astype(F32)
    cum = jnp.dot(tri, a, preferred_element_type=F32, precision=lax.Precision.HIGHEST)
    excl = cum - a
    tot = cum[t - 1:t, :]
    lo128 = _lane_iota((t, LANES)) < HALF
    lo1 = lane < HALF

    def pair_row(v, h0):
        return jnp.where(lo1, v[:, h0:h0 + 1], v[:, h0 + 1:h0 + 2])

    @pl.when(pas == 0)
    def _forward():
        @pl.when(j == 0)
        def _():
            sf_ref[...] = jnp.zeros_like(sf_ref)

        cum_t = cum.T
        excl_t = excl.T
        dt_t = dt.T
        e_in = jnp.exp(cum)
        w_st = jnp.exp(tot - cum) * dt
        e_tot = jnp.exp(tot)
        for g in range(SSD_GROUPS):
            bg = xbc_ref[:, SSD_INNER + g * SSD_STATE:SSD_INNER + (g + 1) * SSD_STATE]
            cg = xbc_ref[:, SSD_INNER + (SSD_GROUPS + g) * SSD_STATE:
                         SSD_INNER + (SSD_GROUPS + g + 1) * SSD_STATE]
            cb = _dot_nt(cg, bg)
            for pp in range(pairs_per_group):
                pr = g * pairs_per_group + pp
                h0 = 2 * pr
                xp = xbc_ref[:, pr * LANES:(pr + 1) * LANES]
                xpf = xp.astype(F32)
                y = dskip_ref[:, pr * LANES:(pr + 1) * LANES] * xpf
                for hh in range(2):
                    h = h0 + hh
                    df = cum[:, h:h + 1] - cum_t[h:h + 1, :]
                    db = excl_t[nh + h:nh + h + 1, :] - excl[:, nh + h:nh + h + 1]
                    lf = jnp.exp(jnp.where(row >= col, df, NEG_INF))
                    ub = jnp.exp(jnp.where(col >= row, db, NEG_INF))
                    m = cb * (lf * dt_t[h:h + 1, :] + ub * dt_t[nh + h:nh + h + 1, :])
                    keep = lo128 if hh == 0 else jnp.logical_not(lo128)
                    xm = jnp.where(keep, xp, jnp.zeros_like(xp))
                    y = y + _dot(m.astype(BF16), xm)
                s_in = sf_ref[pr]
                y = y + _dot(cg, s_in.astype(BF16)) * _pair_cols(e_in, h0)
                yacc_ref[j, :, pr * LANES:(pr + 1) * LANES] = y
                xw = (xpf * _pair_cols(w_st, h0)).astype(BF16)
                sf_ref[pr] = s_in * pair_row(e_tot, h0) + _dot_tn(bg, xw)

    @pl.when(pas == 1)
    def _backward():
        @pl.when(j == 0)
        def _():
            sb_ref[...] = jnp.zeros_like(sb_ref)

        c = n_chunks - 1 - j
        e_in = jnp.exp(tot - excl)
        w_st = jnp.exp(excl) * dt
        e_tot = jnp.exp(tot)
        zt = z_ref[...].astype(F32)
        gate = _silu(zt)
        for g in range(SSD_GROUPS):
            bg = xbc_ref[:, SSD_INNER + g * SSD_STATE:SSD_INNER + (g + 1) * SSD_STATE]
            cg = xbc_ref[:, SSD_INNER + (SSD_GROUPS + g) * SSD_STATE:
                         SSD_INNER + (SSD_GROUPS + g + 1) * SSD_STATE]
            ys = []
            for pp in range(pairs_per_group):
                pr = g * pairs_per_group + pp
                h0 = 2 * pr
                s_in = sb_ref[pr]
                y = (yacc_ref[c, :, pr * LANES:(pr + 1) * LANES]
                     + _dot(cg, s_in.astype(BF16)) * _pair_cols(e_in, nh + h0))
                ys.append(y * gate[:, pr * LANES:(pr + 1) * LANES])
                xpf = xbc_ref[:, pr * LANES:(pr + 1) * LANES].astype(F32)
                xw = (xpf * _pair_cols(w_st, nh + h0)).astype(BF16)
                sb_ref[pr] = s_in * pair_row(e_tot, nh + h0) + _dot_tn(bg, xw)
            yg = jnp.concatenate(ys, axis=-1)
            gw = pairs_per_group * LANES
            o_ref[:, g * gw:(g + 1) * gw] = _rms(
                yg, nw_ref[:, g * gw:(g + 1) * gw]).astype(BF16)


def _ssd(xbc, dt, z, p, batch, seq):
    t = SSD_CHUNK
    nc = seq // t

    def chunk_idx(b, pas, j):
        return b * nc + j + pas * (nc - 1 - 2 * j)

    def out_idx(b, pas, j):
        return b * nc + (nc - 1) - pas * j

    blk = lambda w, f: pl.BlockSpec((t, w), lambda b, pas, j: (f(b, pas, j), 0))
    n_pairs = SSD_HEADS // 2
    return pl.pallas_call(
        functools.partial(_ssd_body, n_chunks=nc),
        grid=(batch, 2, nc),
        in_specs=[blk(SSD_CONV_DIM, chunk_idx), blk(LANES, chunk_idx), blk(SSD_INNER, chunk_idx),
                  _resident((1, LANES)), _resident((1, SSD_INNER)), _resident((1, SSD_INNER))],
        out_specs=blk(SSD_INNER, out_idx),
        out_shape=jax.ShapeDtypeStruct((batch * seq, SSD_INNER), BF16),
        scratch_shapes=[pltpu.VMEM((nc, t, SSD_INNER), F32),
                        pltpu.VMEM((n_pairs, SSD_STATE, LANES), F32),
                        pltpu.VMEM((n_pairs, SSD_STATE, LANES), F32)],
        compiler_params=pltpu.CompilerParams(
            dimension_semantics=("parallel", "arbitrary", "arbitrary"),
            vmem_limit_bytes=40 * 1024 * 1024),
        name="ssd",
    )(xbc, dt, z, p["ssd_a_log"], p["ssd_d"], p["ssd_norm"])


def _swa_body(sink_ref, q_ref, kp_ref, kc_ref, kn_ref, vp_ref, vc_ref, vn_ref, nw_ref, o_ref,
              *, seq):
    j = pl.program_id(1)
    blk = q_ref.shape[0]
    kb = jnp.concatenate([kp_ref[...], kc_ref[...], kn_ref[...]], axis=0)
    vb = jnp.concatenate([vp_ref[...], vc_ref[...], vn_ref[...]], axis=0)
    qi = lax.broadcasted_iota(jnp.int32, (blk, 3 * blk), 0)
    kj = lax.broadcasted_iota(jnp.int32, (blk, 3 * blk), 1)
    rel = kj - blk - qi
    kpos = (j - 1) * blk + kj
    valid = (jnp.abs(rel) <= SWA_WINDOW) & (kpos >= 0) & (kpos < seq)
    lo_q = _lane_iota((blk, LANES)) < HALF
    lo_v = _lane_iota((3 * blk, LANES)) < HALF
    zero_q = jnp.zeros((blk, LANES), BF16)
    zero_v = jnp.zeros((3 * blk, LANES), BF16)
    v_half = (jnp.where(lo_v, vb, zero_v), jnp.where(lo_v, zero_v, vb))
    scale = SWA_HEAD_DIM ** -0.5
    outs = []
    for g in range(SWA_WIDTH // LANES):
        qg = q_ref[:, g * LANES:(g + 1) * LANES]
        acc = jnp.zeros((blk, LANES), F32)
        for half in range(2):
            head = g + 2 * half
            qm = jnp.where(lo_q, qg, zero_q) if half == 0 else jnp.where(lo_q, zero_q, qg)
            s = jnp.where(valid, _dot_nt(qm, kb) * scale, NEG_INF)
            sk = sink_ref[head]
            m = jnp.maximum(jnp.max(s, axis=-1, keepdims=True), sk)
            pexp = jnp.exp(s - m)
            den = jnp.sum(pexp, axis=-1, keepdims=True) + jnp.exp(sk - m)
            acc = acc + _dot((pexp / den).astype(BF16), v_half[half])
        outs.append(acc)
    y = jnp.concatenate(outs, axis=-1)
    o_ref[...] = _rms(y, nw_ref[...]).astype(BF16)


def _swa(q, k, v, sink, nw, batch, seq):
    blk = SWA_BLOCK
    nb = seq // blk
    cur = lambda b, j: (b * nb + j, 0)
    prv = lambda b, j: (b * nb + jnp.maximum(j - 1, 0), 0)
    nxt = lambda b, j: (b * nb + jnp.minimum(j + 1, nb - 1), 0)
    kv = lambda f: pl.BlockSpec((blk, LANES), f)
    return pl.pallas_call(
        functools.partial(_swa_body, seq=seq),
        grid=(batch, nb),
        in_specs=[pl.BlockSpec(memory_space=pltpu.SMEM),
                  pl.BlockSpec((blk, SWA_WIDTH), cur),
                  kv(prv), kv(cur), kv(nxt), kv(prv), kv(cur), kv(nxt),
                  _resident((1, SWA_WIDTH))],
        out_specs=pl.BlockSpec((blk, SWA_WIDTH), cur),
        out_shape=jax.ShapeDtypeStruct((batch * seq, SWA_WIDTH), BF16),
        compiler_params=pltpu.CompilerParams(dimension_semantics=("parallel", "parallel")),
        name="swa",
    )(sink, q, k, k, k, v, v, v, nw)


def _mla_body(q_ref, k_ref, v_ref, nw_ref, o_ref, *, key_chunk):
    tq = q_ref.shape[0]
    seq = k_ref.shape[0]
    qscale = (MLA_QK ** -0.5) * LOG2E
    outs = []
    for pair in range(MLA_HEADS // 2):
        acc_pair = jnp.zeros((tq, LANES), F32)
        for hh in range(2):
            h = 2 * pair + hh
            sl = slice(h * LANES, (h + 1) * LANES)
            qh = (q_ref[:, sl].astype(F32) * qscale).astype(BF16)
            m = jnp.full((tq, 1), NEG_INF, F32)
            l = jnp.zeros((tq, 1), F32)
            acc = jnp.zeros((tq, LANES), F32)
            for kc in range(seq // key_chunk):
                ks = slice(kc * key_chunk, (kc + 1) * key_chunk)
                s = _dot_nt(qh, k_ref[ks, sl])
                m_new = jnp.maximum(m, jnp.max(s, axis=-1, keepdims=True))
                alpha = jnp.exp2(m - m_new)
                pexp = jnp.exp2(s - m_new)
                l = alpha * l + jnp.sum(pexp, axis=-1, keepdims=True)
                acc = alpha * acc + _dot(pexp.astype(BF16), v_ref[ks, sl])
                m = m_new
            acc_pair = acc_pair + acc / l
        outs.append(acc_pair)
    y = jnp.concatenate(outs, axis=-1)
    o_ref[...] = _rms(y, nw_ref[...]).astype(BF16)


def _mla(q, k, v, nw, batch, seq, tq, key_chunk):
    nq = seq // tq
    w = MLA_HEADS * LANES
    full = pl.BlockSpec((seq, w), lambda b, i: (b, 0))
    return pl.pallas_call(
        functools.partial(_mla_body, key_chunk=key_chunk),
        grid=(batch, nq),
        in_specs=[pl.BlockSpec((tq, w), lambda b, i: (b * nq + i, 0)), full, full,
                  _resident((1, MLA_WIDTH))],
        out_specs=pl.BlockSpec((tq, MLA_WIDTH), lambda b, i: (b * nq + i, 0)),
        out_shape=jax.ShapeDtypeStruct((batch * seq, MLA_WIDTH), BF16),
        compiler_params=pltpu.CompilerParams(
            dimension_semantics=("parallel", "arbitrary"),
            vmem_limit_bytes=48 * 1024 * 1024),
        name="mla",
    )(q, k, v, nw)


def _outproj_body(x_ref, ys_ref, yw_ref, ym_ref, w_ref, o_ref):
    a = SSD_INNER
    b = a + SWA_WIDTH
    acc = _dot(ys_ref[...], w_ref[0:a, :])
    acc = acc + _dot(yw_ref[...], w_ref[a:b, :])
    acc = acc + _dot(ym_ref[...], w_ref[b:, :])
    o_ref[...] = x_ref[...] + acc


def _out_proj(x, ys, yw, ym, w, tm):
    n, d = x.shape
    row = lambda wd: pl.BlockSpec((tm, wd), lambda i: (i, 0))
    return pl.pallas_call(
        _outproj_body,
        grid=(n // tm,),
        in_specs=[row(d), row(SSD_INNER), row(SWA_WIDTH), row(MLA_WIDTH), _resident(w.shape)],
        out_specs=row(d),
        out_shape=jax.ShapeDtypeStruct((n, d), F32),
        compiler_params=pltpu.CompilerParams(dimension_semantics=("parallel",)),
        name="out_proj",
    )(x, ys, yw, ym, w)


def _rope_tables(seq):
    def angles(dim):
        inv = 1.0 / jnp.power(ROPE_THETA, jnp.arange(0, dim, 2, dtype=F32) / dim)
        return jnp.arange(seq, dtype=F32)[:, None] * inv[None, :]

    a64 = angles(SWA_HEAD_DIM)
    c, s = jnp.cos(a64), jnp.sin(a64)
    zero = jnp.zeros_like(c)
    c64 = jnp.concatenate([c, c, c, c], axis=-1)
    sa64 = jnp.concatenate([-s, zero, -s, zero], axis=-1)
    sb64 = jnp.concatenate([zero, s, zero, s], axis=-1)
    a32 = angles(MLA_ROPE)
    c, s = jnp.cos(a32), jnp.sin(a32)
    zero = jnp.zeros_like(c)
    ones = jnp.ones((seq, MLA_NOPE), F32)
    pad = jnp.zeros((seq, LANES - MLA_QK), F32)
    zn = jnp.zeros((seq, MLA_NOPE), F32)
    cm = jnp.concatenate([ones, c, c, pad], axis=-1)
    sam = jnp.concatenate([zn, -s, zero, pad], axis=-1)
    sbm = jnp.concatenate([zn, zero, s, pad], axis=-1)
    return jnp.stack([c64, sa64, sb64, cm, sam, sbm])


def _swa_head_perm(t, axis):
    parts = jnp.split(t, SWA_HEADS, axis=axis)
    return jnp.concatenate([parts[0], parts[2], parts[1], parts[3]], axis=axis)


def _prep_layer_params(l, ffn1_norm, ffn1_gate, ffn1_up, ffn1_down, mix_norm, w_in,
                       ssd_conv_w, ssd_conv_b, ssd_dt_bias, ssd_a_log, ssd_d, ssd_norm,
                       swa_q_norm, swa_k_norm, swa_sink, swa_out_norm,
                       mla_q_lat_norm, mla_w_uq, mla_kv_norm, mla_w_ukv, mla_q_norm, mla_k_norm,
                       mla_out_norm, w_out, ffn2_norm, ffn2_gate, ffn2_up, ffn2_down):
    d = w_in.shape[1]
    row = lambda v: v.reshape(1, -1).astype(F32)
    pad_lanes = lambda v, n: jnp.pad(v, [(0, 0)] * (v.ndim - 1) + [(0, n - v.shape[-1])])

    def ffn_w(gate, up, down):
        f = gate.shape[-1]
        nc = f // FFN_CHUNK
        split = lambda w: w.astype(BF16).reshape(d, nc, FFN_CHUNK).transpose(1, 0, 2)
        return split(gate), split(up), down.astype(BF16).reshape(nc, FFN_CHUNK, d)

    wi = w_in[l]
    o = 0
    offs = {}
    for name, size in (("z", SSD_INNER), ("xbc", SSD_CONV_DIM), ("dt", 2 * SSD_HEADS),
                       ("swq", SWA_WIDTH), ("swk", LANES), ("swv", LANES),
                       ("mlq", MLA_Q_RANK), ("ckv", MLA_KV_RANK), ("kr", MLA_ROPE)):
        offs[name] = wi[:, o:o + size]
        o += size
    zc = lambda n: jnp.zeros((d, n), wi.dtype)
    misc = jnp.concatenate([offs["dt"], zc(MLA_NOPE - 2 * SSD_HEADS), offs["kr"],
                            zc(LANES - MLA_QK)], axis=-1)
    w_in_p = jnp.concatenate([offs["z"], offs["xbc"], _swa_head_perm(offs["swq"], 1),
                              offs["swk"], offs["swv"], offs["mlq"], offs["ckv"], misc],
                             axis=-1).astype(BF16)

    wuq = mla_w_uq[l].reshape(MLA_Q_RANK, MLA_HEADS, MLA_QK)
    wuq = pad_lanes(wuq, LANES).reshape(MLA_Q_RANK, MLA_HEADS * LANES).astype(BF16)
    wukv = mla_w_ukv[l].reshape(MLA_KV_RANK, MLA_HEADS, MLA_NOPE + MLA_V)
    knope = pad_lanes(wukv[:, :, :MLA_NOPE], LANES)
    vv = wukv[:, :, MLA_NOPE:]
    zv = jnp.zeros_like(vv)
    even = (jnp.arange(MLA_HEADS) % 2 == 0)[None, :, None]
    vpad = jnp.concatenate([jnp.where(even, vv, zv), jnp.where(even, zv, vv)], axis=-1)
    wukv_p = jnp.concatenate([knope.reshape(MLA_KV_RANK, -1), vpad.reshape(MLA_KV_RANK, -1)],
                             axis=-1).astype(BF16)

    wo = w_out[l]
    a, b = SSD_INNER, SSD_INNER + SWA_WIDTH
    w_out_p = jnp.concatenate([wo[:a], _swa_head_perm(wo[a:b], 0), wo[b:]], axis=0).astype(BF16)

    two = lambda v: jnp.concatenate([v, v], axis=-1)
    return {
        "ffn1": (row(ffn1_norm[l]),) + ffn_w(ffn1_gate[l], ffn1_up[l], ffn1_down[l]),
        "ffn2": (row(ffn2_norm[l]),) + ffn_w(ffn2_gate[l], ffn2_up[l], ffn2_down[l]),
        "w_in": w_in_p,
        "mix_norm": row(mix_norm[l]),
        "conv_w": ssd_conv_w[l].astype(F32),
        "conv_b": row(ssd_conv_b[l]),
        "dt_bias": pad_lanes(row(ssd_dt_bias[l]), LANES),
        "swa_q_gain": row(two(swa_q_norm[l])),
        "swa_k_gain": row(two(swa_k_norm[l])),
        "mla_qlat_norm": row(mla_q_lat_norm[l]),
        "w_uq": wuq,
        "mla_kv_norm": row(mla_kv_norm[l]),
        "w_ukv": wukv_p,
        "mla_q_gain": pad_lanes(row(mla_q_norm[l]), LANES),
        "mla_k_gain": pad_lanes(row(mla_k_norm[l]), LANES),
        "ssd_a_log": pad_lanes(row(ssd_a_log[l]), LANES),
        "ssd_d": row(jnp.repeat(ssd_d[l], SSD_HEAD_DIM)),
        "ssd_norm": row(ssd_norm[l]),
        "swa_sink": swa_sink[l].astype(F32),
        "swa_out_norm": row(_swa_head_perm(swa_out_norm[l], 0)),
        "mla_out_norm": row(mla_out_norm[l]),
        "w_out": w_out_p,
    }


def _tiles(seq):
    pick = lambda pref: math.gcd(seq, pref)
    return {"ffn": pick(512), "proj": pick(512), "mla_q": pick(256), "mla_k": pick(1024)}


def kernel(x, ffn1_norm, ffn1_gate, ffn1_up, ffn1_down, mix_norm, w_in, ssd_conv_w, ssd_conv_b, ssd_dt_bias, ssd_a_log, ssd_d, ssd_norm, swa_q_norm, swa_k_norm, swa_sink, swa_out_norm, mla_q_lat_norm, mla_w_uq, mla_kv_norm, mla_w_ukv, mla_q_norm, mla_k_norm, mla_out_norm, w_out, ffn2_norm, ffn2_gate, ffn2_up, ffn2_down):
    batch, seq, d = x.shape
    assert seq % SSD_CHUNK == 0 and seq % SWA_BLOCK == 0
    params = (ffn1_norm, ffn1_gate, ffn1_up, ffn1_down, mix_norm, w_in, ssd_conv_w, ssd_conv_b,
              ssd_dt_bias, ssd_a_log, ssd_d, ssd_norm, swa_q_norm, swa_k_norm, swa_sink,
              swa_out_norm, mla_q_lat_norm, mla_w_uq, mla_kv_norm, mla_w_ukv, mla_q_norm,
              mla_k_norm, mla_out_norm, w_out, ffn2_norm, ffn2_gate, ffn2_up, ffn2_down)
    tl = _tiles(seq)
    tabs = _rope_tables(seq)
    xf = x.reshape(batch * seq, d).astype(F32)
    for l in range(w_in.shape[0]):
        p = _prep_layer_params(l, *params)
        xf = _ffn(xf, *p["ffn1"], tm=tl["ffn"])
        z, xbc, dt, swq, swk, swv, mq, mk, mv = _in_proj(xf, tabs, p, seq, tl["proj"])
        y_ssd = _ssd(xbc, dt, z, p, batch, seq)
        y_swa = _swa(swq, swk, swv, p["swa_sink"], p["swa_out_norm"], batch, seq)
        y_mla = _mla(mq, mk, mv, p["mla_out_norm"], batch, seq, tl["mla_q"], tl["mla_k"])
        xf = _out_proj(xf, y_ssd, y_swa, y_mla, p["w_out"], tl["proj"])
        xf = _ffn(xf, *p["ffn2"], tm=tl["ffn"])
    return xf.reshape(batch, seq, d).astype(x.dtype)
```

```python
import functools
import math

import jax
import jax.numpy as jnp
from jax import lax
from jax.experimental import pallas as pl
from jax.experimental.pallas import tpu as pltpu

F32 = jnp.float32
BF16 = jnp.bfloat16

LANES = 128
HALF = LANES // 2

SSD_HEADS = 8
SSD_HEAD_DIM = 64
SSD_INNER = SSD_HEADS * SSD_HEAD_DIM
SSD_GROUPS = 2
SSD_STATE = 128
SSD_CONV = 5
SSD_CHUNK = 256
SSD_CONV_DIM = SSD_INNER + 2 * SSD_GROUPS * SSD_STATE
SWA_HEADS = 4
SWA_KV_HEADS = 2
SWA_HEAD_DIM = 64
SWA_WINDOW = 128
SWA_BLOCK = 128
SWA_WIDTH = SWA_HEADS * SWA_HEAD_DIM
MLA_HEADS = 4
MLA_Q_RANK = 256
MLA_KV_RANK = 128
MLA_NOPE = 64
MLA_ROPE = 32
MLA_QK = MLA_NOPE + MLA_ROPE
MLA_V = 64
MLA_WIDTH = MLA_HEADS * MLA_V
ROPE_THETA = 10000.0
EPS = 1e-6

C_Z = 0
C_XBC = C_Z + SSD_INNER
C_SWQ = C_XBC + SSD_CONV_DIM
C_SWK = C_SWQ + SWA_WIDTH
C_SWV = C_SWK + LANES
C_MLQ = C_SWV + LANES
C_CKV = C_MLQ + MLA_Q_RANK
C_MISC = C_CKV + MLA_KV_RANK
C_END = C_MISC + LANES
CONV_HALO = 16
FFN_CHUNK = 256
NEG_INF = float("-inf")
LOG2E = math.log2(math.e)
MLA_BOUND_LIMIT = 48.0
MLA_BOUND_SLACK = 1.01


def _rms(x, w):
    ms = jnp.mean(x * x, axis=-1, keepdims=True)
    return x * lax.rsqrt(ms + EPS) * w


def _silu(x):
    return x / (1.0 + jnp.exp(-x))


def _dot(a, b):
    return jnp.dot(a, b, preferred_element_type=F32)


def _dot_nt(a, b):
    return lax.dot_general(a, b, (((1,), (1,)), ((), ())), preferred_element_type=F32)


def _dot_tn(a, b):
    return lax.dot_general(a, b, (((0,), (0,)), ((), ())), preferred_element_type=F32)


def _lane_iota(shape):
    return lax.broadcasted_iota(jnp.int32, shape, len(shape) - 1)


def _resident(shape):
    nd = len(shape)
    return pl.BlockSpec(shape, lambda *_: (0,) * nd)


def _ffn_body(x_ref, nw_ref, wg_ref, wu_ref, wd_ref, o_ref, xn_ref, acc_ref):
    xn_ref[...] = _rms(x_ref[...], nw_ref[...]).astype(BF16)
    for c in range(wg_ref.shape[1] // FFN_CHUNK):
        cols = slice(c * FFN_CHUNK, (c + 1) * FFN_CHUNK)
        xn = xn_ref[...]
        g = _dot(xn, wg_ref[:, cols])
        u = _dot(xn, wu_ref[:, cols])
        h = (_silu(g) * u).astype(BF16)
        part = _dot(h, wd_ref[cols, :])
        if c == 0:
            acc_ref[...] = part
        else:
            acc_ref[...] += part
    o_ref[...] = x_ref[...] + 0.5 * acc_ref[...]


def _ffn(x, nw, wg, wu, wd, tm):
    n, d = x.shape
    row = pl.BlockSpec((tm, d), lambda i: (i, 0))
    return pl.pallas_call(
        _ffn_body,
        grid=(n // tm,),
        in_specs=[row, _resident((1, d)), _resident(wg.shape), _resident(wu.shape),
                  _resident(wd.shape)],
        out_specs=row,
        out_shape=jax.ShapeDtypeStruct((n, d), F32),
        scratch_shapes=[pltpu.VMEM((tm, d), BF16), pltpu.VMEM((tm, d), F32)],
        compiler_params=pltpu.CompilerParams(
            dimension_semantics=("parallel",), vmem_limit_bytes=56 * 1024 * 1024),
        name="ffn",
    )(x, nw, wg, wu, wd)


def _rope(y, c, sa, sb, shift):
    return (y * c + pltpu.roll(y, LANES - shift, 1) * sa + pltpu.roll(y, shift, 1) * sb)


def _head_sum_matrix(width):
    r = lax.broadcasted_iota(jnp.int32, (LANES, LANES), 0)
    c = lax.broadcasted_iota(jnp.int32, (LANES, LANES), 1)
    same = (r < width) if width > HALF else ((r < HALF) == (c < HALF))
    return jnp.where(same, 1.0, 0.0).astype(BF16)


def _head_norm_rope(x, gain, c, sa, sb, hsum, width, rope_dim, post_scale):
    ms = _dot((x * x).astype(BF16), hsum) / float(width)
    y = x * lax.rsqrt(ms + EPS) * gain
    out = _rope(y, c, sa, sb, rope_dim // 2)
    return out if post_scale == 1.0 else out * post_scale


def _inproj_body(x_ref, xp_ref, xnx_ref, tab_ref, w_ref, mixw_ref, convw_ref, convb_ref,
                 dtb_ref, swqg_ref, swkg_ref, qlw_ref, wuq_ref, kvw_ref, wukv_ref,
                 mqg_ref, mkg_ref,
                 z_ref, xbc_ref, dt_ref, swq_ref, swk_ref, swv_ref, mq_ref, mk_ref, mv_ref,
                 buf_ref, *, tiles_per_seq):
    i = pl.program_id(0)
    tm = x_ref.shape[0]
    pos_tile = i % tiles_per_seq
    mixw = mixw_ref[...]
    hn = _rms(x_ref[...], mixw).astype(BF16)

    z_ref[...] = _dot(hn, w_ref[:, C_Z:C_XBC]).astype(BF16)

    wx = w_ref[:, C_XBC:C_SWQ]
    hp = _rms(xp_ref[...], mixw).astype(BF16)
    hx = _rms(xnx_ref[...], mixw).astype(BF16)
    keep_prev = (pos_tile > 0).astype(F32)
    keep_next = (pos_tile < tiles_per_seq - 1).astype(F32)
    buf_ref[0:CONV_HALO, :] = _dot(hp, wx) * keep_prev
    buf_ref[CONV_HALO:CONV_HALO + tm, :] = _dot(hn, wx)
    buf_ref[CONV_HALO + tm:2 * CONV_HALO + tm, :] = _dot(hx, wx) * keep_next
    conv = jnp.broadcast_to(convb_ref[...], (tm, SSD_CONV_DIM))
    for k in range(SSD_CONV):
        off = CONV_HALO - SSD_CONV // 2 + k
        conv = conv + convw_ref[k:k + 1, :] * buf_ref[off:off + tm, :]
    xbc_ref[...] = _silu(conv).astype(BF16)

    ckv_misc = _dot(hn, w_ref[:, C_CKV:C_END])
    ckv, misc = ckv_misc[:, :LANES], ckv_misc[:, LANES:]
    dtv = misc + dtb_ref[...]
    dt_ref[...] = jnp.maximum(dtv, 0.0) + jnp.log1p(jnp.exp(-jnp.abs(dtv)))

    swa_rope = (tab_ref[0], tab_ref[1], tab_ref[2], _head_sum_matrix(SWA_HEAD_DIM),
                SWA_HEAD_DIM, SWA_HEAD_DIM)
    qq = _dot(hn, w_ref[:, C_SWQ:C_SWK])
    for g in range(SWA_WIDTH // LANES):
        sl = slice(g * LANES, (g + 1) * LANES)
        swq_ref[:, sl] = _head_norm_rope(qq[:, sl], swqg_ref[...], *swa_rope,
                                         SWA_HEAD_DIM ** -0.5).astype(BF16)
    kv_swa = _dot(hn, w_ref[:, C_SWK:C_MLQ])
    swk_ref[...] = _head_norm_rope(kv_swa[:, :LANES], swkg_ref[...], *swa_rope, 1.0).astype(BF16)
    swv_ref[...] = kv_swa[:, LANES:].astype(BF16)

    mla_rope = (tab_ref[3], tab_ref[4], tab_ref[5], _head_sum_matrix(MLA_QK), MLA_QK, MLA_ROPE)
    ql = _dot(hn, w_ref[:, C_MLQ:C_CKV])
    q = _dot(_rms(ql, qlw_ref[...]).astype(BF16), wuq_ref[...])
    kv = _dot(_rms(ckv, kvw_ref[...]).astype(BF16), wukv_ref[...])
    lane = _lane_iota(misc.shape)
    kr = jnp.where((lane >= MLA_NOPE) & (lane < MLA_QK), misc, 0.0)
    for h in range(MLA_HEADS):
        sl = slice(h * LANES, (h + 1) * LANES)
        mq_ref[:, sl] = _head_norm_rope(q[:, sl], mqg_ref[...], *mla_rope,
                                        MLA_QK ** -0.5 * LOG2E).astype(BF16)
        mk_ref[:, sl] = _head_norm_rope(kv[:, sl] + kr, mkg_ref[...], *mla_rope, 1.0).astype(BF16)
        vh = kv[:, (MLA_HEADS + h) * LANES:(MLA_HEADS + h + 1) * LANES]
        ones_lane = HALF if h % 2 == 0 else 0
        mv_ref[:, sl] = jnp.where(lane == ones_lane, 1.0, vh).astype(BF16)


def _in_proj(x, tabs, p, seq, tm):
    n, d = x.shape
    tiles_per_seq = seq // tm
    halo_per_tile = tm // CONV_HALO
    n_halo = n // CONV_HALO
    row = lambda w: pl.BlockSpec((tm, w), lambda i: (i, 0))
    in_specs = [
        row(d),
        pl.BlockSpec((CONV_HALO, d), lambda i: (jnp.maximum(i * halo_per_tile - 1, 0), 0)),
        pl.BlockSpec((CONV_HALO, d),
                     lambda i: (jnp.minimum((i + 1) * halo_per_tile, n_halo - 1), 0)),
        pl.BlockSpec((6, tm, LANES), lambda i: (0, i % tiles_per_seq, 0)),
    ]
    names = ("w_in", "mix_norm", "conv_w", "conv_b", "dt_bias", "swa_q_gain", "swa_k_gain",
             "mla_qlat_norm", "w_uq", "mla_kv_norm", "w_ukv", "mla_q_gain", "mla_k_gain")
    args = [p[k] for k in names]
    in_specs += [_resident(a.shape) for a in args]
    widths = (SSD_INNER, SSD_CONV_DIM, LANES, SWA_WIDTH, LANES, LANES,
              MLA_HEADS * LANES, MLA_HEADS * LANES, MLA_HEADS * LANES)
    dts = (BF16, BF16, F32, BF16, BF16, BF16, BF16, BF16, BF16)
    return pl.pallas_call(
        functools.partial(_inproj_body, tiles_per_seq=tiles_per_seq),
        grid=(n // tm,),
        in_specs=in_specs,
        out_specs=[row(w) for w in widths],
        out_shape=[jax.ShapeDtypeStruct((n, w), t) for w, t in zip(widths, dts)],
        scratch_shapes=[pltpu.VMEM((tm + 2 * CONV_HALO, SSD_CONV_DIM), F32)],
        compiler_params=pltpu.CompilerParams(
            dimension_semantics=("parallel",), vmem_limit_bytes=48 * 1024 * 1024),
        name="in_proj",
    )(x, x, x, tabs, *args)


def _pair_cols(v, h0):
    t = v.shape[0]
    lo = _lane_iota((t, LANES)) < HALF
    return jnp.where(lo, jnp.broadcast_to(v[:, h0:h0 + 1], (t, LANES)),
                     jnp.broadcast_to(v[:, h0 + 1:h0 + 2], (t, LANES)))


def _ssd_body(xbc_ref, dt_ref, z_ref, alog_ref, dskip_ref, nw_ref, o_ref,
              yacc_ref, sf_ref, sb_ref, *, n_chunks):
    pas = pl.program_id(1)
    j = pl.program_id(2)
    t = xbc_ref.shape[0]
    nh = SSD_HEADS
    pairs_per_group = nh // SSD_GROUPS // 2

    lane = _lane_iota((1, LANES))
    avec = jnp.where(lane < 2 * nh, -jnp.exp(alog_ref[...]), 0.0)
    dt = dt_ref[...]
    a = dt * avec
    row = lax.broadcasted_iota(jnp.int32, (t, t), 0)
    col = lax.broadcasted_iota(jnp.int32, (t, t), 1)
    tri = (col <= row).astype(F32)
    cum = jnp.dot(tri, a, preferred_element_type=F32, precision=lax.Precision.HIGHEST)
    excl = cum - a
    tot = cum[t - 1:t, :]
    lo128 = _lane_iota((t, LANES)) < HALF
    lo1 = lane < HALF

    def pair_row(v, h0):
        return jnp.where(lo1, v[:, h0:h0 + 1], v[:, h0 + 1:h0 + 2])

    @pl.when(pas == 0)
    def _forward():
        @pl.when(j == 0)
        def _():
            sf_ref[...] = jnp.zeros_like(sf_ref)

        cum_t = cum.T
        excl_t = excl.T
        dt_t = dt.T
        e_in = jnp.exp(cum)
        w_st = jnp.exp(tot - cum) * dt
        e_tot = jnp.exp(tot)
        for g in range(SSD_GROUPS):
            bg = xbc_ref[:, SSD_INNER + g * SSD_STATE:SSD_INNER + (g + 1) * SSD_STATE]
            cg = xbc_ref[:, SSD_INNER + (SSD_GROUPS + g) * SSD_STATE:
                         SSD_INNER + (SSD_GROUPS + g + 1) * SSD_STATE]
            cb = _dot_nt(cg, bg)
            for pp in range(pairs_per_group):
                pr = g * pairs_per_group + pp
                h0 = 2 * pr
                xp = xbc_ref[:, pr * LANES:(pr + 1) * LANES]
                xpf = xp.astype(F32)
                y = dskip_ref[:, pr * LANES:(pr + 1) * LANES] * xpf
                for hh in range(2):
                    h = h0 + hh
                    df = cum[:, h:h + 1] - cum_t[h:h + 1, :]
                    db = excl_t[nh + h:nh + h + 1, :] - excl[:, nh + h:nh + h + 1]
                    lf = jnp.exp(jnp.where(row >= col, df, NEG_INF))
                    ub = jnp.exp(jnp.where(col >= row, db, NEG_INF))
                    m = cb * (lf * dt_t[h:h + 1, :] + ub * dt_t[nh + h:nh + h + 1, :])
                    keep = lo128 if hh == 0 else jnp.logical_not(lo128)
                    xm = jnp.where(keep, xp, jnp.zeros_like(xp))
                    y = y + _dot(m.astype(BF16), xm)
                s_in = sf_ref[pr]
                y = y + _dot(cg, s_in.astype(BF16)) * _pair_cols(e_in, h0)
                yacc_ref[j, :, pr * LANES:(pr + 1) * LANES] = y
                xw = (xpf * _pair_cols(w_st, h0)).astype(BF16)
                sf_ref[pr] = s_in * pair_row(e_tot, h0) + _dot_tn(bg, xw)

    @pl.when(pas == 1)
    def _backward():
        @pl.when(j == 0)
        def _():
            sb_ref[...] = jnp.zeros_like(sb_ref)

        c = n_chunks - 1 - j
        e_in = jnp.exp(tot - excl)
        w_st = jnp.exp(excl) * dt
        e_tot = jnp.exp(tot)
        zt = z_ref[...].astype(F32)
        gate = _silu(zt)
        for g in range(SSD_GROUPS):
            bg = xbc_ref[:, SSD_INNER + g * SSD_STATE:SSD_INNER + (g + 1) * SSD_STATE]
            cg = xbc_ref[:, SSD_INNER + (SSD_GROUPS + g) * SSD_STATE:
                         SSD_INNER + (SSD_GROUPS + g + 1) * SSD_STATE]
            ys = []
            for pp in range(pairs_per_group):
                pr = g * pairs_per_group + pp
                h0 = 2 * pr
                s_in = sb_ref[pr]
                y = (yacc_ref[c, :, pr * LANES:(pr + 1) * LANES]
                     + _dot(cg, s_in.astype(BF16)) * _pair_cols(e_in, nh + h0))
                ys.append(y * gate[:, pr * LANES:(pr + 1) * LANES])
                xpf = xbc_ref[:, pr * LANES:(pr + 1) * LANES].astype(F32)
                xw = (xpf * _pair_cols(w_st, nh + h0)).astype(BF16)
                sb_ref[pr] = s_in * pair_row(e_tot, nh + h0) + _dot_tn(bg, xw)
            yg = jnp.concatenate(ys, axis=-1)
            gw = pairs_per_group * LANES
            o_ref[:, g * gw:(g + 1) * gw] = _rms(
                yg, nw_ref[:, g * gw:(g + 1) * gw]).astype(BF16)


def _ssd(xbc, dt, z, p, batch, seq):
    t = SSD_CHUNK
    nc = seq // t

    def chunk_idx(b, pas, j):
        return b * nc + j + pas * (nc - 1 - 2 * j)

    def out_idx(b, pas, j):
        return b * nc + (nc - 1) - pas * j

    blk = lambda w, f: pl.BlockSpec((t, w), lambda b, pas, j: (f(b, pas, j), 0))
    n_pairs = SSD_HEADS // 2
    return pl.pallas_call(
        functools.partial(_ssd_body, n_chunks=nc),
        grid=(batch, 2, nc),
        in_specs=[blk(SSD_CONV_DIM, chunk_idx), blk(LANES, chunk_idx), blk(SSD_INNER, chunk_idx),
                  _resident((1, LANES)), _resident((1, SSD_INNER)), _resident((1, SSD_INNER))],
        out_specs=blk(SSD_INNER, out_idx),
        out_shape=jax.ShapeDtypeStruct((batch * seq, SSD_INNER), BF16),
        scratch_shapes=[pltpu.VMEM((nc, t, SSD_INNER), F32),
                        pltpu.VMEM((n_pairs, SSD_STATE, LANES), F32),
                        pltpu.VMEM((n_pairs, SSD_STATE, LANES), F32)],
        compiler_params=pltpu.CompilerParams(
            dimension_semantics=("parallel", "arbitrary", "arbitrary"),
            vmem_limit_bytes=40 * 1024 * 1024),
        name="ssd",
    )(xbc, dt, z, p["ssd_a_log"], p["ssd_d"], p["ssd_norm"])


def _swa_body(sink_ref, q_ref, kp_ref, kc_ref, kn_ref, vp_ref, vc_ref, vn_ref, nw_ref, o_ref,
              *, seq):
    j = pl.program_id(1)
    blk = SWA_BLOCK
    n_blk = q_ref.shape[0] // blk
    kb = jnp.concatenate([kp_ref[...], kc_ref[...], kn_ref[...]], axis=0)
    vb = jnp.concatenate([vp_ref[...], vc_ref[...], vn_ref[...]], axis=0)
    qi = lax.broadcasted_iota(jnp.int32, (blk, 3 * blk), 0)
    kj = lax.broadcasted_iota(jnp.int32, (blk, 3 * blk), 1)
    band = jnp.abs(kj - blk - qi) <= SWA_WINDOW
    lo = _lane_iota((blk, LANES)) < HALF
    lo_v = _lane_iota(vb.shape) < HALF
    zero_q = jnp.zeros((blk, LANES), BF16)
    v_half = (jnp.where(lo_v, vb, jnp.zeros_like(vb)), jnp.where(lo_v, jnp.zeros_like(vb), vb))
    for t in range(n_blk):
        kpos = (j * n_blk + t - 1) * blk + kj
        valid = band & (kpos >= 0) & (kpos < seq)
        rows = slice(t * blk, (t + 1) * blk)
        keys = slice(t * blk, (t + 3) * blk)
        outs = []
        for g in range(SWA_WIDTH // LANES):
            qg = q_ref[rows, g * LANES:(g + 1) * LANES]
            acc = jnp.zeros((blk, LANES), F32)
            inv = []
            for half in range(2):
                head = g + 2 * half
                qm = jnp.where(lo, qg, zero_q) if half == 0 else jnp.where(lo, zero_q, qg)
                s = jnp.where(valid, _dot_nt(qm, kb[keys]), NEG_INF)
                sk = sink_ref[head]
                m = jnp.maximum(jnp.max(s, axis=-1, keepdims=True), sk)
                pexp = jnp.exp(s - m)
                den = jnp.sum(pexp, axis=-1, keepdims=True) + jnp.exp(sk - m)
                inv.append(1.0 / den)
                acc = acc + _dot(pexp.astype(BF16), v_half[half][keys])
            outs.append(acc * jnp.where(lo, inv[0], inv[1]))
        y = jnp.concatenate(outs, axis=-1)
        o_ref[rows, :] = _rms(y, nw_ref[...]).astype(BF16)


def _swa(q, k, v, sink, nw, batch, seq, n_blk):
    blk = SWA_BLOCK
    nb = seq // blk
    ns = nb // n_blk
    cur = lambda b, j: (b * ns + j, 0)
    prv = lambda b, j: (b * nb + jnp.maximum(j * n_blk - 1, 0), 0)
    nxt = lambda b, j: (b * nb + jnp.minimum((j + 1) * n_blk, nb - 1), 0)
    edge = lambda f: pl.BlockSpec((blk, LANES), f)
    main = pl.BlockSpec((n_blk * blk, LANES), cur)
    return pl.pallas_call(
        functools.partial(_swa_body, seq=seq),
        grid=(batch, ns),
        in_specs=[pl.BlockSpec(memory_space=pltpu.SMEM),
                  pl.BlockSpec((n_blk * blk, SWA_WIDTH), cur),
                  edge(prv), main, edge(nxt), edge(prv), main, edge(nxt),
                  _resident((1, SWA_WIDTH))],
        out_specs=pl.BlockSpec((n_blk * blk, SWA_WIDTH), cur),
        out_shape=jax.ShapeDtypeStruct((batch * seq, SWA_WIDTH), BF16),
        compiler_params=pltpu.CompilerParams(dimension_semantics=("parallel", "parallel")),
        name="swa",
    )(sink, q, k, k, k, v, v, v, nw)


def _mla_body(q_ref, k_ref, v_ref, nw_ref, o_ref, knorm_ref, *, key_chunk):
    i = pl.program_id(1)
    tq = q_ref.shape[0]
    seq = k_ref.shape[0]
    heads = [slice(h * LANES, (h + 1) * LANES) for h in range(MLA_HEADS)]
    chunks = [slice(c * key_chunk, (c + 1) * key_chunk) for c in range(seq // key_chunk)]
    lo = _lane_iota((tq, LANES)) < HALF

    @pl.when(i == 0)
    def _key_norms():
        for h, sl in enumerate(heads):
            kf = k_ref[:, sl].astype(F32)
            ss = jnp.max(jnp.sum(kf * kf, axis=-1, keepdims=True), axis=0, keepdims=True)
            knorm_ref[h:h + 1, :] = jnp.broadcast_to(jnp.sqrt(ss), (1, LANES))

    bounds = []
    for h, sl in enumerate(heads):
        qf = q_ref[:, sl].astype(F32)
        qn = jnp.sqrt(jnp.sum(qf * qf, axis=-1, keepdims=True))
        bounds.append(qn * knorm_ref[h:h + 1, 0:1] * MLA_BOUND_SLACK)
    bmax = jnp.max(jnp.maximum(jnp.maximum(bounds[0], bounds[1]),
                               jnp.maximum(bounds[2], bounds[3])))
    use_bound = bmax <= MLA_BOUND_LIMIT

    def denom(acc, h):
        return acc[:, HALF:HALF + 1] if h % 2 == 0 else acc[:, 0:1]

    def finish(accs, dens):
        outs = [jnp.where(lo, accs[2 * p] / dens[2 * p], accs[2 * p + 1] / dens[2 * p + 1])
                for p in range(MLA_HEADS // 2)]
        y = jnp.concatenate(outs, axis=-1)
        o_ref[...] = _rms(y, nw_ref[...]).astype(BF16)

    @pl.when(use_bound)
    def _bounded():
        accs = []
        for h, sl in enumerate(heads):
            qh = q_ref[:, sl]
            acc = None
            for ks in chunks:
                pexp = jnp.exp2(_dot_nt(qh, k_ref[ks, sl]) - bounds[h]).astype(BF16)
                part = _dot(pexp, v_ref[ks, sl])
                acc = part if acc is None else acc + part
            accs.append(acc)
        finish(accs, [denom(a, h) for h, a in enumerate(accs)])

    @pl.when(jnp.logical_not(use_bound))
    def _online():
        accs, dens = [], []
        for h, sl in enumerate(heads):
            qh = q_ref[:, sl]
            m = jnp.full((tq, 1), NEG_INF, F32)
            acc = jnp.zeros((tq, LANES), F32)
            for ks in chunks:
                s = _dot_nt(qh, k_ref[ks, sl])
                m_new = jnp.maximum(m, jnp.max(s, axis=-1, keepdims=True))
                pexp = jnp.exp2(s - m_new).astype(BF16)
                acc = jnp.exp2(m - m_new) * acc + _dot(pexp, v_ref[ks, sl])
                m = m_new
            accs.append(acc)
            dens.append(denom(acc, h))
        finish(accs, dens)


def _mla(q, k, v, nw, batch, seq, tq, key_chunk):
    nq = seq // tq
    w = MLA_HEADS * LANES
    full = pl.BlockSpec((seq, w), lambda b, i: (b, 0))
    return pl.pallas_call(
        functools.partial(_mla_body, key_chunk=key_chunk),
        grid=(batch, nq),
        in_specs=[pl.BlockSpec((tq, w), lambda b, i: (b * nq + i, 0)), full, full,
                  _resident((1, MLA_WIDTH))],
        out_specs=pl.BlockSpec((tq, MLA_WIDTH), lambda b, i: (b * nq + i, 0)),
        out_shape=jax.ShapeDtypeStruct((batch * seq, MLA_WIDTH), BF16),
        scratch_shapes=[pltpu.VMEM((8, LANES), F32)],
        compiler_params=pltpu.CompilerParams(
            dimension_semantics=("arbitrary", "arbitrary"),
            vmem_limit_bytes=48 * 1024 * 1024),
        name="mla",
    )(q, k, v, nw)


def _outproj_body(x_ref, ys_ref, yw_ref, ym_ref, w_ref, o_ref):
    a = SSD_INNER
    b = a + SWA_WIDTH
    acc = _dot(ys_ref[...], w_ref[0:a, :])
    acc = acc + _dot(yw_ref[...], w_ref[a:b, :])
    acc = acc + _dot(ym_ref[...], w_ref[b:, :])
    o_ref[...] = x_ref[...] + acc


def _out_proj(x, ys, yw, ym, w, tm):
    n, d = x.shape
    row = lambda wd: pl.BlockSpec((tm, wd), lambda i: (i, 0))
    return pl.pallas_call(
        _outproj_body,
        grid=(n // tm,),
        in_specs=[row(d), row(SSD_INNER), row(SWA_WIDTH), row(MLA_WIDTH), _resident(w.shape)],
        out_specs=row(d),
        out_shape=jax.ShapeDtypeStruct((n, d), F32),
        compiler_params=pltpu.CompilerParams(dimension_semantics=("parallel",)),
        name="out_proj",
    )(x, ys, yw, ym, w)


def _rope_tables(seq):
    def angles(dim):
        inv = 1.0 / jnp.power(ROPE_THETA, jnp.arange(0, dim, 2, dtype=F32) / dim)
        return jnp.arange(seq, dtype=F32)[:, None] * inv[None, :]

    a64 = angles(SWA_HEAD_DIM)
    c, s = jnp.cos(a64), jnp.sin(a64)
    zero = jnp.zeros_like(c)
    c64 = jnp.concatenate([c, c, c, c], axis=-1)
    sa64 = jnp.concatenate([-s, zero, -s, zero], axis=-1)
    sb64 = jnp.concatenate([zero, s, zero, s], axis=-1)
    a32 = angles(MLA_ROPE)
    c, s = jnp.cos(a32), jnp.sin(a32)
    zero = jnp.zeros_like(c)
    ones = jnp.ones((seq, MLA_NOPE), F32)
    pad = jnp.zeros((seq, LANES - MLA_QK), F32)
    zn = jnp.zeros((seq, MLA_NOPE), F32)
    cm = jnp.concatenate([ones, c, c, pad], axis=-1)
    sam = jnp.concatenate([zn, -s, zero, pad], axis=-1)
    sbm = jnp.concatenate([zn, zero, s, pad], axis=-1)
    return jnp.stack([c64, sa64, sb64, cm, sam, sbm])


def _swa_head_perm(t, axis):
    parts = jnp.split(t, SWA_HEADS, axis=axis)
    return jnp.concatenate([parts[0], parts[2], parts[1], parts[3]], axis=axis)


def _prep_layer_params(l, ffn1_norm, ffn1_gate, ffn1_up, ffn1_down, mix_norm, w_in,
                       ssd_conv_w, ssd_conv_b, ssd_dt_bias, ssd_a_log, ssd_d, ssd_norm,
                       swa_q_norm, swa_k_norm, swa_sink, swa_out_norm,
                       mla_q_lat_norm, mla_w_uq, mla_kv_norm, mla_w_ukv, mla_q_norm, mla_k_norm,
                       mla_out_norm, w_out, ffn2_norm, ffn2_gate, ffn2_up, ffn2_down):
    d = w_in.shape[1]
    row = lambda v: v.reshape(1, -1).astype(F32)
    pad_lanes = lambda v, n: jnp.pad(v, [(0, 0)] * (v.ndim - 1) + [(0, n - v.shape[-1])])

    def ffn_w(gate, up, down):
        assert gate.shape[-1] % FFN_CHUNK == 0
        return gate.astype(BF16), up.astype(BF16), down.astype(BF16)

    wi = w_in[l]
    o = 0
    offs = {}
    for name, size in (("z", SSD_INNER), ("xbc", SSD_CONV_DIM), ("dt", 2 * SSD_HEADS),
                       ("swq", SWA_WIDTH), ("swk", LANES), ("swv", LANES),
                       ("mlq", MLA_Q_RANK), ("ckv", MLA_KV_RANK), ("kr", MLA_ROPE)):
        offs[name] = wi[:, o:o + size]
        o += size
    zc = lambda n: jnp.zeros((d, n), wi.dtype)
    misc = jnp.concatenate([offs["dt"], zc(MLA_NOPE - 2 * SSD_HEADS), offs["kr"],
                            zc(LANES - MLA_QK)], axis=-1)
    w_in_p = jnp.concatenate([offs["z"], offs["xbc"], _swa_head_perm(offs["swq"], 1),
                              offs["swk"], offs["swv"], offs["mlq"], offs["ckv"], misc],
                             axis=-1).astype(BF16)

    wuq = mla_w_uq[l].reshape(MLA_Q_RANK, MLA_HEADS, MLA_QK)
    wuq = pad_lanes(wuq, LANES).reshape(MLA_Q_RANK, MLA_HEADS * LANES).astype(BF16)
    wukv = mla_w_ukv[l].reshape(MLA_KV_RANK, MLA_HEADS, MLA_NOPE + MLA_V)
    knope = pad_lanes(wukv[:, :, :MLA_NOPE], LANES)
    vv = wukv[:, :, MLA_NOPE:]
    zv = jnp.zeros_like(vv)
    even = (jnp.arange(MLA_HEADS) % 2 == 0)[None, :, None]
    vpad = jnp.concatenate([jnp.where(even, vv, zv), jnp.where(even, zv, vv)], axis=-1)
    wukv_p = jnp.concatenate([knope.reshape(MLA_KV_RANK, -1), vpad.reshape(MLA_KV_RANK, -1)],
                             axis=-1).astype(BF16)

    wo = w_out[l]
    a, b = SSD_INNER, SSD_INNER + SWA_WIDTH
    w_out_p = jnp.concatenate([wo[:a], _swa_head_perm(wo[a:b], 0), wo[b:]], axis=0).astype(BF16)

    two = lambda v: jnp.concatenate([v, v], axis=-1)
    return {
        "ffn1": (row(ffn1_norm[l]),) + ffn_w(ffn1_gate[l], ffn1_up[l], ffn1_down[l]),
        "ffn2": (row(ffn2_norm[l]),) + ffn_w(ffn2_gate[l], ffn2_up[l], ffn2_down[l]),
        "w_in": w_in_p,
        "mix_norm": row(mix_norm[l]),
        "conv_w": ssd_conv_w[l].astype(F32),
        "conv_b": row(ssd_conv_b[l]),
        "dt_bias": pad_lanes(row(ssd_dt_bias[l]), LANES),
        "swa_q_gain": row(two(swa_q_norm[l])),
        "swa_k_gain": row(two(swa_k_norm[l])),
        "mla_qlat_norm": row(mla_q_lat_norm[l]),
        "w_uq": wuq,
        "mla_kv_norm": row(mla_kv_norm[l]),
        "w_ukv": wukv_p,
        "mla_q_gain": pad_lanes(row(mla_q_norm[l]), LANES),
        "mla_k_gain": pad_lanes(row(mla_k_norm[l]), LANES),
        "ssd_a_log": pad_lanes(row(ssd_a_log[l]), LANES),
        "ssd_d": row(jnp.repeat(ssd_d[l], SSD_HEAD_DIM)),
        "ssd_norm": row(ssd_norm[l]),
        "swa_sink": swa_sink[l].astype(F32),
        "swa_out_norm": row(_swa_head_perm(swa_out_norm[l], 0)),
        "mla_out_norm": row(mla_out_norm[l]),
        "w_out": w_out_p,
    }


def _tiles(seq):
    pick = lambda pref: math.gcd(seq, pref)
    return {"ffn": pick(512), "proj": pick(512), "mla_q": pick(256), "mla_k": pick(1024),
            "swa_blocks": pick(4 * SWA_BLOCK) // SWA_BLOCK}


def kernel(x, ffn1_norm, ffn1_gate, ffn1_up, ffn1_down, mix_norm, w_in, ssd_conv_w, ssd_conv_b, ssd_dt_bias, ssd_a_log, ssd_d, ssd_norm, swa_q_norm, swa_k_norm, swa_sink, swa_out_norm, mla_q_lat_norm, mla_w_uq, mla_kv_norm, mla_w_ukv, mla_q_norm, mla_k_norm, mla_out_norm, w_out, ffn2_norm, ffn2_gate, ffn2_up, ffn2_down):
    batch, seq, d = x.shape
    assert seq % SSD_CHUNK == 0 and seq % SWA_BLOCK == 0
    params = (ffn1_norm, ffn1_gate, ffn1_up, ffn1_down, mix_norm, w_in, ssd_conv_w, ssd_conv_b,
              ssd_dt_bias, ssd_a_log, ssd_d, ssd_norm, swa_q_norm, swa_k_norm, swa_sink,
              swa_out_norm, mla_q_lat_norm, mla_w_uq, mla_kv_norm, mla_w_ukv, mla_q_norm,
              mla_k_norm, mla_out_norm, w_out, ffn2_norm, ffn2_gate, ffn2_up, ffn2_down)
    tl = _tiles(seq)
    tabs = _rope_tables(seq)
    xf = x.reshape(batch * seq, d).astype(F32)
    for l in range(w_in.shape[0]):
        p = _prep_layer_params(l, *params)
        xf = _ffn(xf, *p["ffn1"], tm=tl["ffn"])
        z, xbc, dt, swq, swk, swv, mq, mk, mv = _in_proj(xf, tabs, p, seq, tl["proj"])
        y_ssd = _ssd(xbc, dt, z, p, batch, seq)
        y_swa = _swa(swq, swk, swv, p["swa_sink"], p["swa_out_norm"], batch, seq,
                     tl["swa_blocks"])
        y_mla = _mla(mq, mk, mv, p["mla_out_norm"], batch, seq, tl["mla_q"], tl["mla_k"])
        xf = _out_proj(xf, y_ssd, y_swa, y_mla, p["w_out"], tl["proj"])
        xf = _ffn(xf, *p["ffn2"], tm=tl["ffn"])
    return xf.reshape(batch, seq, d).astype(x.dtype)
```

```python
import functools
import math

import jax
import jax.numpy as jnp
from jax import lax
from jax.experimental import pallas as pl
from jax.experimental.pallas import tpu as pltpu

F32 = jnp.float32
BF16 = jnp.bfloat16

LANES = 128
HALF = LANES // 2

SSD_HEADS = 8
SSD_HEAD_DIM = 64
SSD_INNER = SSD_HEADS * SSD_HEAD_DIM
SSD_GROUPS = 2
SSD_STATE = 128
SSD_CONV = 5
SSD_CHUNK = 256
SSD_CONV_DIM = SSD_INNER + 2 * SSD_GROUPS * SSD_STATE
SWA_HEADS = 4
SWA_KV_HEADS = 2
SWA_HEAD_DIM = 64
SWA_WINDOW = 128
SWA_BLOCK = 128
SWA_WIDTH = SWA_HEADS * SWA_HEAD_DIM
MLA_HEADS = 4
MLA_Q_RANK = 256
MLA_KV_RANK = 128
MLA_NOPE = 64
MLA_ROPE = 32
MLA_QK = MLA_NOPE + MLA_ROPE
MLA_V = 64
MLA_WIDTH = MLA_HEADS * MLA_V
ROPE_THETA = 10000.0
EPS = 1e-6

C_Z = 0
C_XBC = C_Z + SSD_INNER
C_SWQ = C_XBC + SSD_CONV_DIM
C_SWK = C_SWQ + SWA_WIDTH
C_SWV = C_SWK + LANES
C_MLQ = C_SWV + LANES
C_CKV = C_MLQ + MLA_Q_RANK
C_MISC = C_CKV + MLA_KV_RANK
C_END = C_MISC + LANES
CONV_HALO = 16
FFN_CHUNK = 256
NEG_INF = float("-inf")
LOG_FLOOR = -1e30
LOG2E = math.log2(math.e)
MLA_BOUND_LIMIT = 48.0
MLA_BOUND_SLACK = 1.01


def _rms(x, w):
    ms = jnp.mean(x * x, axis=-1, keepdims=True)
    return x * lax.rsqrt(ms + EPS) * w


def _silu(x):
    return x / (1.0 + jnp.exp(-x))


def _dot(a, b):
    return jnp.dot(a, b, preferred_element_type=F32)


def _dot_nt(a, b):
    return lax.dot_general(a, b, (((1,), (1,)), ((), ())), preferred_element_type=F32)


def _dot_tn(a, b):
    return lax.dot_general(a, b, (((0,), (0,)), ((), ())), preferred_element_type=F32)


def _lane_iota(shape):
    return lax.broadcasted_iota(jnp.int32, shape, len(shape) - 1)


def _resident(shape):
    nd = len(shape)
    return pl.BlockSpec(shape, lambda *_: (0,) * nd)


def _cast_body(*refs):
    n = len(refs) // 2
    for src, dst in zip(refs[:n], refs[n:]):
        dst[...] = src[...].astype(dst.dtype)


def _cast_ffn_weights(l, gate, up, down, steps=4):
    _, d, f = gate.shape
    col = pl.BlockSpec((None, d // steps, f), lambda i: (l, i, 0))
    rowb = pl.BlockSpec((None, f // steps, d), lambda i: (l, i, 0))
    out_col = pl.BlockSpec((d // steps, f), lambda i: (i, 0))
    out_row = pl.BlockSpec((f // steps, d), lambda i: (i, 0))
    return pl.pallas_call(
        _cast_body,
        grid=(steps,),
        in_specs=[col, col, rowb],
        out_specs=[out_col, out_col, out_row],
        out_shape=[jax.ShapeDtypeStruct((d, f), BF16), jax.ShapeDtypeStruct((d, f), BF16),
                   jax.ShapeDtypeStruct((f, d), BF16)],
        compiler_params=pltpu.CompilerParams(
            dimension_semantics=("parallel",), vmem_limit_bytes=40 * 1024 * 1024),
        name="cast_ffn_weights",
    )(gate, up, down)


def _ffn_body(*refs, fuse_out_proj):
    if fuse_out_proj:
        x_ref, ys_ref, yw_ref, ym_ref, wo_ref = refs[:5]
        a, b = SSD_INNER, SSD_INNER + SWA_WIDTH
        x = (x_ref[...] + _dot(ys_ref[...], wo_ref[0:a, :]) + _dot(yw_ref[...], wo_ref[a:b, :])
             + _dot(ym_ref[...], wo_ref[b:, :]))
        refs = refs[5:]
    else:
        x_ref = refs[0]
        x = x_ref[...]
        refs = refs[1:]
    nw_ref, wg_ref, wu_ref, wd_ref, o_ref, xn_ref, acc_ref = refs
    o_ref[...] = x
    xn_ref[...] = _rms(x, nw_ref[...]).astype(BF16)
    for c in range(wg_ref.shape[1] // FFN_CHUNK):
        cols = slice(c * FFN_CHUNK, (c + 1) * FFN_CHUNK)
        xn = xn_ref[...]
        g = _dot(xn, wg_ref[:, cols])
        u = _dot(xn, wu_ref[:, cols])
        h = (_silu(g) * u).astype(BF16)
        part = _dot(h, wd_ref[cols, :])
        if c == 0:
            acc_ref[...] = part
        else:
            acc_ref[...] += part
    o_ref[...] += 0.5 * acc_ref[...]


def _ffn(x, nw, wg, wu, wd, tm, mixer=None):
    n, d = x.shape
    row = lambda w: pl.BlockSpec((tm, w), lambda i: (i, 0))
    args, specs = [x], [row(d)]
    if mixer is not None:
        args += list(mixer)
        specs += [row(a.shape[1]) for a in mixer[:3]] + [_resident(mixer[3].shape)]
    args += [nw, wg, wu, wd]
    specs += [_resident((1, d)), _resident(wg.shape), _resident(wu.shape), _resident(wd.shape)]
    return pl.pallas_call(
        functools.partial(_ffn_body, fuse_out_proj=mixer is not None),
        grid=(n // tm,),
        in_specs=specs,
        out_specs=row(d),
        out_shape=jax.ShapeDtypeStruct((n, d), F32),
        scratch_shapes=[pltpu.VMEM((tm, d), BF16), pltpu.VMEM((tm, d), F32)],
        compiler_params=pltpu.CompilerParams(
            dimension_semantics=("parallel",), vmem_limit_bytes=56 * 1024 * 1024),
        name="ffn",
    )(*args)


def _rope(y, c, sa, sb, shift):
    return (y * c + pltpu.roll(y, LANES - shift, 1) * sa + pltpu.roll(y, shift, 1) * sb)


def _head_sum_matrix(width):
    r = lax.broadcasted_iota(jnp.int32, (LANES, LANES), 0)
    c = lax.broadcasted_iota(jnp.int32, (LANES, LANES), 1)
    same = (r < width) if width > HALF else ((r < HALF) == (c < HALF))
    return jnp.where(same, 1.0, 0.0).astype(BF16)


def _head_norm_rope(x, sumsq, gain, c, sa, sb, width, rope_dim, post_scale):
    y = x * lax.rsqrt(sumsq / float(width) + EPS) * gain
    out = _rope(y, c, sa, sb, rope_dim // 2)
    return out if post_scale == 1.0 else out * post_scale


def _inproj_body(x_ref, xp_ref, xnx_ref, tab_ref, w_ref, mixw_ref, convw_ref, convb_ref,
                 dtb_ref, swqg_ref, swkg_ref, qlw_ref, wuq_ref, kvw_ref, wukv_ref,
                 mqg_ref, mkg_ref,
                 z_ref, xbc_ref, dt_ref, swq_ref, swk_ref, swv_ref, mq_ref, mk_ref, mv_ref,
                 buf_ref, *, tiles_per_seq):
    i = pl.program_id(0)
    tm = x_ref.shape[0]
    pos_tile = i % tiles_per_seq
    mixw = mixw_ref[...]
    hn = _rms(x_ref[...], mixw).astype(BF16)
    hp = _rms(xp_ref[...], mixw).astype(BF16)
    hx = _rms(xnx_ref[...], mixw).astype(BF16)
    groups = lambda v: [v[:, g * LANES:(g + 1) * LANES] for g in range(v.shape[1] // LANES)]

    wx = w_ref[:, C_XBC:C_SWQ]
    z = _dot(hn, w_ref[:, C_Z:C_XBC])
    xbc_prev, xbc_main, xbc_next = _dot(hp, wx), _dot(hn, wx), _dot(hx, wx)
    ckv, misc = groups(_dot(hn, w_ref[:, C_CKV:C_END]))
    swa_q = groups(_dot(hn, w_ref[:, C_SWQ:C_SWK]))
    swa_k, swa_v = groups(_dot(hn, w_ref[:, C_SWK:C_MLQ]))
    ql = _dot(hn, w_ref[:, C_MLQ:C_CKV])

    q_up = groups(_dot(_rms(ql, qlw_ref[...]).astype(BF16), wuq_ref[...]))
    kv_up = groups(_dot(_rms(ckv, kvw_ref[...]).astype(BF16), wukv_ref[...]))
    lane = _lane_iota(misc.shape)
    kr = jnp.where((lane >= MLA_NOPE) & (lane < MLA_QK), misc, 0.0)
    mla_q = q_up
    mla_k = [k_nope + kr for k_nope in kv_up[:MLA_HEADS]]
    mla_v = kv_up[MLA_HEADS:]

    sum64, sum96 = _head_sum_matrix(SWA_HEAD_DIM), _head_sum_matrix(MLA_QK)
    sumsq = lambda xs, m: [_dot((v * v).astype(BF16), m) for v in xs]
    ss_swa_q, ss_swa_k = sumsq(swa_q, sum64), sumsq([swa_k], sum64)
    ss_mla_q, ss_mla_k = sumsq(mla_q, sum96), sumsq(mla_k, sum96)

    z_ref[...] = z.astype(BF16)

    keep_prev = (pos_tile > 0).astype(F32)
    keep_next = (pos_tile < tiles_per_seq - 1).astype(F32)
    buf_ref[0:CONV_HALO, :] = xbc_prev * keep_prev
    buf_ref[CONV_HALO:CONV_HALO + tm, :] = xbc_main
    buf_ref[CONV_HALO + tm:2 * CONV_HALO + tm, :] = xbc_next * keep_next
    conv = jnp.broadcast_to(convb_ref[...], (tm, SSD_CONV_DIM))
    for k in range(SSD_CONV):
        off = CONV_HALO - SSD_CONV // 2 + k
        conv = conv + convw_ref[k:k + 1, :] * buf_ref[off:off + tm, :]
    xbc_ref[...] = _silu(conv).astype(BF16)

    dtv = misc + dtb_ref[...]
    dt_ref[...] = jnp.maximum(dtv, 0.0) + jnp.log1p(jnp.exp(-jnp.abs(dtv)))

    swa_rope = (tab_ref[0], tab_ref[1], tab_ref[2], SWA_HEAD_DIM, SWA_HEAD_DIM)
    for g, (v, ss) in enumerate(zip(swa_q, ss_swa_q)):
        swq_ref[:, g * LANES:(g + 1) * LANES] = _head_norm_rope(
            v, ss, swqg_ref[...], *swa_rope, SWA_HEAD_DIM ** -0.5 * LOG2E).astype(BF16)
    swk_ref[...] = _head_norm_rope(swa_k, ss_swa_k[0], swkg_ref[...], *swa_rope, 1.0).astype(BF16)
    swv_ref[...] = swa_v.astype(BF16)

    mla_rope = (tab_ref[3], tab_ref[4], tab_ref[5], MLA_QK, MLA_ROPE)
    for h in range(MLA_HEADS):
        sl = slice(h * LANES, (h + 1) * LANES)
        mq_ref[:, sl] = _head_norm_rope(mla_q[h], ss_mla_q[h], mqg_ref[...], *mla_rope,
                                        MLA_QK ** -0.5 * LOG2E).astype(BF16)
        mk_ref[:, sl] = _head_norm_rope(mla_k[h], ss_mla_k[h], mkg_ref[...], *mla_rope,
                                        1.0).astype(BF16)
        ones_lane = HALF if h % 2 == 0 else 0
        mv_ref[:, sl] = jnp.where(lane == ones_lane, 1.0, mla_v[h]).astype(BF16)


def _in_proj(x, tabs, p, seq, tm):
    n, d = x.shape
    tiles_per_seq = seq // tm
    halo_per_tile = tm // CONV_HALO
    n_halo = n // CONV_HALO
    row = lambda w: pl.BlockSpec((tm, w), lambda i: (i, 0))
    in_specs = [
        row(d),
        pl.BlockSpec((CONV_HALO, d), lambda i: (jnp.maximum(i * halo_per_tile - 1, 0), 0)),
        pl.BlockSpec((CONV_HALO, d),
                     lambda i: (jnp.minimum((i + 1) * halo_per_tile, n_halo - 1), 0)),
        pl.BlockSpec((6, tm, LANES), lambda i: (0, i % tiles_per_seq, 0)),
    ]
    names = ("w_in", "mix_norm", "conv_w", "conv_b", "dt_bias", "swa_q_gain", "swa_k_gain",
             "mla_qlat_norm", "w_uq", "mla_kv_norm", "w_ukv", "mla_q_gain", "mla_k_gain")
    args = [p[k] for k in names]
    in_specs += [_resident(a.shape) for a in args]
    widths = (SSD_INNER, SSD_CONV_DIM, LANES, SWA_WIDTH, LANES, LANES,
              MLA_HEADS * LANES, MLA_HEADS * LANES, MLA_HEADS * LANES)
    dts = (BF16, BF16, F32, BF16, BF16, BF16, BF16, BF16, BF16)
    return pl.pallas_call(
        functools.partial(_inproj_body, tiles_per_seq=tiles_per_seq),
        grid=(n // tm,),
        in_specs=in_specs,
        out_specs=[row(w) for w in widths],
        out_shape=[jax.ShapeDtypeStruct((n, w), t) for w, t in zip(widths, dts)],
        scratch_shapes=[pltpu.VMEM((tm + 2 * CONV_HALO, SSD_CONV_DIM), F32)],
        compiler_params=pltpu.CompilerParams(
            dimension_semantics=("parallel",), vmem_limit_bytes=48 * 1024 * 1024),
        name="in_proj",
    )(x, x, x, tabs, *args)


def _split_bf16(v, n):
    pieces = []
    for _ in range(n - 1):
        p = v.astype(BF16)
        pieces.append(p)
        v = v - p.astype(F32)
    pieces.append(v.astype(BF16))
    return pieces


def _pack_lanes(pieces, width):
    lane = _lane_iota(pieces[0].shape)
    out = jnp.zeros(pieces[0].shape, F32)
    for k, p in enumerate(pieces):
        pf = p.astype(F32)
        if k:
            pf = pltpu.roll(pf, k * width, 1)
        out = jnp.where((lane >= k * width) & (lane < (k + 1) * width), pf, out)
    return out.astype(BF16)


def _ssd_expand_matrices():
    nd = 2 * SSD_HEADS
    rows = jnp.arange(LANES)[:, None]
    head = jnp.arange(SSD_INNER)[None, :] // SSD_HEAD_DIM
    mats = []
    for direction in range(2):
        for first_piece in (0, 2):
            d = direction * SSD_HEADS + head
            hit = (rows == first_piece * nd + d) | (rows == (first_piece + 1) * nd + d)
            mats.append(hit)
    return jnp.stack(mats).astype(BF16)


def _ssd_body(xbc_ref, dt_ref, z_ref, alog_ref, dskip_ref, nw_ref, expand_ref, o_ref,
              yacc_ref, sf_ref, sb_ref, cols_ref, *, n_chunks):
    pas = pl.program_id(1)
    j = pl.program_id(2)
    t = xbc_ref.shape[0]
    nh = SSD_HEADS

    gw = SSD_INNER // SSD_GROUPS
    hpg = nh // SSD_GROUPS
    nd = 2 * nh
    lane = _lane_iota((t, LANES))

    def b_of(g):
        return xbc_ref[:, SSD_INNER + g * SSD_STATE:SSD_INNER + (g + 1) * SSD_STATE]

    def c_of(g):
        return xbc_ref[:, SSD_INNER + (SSD_GROUPS + g) * SSD_STATE:
                       SSD_INNER + (SSD_GROUPS + g + 1) * SSD_STATE]

    @pl.when(pas == 0)
    def _forward():
        @pl.when(j == 0)
        def _():
            sf_ref[...] = jnp.zeros_like(sf_ref)

        fwd = lane < nh
        dt = dt_ref[...]
        avec = jnp.where(_lane_iota((1, LANES)) < nd, -jnp.exp(alog_ref[...]), 0.0)
        a = dt * avec
        row = lax.broadcasted_iota(jnp.int32, (t, t), 0)
        col = lax.broadcasted_iota(jnp.int32, (t, t), 1)
        tri = jnp.where(col <= row, 1.0, 0.0).astype(BF16)
        cum = sum(_dot(tri, part) for part in _split_bf16(a, 3))
        excl = cum - a
        tot = cum[t - 1:t, :]

        e_in = jnp.exp(jnp.where(fwd, cum, tot - excl))
        w_st = jnp.exp(jnp.where(fwd, tot - cum, excl)) * dt
        e1, e2 = _split_bf16(e_in, 2)
        w1, w2 = _split_bf16(w_st, 2)
        cols = _pack_lanes([e1, e2, w1, w2], nd)
        cols_ref[j] = cols

        r = jnp.where(fwd, cum, -excl) * LOG2E
        cc = r - jnp.maximum(jnp.log(dt), LOG_FLOOR) * LOG2E
        ones = jnp.ones((t, LANES), F32)
        lhs = jnp.where(lane < 3 * nd, _pack_lanes(_split_bf16(r, 3), nd),
                        jnp.where(lane < 6 * nd, ones, 0.0).astype(BF16))
        cc_t = cc.T[0:nd, :]
        c1, c2, c3 = _split_bf16(cc_t, 3)
        rhs = jnp.concatenate(
            [jnp.ones((3 * nd, t), BF16), -c1, -c2, -c3, jnp.zeros((LANES - 6 * nd, t), BF16)],
            axis=0)
        sub = lax.broadcasted_iota(jnp.int32, (LANES, t), 0) % nd

        e_x = _dot(cols, expand_ref[0])
        w_x = _dot(cols, expand_ref[1])
        cbs = [_dot_nt(c_of(g), b_of(g)) for g in range(SSD_GROUPS)]
        offs = [_dot(c_of(g), sf_ref[g].astype(BF16)) for g in range(SSD_GROUPS)]
        exps = [_dot(lhs, jnp.where(sub == idx, rhs, jnp.zeros_like(rhs))) for idx in range(nd)]

        th = t // 2
        quad = lambda v, qi, qj: v[qi * th:(qi + 1) * th, qj * th:(qj + 1) * th]
        qrow = lax.broadcasted_iota(jnp.int32, (th, th), 0)
        qcol = lax.broadcasted_iota(jnp.int32, (th, th), 1)
        on_low = qrow >= qcol
        on_up = qcol >= qrow

        def mixing(h, cb):
            df, db = exps[h], exps[nh + h]
            diag = [quad(cb, q, q) * (jnp.exp2(jnp.where(on_low, quad(df, q, q), NEG_INF))
                                      + jnp.exp2(jnp.where(on_up, quad(db, q, q), NEG_INF)))
                    for q in range(2)]
            upper = quad(cb, 0, 1) * jnp.exp2(quad(db, 0, 1))
            lower = quad(cb, 1, 0) * jnp.exp2(quad(df, 1, 0))
            return jnp.concatenate([jnp.concatenate([diag[0], upper], axis=1),
                                    jnp.concatenate([lower, diag[1]], axis=1)], axis=0)

        glane = _lane_iota((t, gw))
        for g in range(SSD_GROUPS):
            gs = slice(g * gw, (g + 1) * gw)
            xg = xbc_ref[:, gs]
            xgf = xg.astype(F32)
            xw = (xgf * w_x[:, gs]).astype(BF16)
            sf_ref[g] = sf_ref[g] * e_x[t - 1:t, gs] + _dot_tn(b_of(g), xw)
            y = dskip_ref[:, gs] * xgf + offs[g] * e_x[:, gs]
            for hh in range(hpg):
                m = mixing(g * hpg + hh, cbs[g]).astype(BF16)
                mine = (glane >= hh * SSD_HEAD_DIM) & (glane < (hh + 1) * SSD_HEAD_DIM)
                y = y + _dot(m, jnp.where(mine, xg, jnp.zeros_like(xg)))
            yacc_ref[j, :, gs] = y

    @pl.when(pas == 1)
    def _backward():
        @pl.when(j == 0)
        def _():
            sb_ref[...] = jnp.zeros_like(sb_ref)

        c = n_chunks - 1 - j
        cols = cols_ref[c]
        e_x = _dot(cols, expand_ref[2])
        w_x = _dot(cols, expand_ref[3])
        gate = _silu(z_ref[...].astype(F32))
        for g in range(SSD_GROUPS):
            gs = slice(g * gw, (g + 1) * gw)
            bg, cg = b_of(g), c_of(g)
            s_in = sb_ref[g]
            y = yacc_ref[c, :, gs] + _dot(cg, s_in.astype(BF16)) * e_x[:, gs]
            o_ref[:, gs] = _rms(y * gate[:, gs], nw_ref[:, gs]).astype(BF16)
            xw = (xbc_ref[:, gs].astype(F32) * w_x[:, gs]).astype(BF16)
            sb_ref[g] = s_in * e_x[0:1, gs] + _dot_tn(bg, xw)


def _ssd(xbc, dt, z, p, batch, seq):
    t = SSD_CHUNK
    nc = seq // t

    def chunk_idx(b, pas, j):
        return b * nc + j + pas * (nc - 1 - 2 * j)

    def out_idx(b, pas, j):
        return b * nc + (nc - 1) - pas * j

    blk = lambda w, f: pl.BlockSpec((t, w), lambda b, pas, j: (f(b, pas, j), 0))
    expand = _ssd_expand_matrices()
    state = pltpu.VMEM((SSD_GROUPS, SSD_STATE, SSD_INNER // SSD_GROUPS), F32)
    return pl.pallas_call(
        functools.partial(_ssd_body, n_chunks=nc),
        grid=(batch, 2, nc),
        in_specs=[blk(SSD_CONV_DIM, chunk_idx), blk(LANES, chunk_idx), blk(SSD_INNER, chunk_idx),
                  _resident((1, LANES)), _resident((1, SSD_INNER)), _resident((1, SSD_INNER)),
                  _resident(expand.shape)],
        out_specs=blk(SSD_INNER, out_idx),
        out_shape=jax.ShapeDtypeStruct((batch * seq, SSD_INNER), BF16),
        scratch_shapes=[pltpu.VMEM((nc, t, SSD_INNER), F32), state, state,
                        pltpu.VMEM((nc, t, LANES), BF16)],
        compiler_params=pltpu.CompilerParams(
            dimension_semantics=("arbitrary", "arbitrary", "arbitrary"),
            vmem_limit_bytes=40 * 1024 * 1024),
        name="ssd",
    )(xbc, dt, z, p["ssd_a_log"], p["ssd_d"], p["ssd_norm"], expand)


def _swa_body(sink_ref, q_ref, kp_ref, kc_ref, kn_ref, vp_ref, vc_ref, vn_ref, nw_ref, o_ref):
    assert SWA_WINDOW == SWA_BLOCK
    j = pl.program_id(1)
    blk = SWA_BLOCK
    n_blk = q_ref.shape[0] // blk
    kb = jnp.concatenate([kp_ref[...], kc_ref[...], kn_ref[...]], axis=0)
    vb = jnp.concatenate([vp_ref[...], vc_ref[...], vn_ref[...]], axis=0)
    qi = lax.broadcasted_iota(jnp.int32, (blk, blk), 0)
    kj = lax.broadcasted_iota(jnp.int32, (blk, blk), 1)
    lo = _lane_iota((blk, LANES)) < HALF
    lo_v = _lane_iota(vb.shape) < HALF
    zero_q = jnp.zeros((blk, LANES), BF16)
    lane_v = _lane_iota(vb.shape)
    unit = lambda at: jnp.where(lane_v == at, 1.0, 0.0).astype(BF16)
    v_half = (jnp.where(lo_v, vb, unit(HALF)), jnp.where(lo_v, unit(0), vb))
    first = jnp.where(j == 0, blk, 0)
    last = jnp.where(j == pl.num_programs(1) - 1, blk, 0)
    items = [(t, g, half) for t in range(n_blk) for g in range(SWA_WIDTH // LANES)
             for half in range(2)]
    logits = {}
    for t, g, half in items:
        qg = q_ref[t * blk:(t + 1) * blk, g * LANES:(g + 1) * LANES]
        qm = jnp.where(lo, qg, zero_q) if half == 0 else jnp.where(lo, zero_q, qg)
        logits[t, g, half] = _dot_nt(qm, kb[t * blk:(t + 3) * blk])
    scaled = {}
    for t, g, half in items:
        keep_prev = kj >= (qi + first if t == 0 else qi)
        keep_next = kj <= (qi - last if t == n_blk - 1 else qi)
        s = logits[t, g, half]
        s_prev = jnp.where(keep_prev, s[:, :blk], NEG_INF)
        s_own = s[:, blk:2 * blk]
        s_next = jnp.where(keep_next, s[:, 2 * blk:], NEG_INF)
        sk = sink_ref[g + 2 * half] * LOG2E
        m = jnp.max(jnp.maximum(jnp.maximum(s_prev, s_own), s_next), axis=-1, keepdims=True)
        m = jnp.maximum(m, sk)
        pexp = jnp.concatenate([jnp.exp2(v - m) for v in (s_prev, s_own, s_next)],
                               axis=-1).astype(BF16)
        acc = _dot(pexp, v_half[half][t * blk:(t + 3) * blk])
        ones_lane = HALF if half == 0 else 0
        den = acc[:, ones_lane:ones_lane + 1] + jnp.exp2(sk - m)
        scaled[t, g, half] = acc / den
    for t in range(n_blk):
        y = jnp.concatenate([jnp.where(lo, scaled[t, g, 0], scaled[t, g, 1])
                             for g in range(SWA_WIDTH // LANES)], axis=-1)
        o_ref[t * blk:(t + 1) * blk, :] = _rms(y, nw_ref[...]).astype(BF16)


def _swa(q, k, v, sink, nw, batch, seq, n_blk):
    blk = SWA_BLOCK
    nb = seq // blk
    ns = nb // n_blk
    cur = lambda b, j: (b * ns + j, 0)
    prv = lambda b, j: (b * nb + jnp.maximum(j * n_blk - 1, 0), 0)
    nxt = lambda b, j: (b * nb + jnp.minimum((j + 1) * n_blk, nb - 1), 0)
    edge = lambda f: pl.BlockSpec((blk, LANES), f)
    main = pl.BlockSpec((n_blk * blk, LANES), cur)
    return pl.pallas_call(
        _swa_body,
        grid=(batch, ns),
        in_specs=[pl.BlockSpec(memory_space=pltpu.SMEM),
                  pl.BlockSpec((n_blk * blk, SWA_WIDTH), cur),
                  edge(prv), main, edge(nxt), edge(prv), main, edge(nxt),
                  _resident((1, SWA_WIDTH))],
        out_specs=pl.BlockSpec((n_blk * blk, SWA_WIDTH), cur),
        out_shape=jax.ShapeDtypeStruct((batch * seq, SWA_WIDTH), BF16),
        compiler_params=pltpu.CompilerParams(dimension_semantics=("parallel", "parallel")),
        name="swa",
    )(sink, q, k, k, k, v, v, v, nw)


def _mla_body(q_ref, k_ref, v_ref, nw_ref, o_ref, knorm_ref, *, key_chunk):
    i = pl.program_id(1)
    tq = q_ref.shape[0]
    seq = k_ref.shape[0]
    heads = [slice(h * LANES, (h + 1) * LANES) for h in range(MLA_HEADS)]
    chunks = [slice(c * key_chunk, (c + 1) * key_chunk) for c in range(seq // key_chunk)]
    lo = _lane_iota((tq, LANES)) < HALF

    @pl.when(i == 0)
    def _key_norms():
        for h, sl in enumerate(heads):
            kf = k_ref[:, sl].astype(F32)
            ss = jnp.max(jnp.sum(kf * kf, axis=-1, keepdims=True), axis=0, keepdims=True)
            knorm_ref[h:h + 1, :] = jnp.broadcast_to(jnp.sqrt(ss), (1, LANES))

    bounds = []
    for h, sl in enumerate(heads):
        qf = q_ref[:, sl].astype(F32)
        qn = jnp.sqrt(jnp.sum(qf * qf, axis=-1, keepdims=True))
        bounds.append(qn * knorm_ref[h:h + 1, 0:1] * MLA_BOUND_SLACK)
    bmax = jnp.max(jnp.maximum(jnp.maximum(bounds[0], bounds[1]),
                               jnp.maximum(bounds[2], bounds[3])))
    use_bound = bmax <= MLA_BOUND_LIMIT

    def denom(acc, h):
        return acc[:, HALF:HALF + 1] if h % 2 == 0 else acc[:, 0:1]

    def finish(accs, dens):
        outs = [jnp.where(lo, accs[2 * p] / dens[2 * p], accs[2 * p + 1] / dens[2 * p + 1])
                for p in range(MLA_HEADS // 2)]
        y = jnp.concatenate(outs, axis=-1)
        o_ref[...] = _rms(y, nw_ref[...]).astype(BF16)

    @pl.when(use_bound)
    def _bounded():
        accs = []
        for h, sl in enumerate(heads):
            qh = q_ref[:, sl]
            acc = None
            for ks in chunks:
                pexp = jnp.exp2(_dot_nt(qh, k_ref[ks, sl]) - bounds[h]).astype(BF16)
                part = _dot(pexp, v_ref[ks, sl])
                acc = part if acc is None else acc + part
            accs.append(acc)
        finish(accs, [denom(a, h) for h, a in enumerate(accs)])

    @pl.when(jnp.logical_not(use_bound))
    def _online():
        accs, dens = [], []
        for h, sl in enumerate(heads):
            qh = q_ref[:, sl]
            m = jnp.full((tq, 1), NEG_INF, F32)
            acc = jnp.zeros((tq, LANES), F32)
            for ks in chunks:
                s = _dot_nt(qh, k_ref[ks, sl])
                m_new = jnp.maximum(m, jnp.max(s, axis=-1, keepdims=True))
                pexp = jnp.exp2(s - m_new).astype(BF16)
                acc = jnp.exp2(m - m_new) * acc + _dot(pexp, v_ref[ks, sl])
                m = m_new
            accs.append(acc)
            dens.append(denom(acc, h))
        finish(accs, dens)


def _mla(q, k, v, nw, batch, seq, tq, key_chunk):
    nq = seq // tq
    w = MLA_HEADS * LANES
    full = pl.BlockSpec((seq, w), lambda b, i: (b, 0))
    return pl.pallas_call(
        functools.partial(_mla_body, key_chunk=key_chunk),
        grid=(batch, nq),
        in_specs=[pl.BlockSpec((tq, w), lambda b, i: (b * nq + i, 0)), full, full,
                  _resident((1, MLA_WIDTH))],
        out_specs=pl.BlockSpec((tq, MLA_WIDTH), lambda b, i: (b * nq + i, 0)),
        out_shape=jax.ShapeDtypeStruct((batch * seq, MLA_WIDTH), BF16),
        scratch_shapes=[pltpu.VMEM((8, LANES), F32)],
        compiler_params=pltpu.CompilerParams(
            dimension_semantics=("arbitrary", "arbitrary"),
            vmem_limit_bytes=48 * 1024 * 1024),
        name="mla",
    )(q, k, v, nw)


def _rope_tables(seq):
    def angles(dim):
        inv = 1.0 / jnp.power(ROPE_THETA, jnp.arange(0, dim, 2, dtype=F32) / dim)
        return jnp.arange(seq, dtype=F32)[:, None] * inv[None, :]

    a64 = angles(SWA_HEAD_DIM)
    c, s = jnp.cos(a64), jnp.sin(a64)
    zero = jnp.zeros_like(c)
    c64 = jnp.concatenate([c, c, c, c], axis=-1)
    sa64 = jnp.concatenate([-s, zero, -s, zero], axis=-1)
    sb64 = jnp.concatenate([zero, s, zero, s], axis=-1)
    a32 = angles(MLA_ROPE)
    c, s = jnp.cos(a32), jnp.sin(a32)
    zero = jnp.zeros_like(c)
    ones = jnp.ones((seq, MLA_NOPE), F32)
    pad = jnp.zeros((seq, LANES - MLA_QK), F32)
    zn = jnp.zeros((seq, MLA_NOPE), F32)
    cm = jnp.concatenate([ones, c, c, pad], axis=-1)
    sam = jnp.concatenate([zn, -s, zero, pad], axis=-1)
    sbm = jnp.concatenate([zn, zero, s, pad], axis=-1)
    return jnp.stack([c64, sa64, sb64, cm, sam, sbm])


def _swa_head_perm(t, axis):
    parts = jnp.split(t, SWA_HEADS, axis=axis)
    return jnp.concatenate([parts[0], parts[2], parts[1], parts[3]], axis=axis)


def _prep_layer_params(l, ffn1_norm, ffn1_gate, ffn1_up, ffn1_down, mix_norm, w_in,
                       ssd_conv_w, ssd_conv_b, ssd_dt_bias, ssd_a_log, ssd_d, ssd_norm,
                       swa_q_norm, swa_k_norm, swa_sink, swa_out_norm,
                       mla_q_lat_norm, mla_w_uq, mla_kv_norm, mla_w_ukv, mla_q_norm, mla_k_norm,
                       mla_out_norm, w_out, ffn2_norm, ffn2_gate, ffn2_up, ffn2_down):
    d = w_in.shape[1]
    row = lambda v: v.reshape(1, -1).astype(F32)
    pad_lanes = lambda v, n: jnp.pad(v, [(0, 0)] * (v.ndim - 1) + [(0, n - v.shape[-1])])

    def ffn_w(gate, up, down):
        assert gate.shape[-1] % FFN_CHUNK == 0
        return tuple(_cast_ffn_weights(l, gate, up, down))

    wi = w_in[l]
    o = 0
    offs = {}
    for name, size in (("z", SSD_INNER), ("xbc", SSD_CONV_DIM), ("dt", 2 * SSD_HEADS),
                       ("swq", SWA_WIDTH), ("swk", LANES), ("swv", LANES),
                       ("mlq", MLA_Q_RANK), ("ckv", MLA_KV_RANK), ("kr", MLA_ROPE)):
        offs[name] = wi[:, o:o + size]
        o += size
    zc = lambda n: jnp.zeros((d, n), wi.dtype)
    misc = jnp.concatenate([offs["dt"], zc(MLA_NOPE - 2 * SSD_HEADS), offs["kr"],
                            zc(LANES - MLA_QK)], axis=-1)
    w_in_p = jnp.concatenate([offs["z"], offs["xbc"], _swa_head_perm(offs["swq"], 1),
                              offs["swk"], offs["swv"], offs["mlq"], offs["ckv"], misc],
                             axis=-1).astype(BF16)

    wuq = mla_w_uq[l].reshape(MLA_Q_RANK, MLA_HEADS, MLA_QK)
    wuq = pad_lanes(wuq, LANES).reshape(MLA_Q_RANK, MLA_HEADS * LANES).astype(BF16)
    wukv = mla_w_ukv[l].reshape(MLA_KV_RANK, MLA_HEADS, MLA_NOPE + MLA_V)
    knope = pad_lanes(wukv[:, :, :MLA_NOPE], LANES)
    vv = wukv[:, :, MLA_NOPE:]
    zv = jnp.zeros_like(vv)
    even = (jnp.arange(MLA_HEADS) % 2 == 0)[None, :, None]
    vpad = jnp.concatenate([jnp.where(even, vv, zv), jnp.where(even, zv, vv)], axis=-1)
    wukv_p = jnp.concatenate([knope.reshape(MLA_KV_RANK, -1), vpad.reshape(MLA_KV_RANK, -1)],
                             axis=-1).astype(BF16)

    wo = w_out[l]
    a, b = SSD_INNER, SSD_INNER + SWA_WIDTH
    w_out_p = jnp.concatenate([wo[:a], _swa_head_perm(wo[a:b], 0), wo[b:]], axis=0).astype(BF16)

    two = lambda v: jnp.concatenate([v, v], axis=-1)
    return {
        "ffn1": (row(ffn1_norm[l]),) + ffn_w(ffn1_gate, ffn1_up, ffn1_down),
        "ffn2": (row(ffn2_norm[l]),) + ffn_w(ffn2_gate, ffn2_up, ffn2_down),
        "w_in": w_in_p,
        "mix_norm": row(mix_norm[l]),
        "conv_w": ssd_conv_w[l].astype(F32),
        "conv_b": row(ssd_conv_b[l]),
        "dt_bias": pad_lanes(row(ssd_dt_bias[l]), LANES),
        "swa_q_gain": row(two(swa_q_norm[l])),
        "swa_k_gain": row(two(swa_k_norm[l])),
        "mla_qlat_norm": row(mla_q_lat_norm[l]),
        "w_uq": wuq,
        "mla_kv_norm": row(mla_kv_norm[l]),
        "w_ukv": wukv_p,
        "mla_q_gain": pad_lanes(row(mla_q_norm[l]), LANES),
        "mla_k_gain": pad_lanes(row(mla_k_norm[l]), LANES),
        "ssd_a_log": pad_lanes(row(ssd_a_log[l]), LANES),
        "ssd_d": row(jnp.repeat(ssd_d[l], SSD_HEAD_DIM)),
        "ssd_norm": row(ssd_norm[l]),
        "swa_sink": swa_sink[l].astype(F32),
        "swa_out_norm": row(_swa_head_perm(swa_out_norm[l], 0)),
        "mla_out_norm": row(mla_out_norm[l]),
        "w_out": w_out_p,
    }


def _tiles(seq):
    pick = lambda pref: math.gcd(seq, pref)
    return {"ffn": pick(512), "proj": pick(512), "mla_q": pick(256), "mla_k": pick(1024),
            "swa_blocks": pick(4 * SWA_BLOCK) // SWA_BLOCK}


def kernel(x, ffn1_norm, ffn1_gate, ffn1_up, ffn1_down, mix_norm, w_in, ssd_conv_w, ssd_conv_b, ssd_dt_bias, ssd_a_log, ssd_d, ssd_norm, swa_q_norm, swa_k_norm, swa_sink, swa_out_norm, mla_q_lat_norm, mla_w_uq, mla_kv_norm, mla_w_ukv, mla_q_norm, mla_k_norm, mla_out_norm, w_out, ffn2_norm, ffn2_gate, ffn2_up, ffn2_down):
    batch, seq, d = x.shape
    assert seq % SSD_CHUNK == 0 and seq % SWA_BLOCK == 0
    params = (ffn1_norm, ffn1_gate, ffn1_up, ffn1_down, mix_norm, w_in, ssd_conv_w, ssd_conv_b,
              ssd_dt_bias, ssd_a_log, ssd_d, ssd_norm, swa_q_norm, swa_k_norm, swa_sink,
              swa_out_norm, mla_q_lat_norm, mla_w_uq, mla_kv_norm, mla_w_ukv, mla_q_norm,
              mla_k_norm, mla_out_norm, w_out, ffn2_norm, ffn2_gate, ffn2_up, ffn2_down)
    tl = _tiles(seq)
    tabs = _rope_tables(seq)
    xf = x.reshape(batch * seq, d).astype(F32)
    for l in range(w_in.shape[0]):
        p = _prep_layer_params(l, *params)
        xf = _ffn(xf, *p["ffn1"], tm=tl["ffn"])
        z, xbc, dt, swq, swk, swv, mq, mk, mv = _in_proj(xf, tabs, p, seq, tl["proj"])
        y_ssd = _ssd(xbc, dt, z, p, batch, seq)
        y_swa = _swa(swq, swk, swv, p["swa_sink"], p["swa_out_norm"], batch, seq,
                     tl["swa_blocks"])
        y_mla = _mla(mq, mk, mv, p["mla_out_norm"], batch, seq, tl["mla_q"], tl["mla_k"])
        xf = _ffn(xf, *p["ffn2"], tm=tl["ffn"], mixer=(y_ssd, y_swa, y_mla, p["w_out"]))
    return xf.reshape(batch, seq, d).astype(x.dtype)
```

```python
import functools
import math

import jax
import jax.numpy as jnp
from jax import lax
from jax.experimental import pallas as pl
from jax.experimental.pallas import tpu as pltpu

F32 = jnp.float32
BF16 = jnp.bfloat16

LANES = 128
HALF = LANES // 2

SSD_HEADS = 8
SSD_HEAD_DIM = 64
SSD_INNER = SSD_HEADS * SSD_HEAD_DIM
SSD_GROUPS = 2
SSD_STATE = 128
SSD_CONV = 5
SSD_CHUNK = 256
SSD_CONV_DIM = SSD_INNER + 2 * SSD_GROUPS * SSD_STATE
SWA_HEADS = 4
SWA_KV_HEADS = 2
SWA_HEAD_DIM = 64
SWA_WINDOW = 128
SWA_BLOCK = 128
SWA_WIDTH = SWA_HEADS * SWA_HEAD_DIM
MLA_HEADS = 4
MLA_Q_RANK = 256
MLA_KV_RANK = 128
MLA_NOPE = 64
MLA_ROPE = 32
MLA_QK = MLA_NOPE + MLA_ROPE
MLA_V = 64
MLA_WIDTH = MLA_HEADS * MLA_V
ROPE_THETA = 10000.0
EPS = 1e-6

C_Z = 0
C_XBC = C_Z + SSD_INNER
C_SWQ = C_XBC + SSD_CONV_DIM
C_SWK = C_SWQ + SWA_WIDTH
C_SWV = C_SWK + LANES
C_MLQ = C_SWV + LANES
C_CKV = C_MLQ + MLA_Q_RANK
C_MISC = C_CKV + MLA_KV_RANK
C_END = C_MISC + LANES
CONV_HALO = 16
FFN_CHUNK = 256
NEG_INF = float("-inf")
LOG_FLOOR = -1e30
LOG2E = math.log2(math.e)
MLA_BOUND_LIMIT = 48.0
MLA_BOUND_SLACK = 1.01


def _rms(x, w):
    ms = jnp.mean(x * x, axis=-1, keepdims=True)
    return x * lax.rsqrt(ms + EPS) * w


def _silu(x):
    return x / (1.0 + jnp.exp(-x))


def _dot(a, b):
    return jnp.dot(a, b, preferred_element_type=F32)


def _dot_nt(a, b):
    return lax.dot_general(a, b, (((1,), (1,)), ((), ())), preferred_element_type=F32)


def _dot_tn(a, b):
    return lax.dot_general(a, b, (((0,), (0,)), ((), ())), preferred_element_type=F32)


def _lane_iota(shape):
    return lax.broadcasted_iota(jnp.int32, shape, len(shape) - 1)


def _resident(shape, single_buffer=False):
    nd = len(shape)
    mode = {"pipeline_mode": pl.Buffered(1)} if single_buffer else {}
    return pl.BlockSpec(shape, lambda *_: (0,) * nd, **mode)


def _cast_body(*refs):
    n = len(refs) // 2
    for src, dst in zip(refs[:n], refs[n:]):
        dst[...] = src[...].astype(dst.dtype)


def _cast_ffn_weights(l, gate, up, down, steps=4):
    _, d, f = gate.shape
    col = pl.BlockSpec((None, d // steps, f), lambda i: (l, i, 0))
    rowb = pl.BlockSpec((None, f // steps, d), lambda i: (l, i, 0))
    out_col = pl.BlockSpec((d // steps, f), lambda i: (i, 0))
    out_row = pl.BlockSpec((f // steps, d), lambda i: (i, 0))
    return pl.pallas_call(
        _cast_body,
        grid=(steps,),
        in_specs=[col, col, rowb],
        out_specs=[out_col, out_col, out_row],
        out_shape=[jax.ShapeDtypeStruct((d, f), BF16), jax.ShapeDtypeStruct((d, f), BF16),
                   jax.ShapeDtypeStruct((f, d), BF16)],
        compiler_params=pltpu.CompilerParams(
            dimension_semantics=("parallel",), vmem_limit_bytes=40 * 1024 * 1024),
        name="cast_ffn_weights",
    )(gate, up, down)


def _ffn_body(*refs, fuse_out_proj):
    if fuse_out_proj:
        x_ref, ys_ref, yw_ref, ym_ref, wo_ref = refs[:5]
        a, b = SSD_INNER, SSD_INNER + SWA_WIDTH
        x = (x_ref[...] + _dot(ys_ref[...], wo_ref[0:a, :]) + _dot(yw_ref[...], wo_ref[a:b, :])
             + _dot(ym_ref[...], wo_ref[b:, :]))
        refs = refs[5:]
    else:
        x_ref = refs[0]
        x = x_ref[...]
        refs = refs[1:]
    nw_ref, wg_ref, wu_ref, wd_ref, o_ref, xn_ref, acc_ref = refs
    o_ref[...] = x
    xn_ref[...] = _rms(x, nw_ref[...]).astype(BF16)
    for c in range(wg_ref.shape[1] // FFN_CHUNK):
        cols = slice(c * FFN_CHUNK, (c + 1) * FFN_CHUNK)
        xn = xn_ref[...]
        g = _dot(xn, wg_ref[:, cols])
        u = _dot(xn, wu_ref[:, cols])
        h = (_silu(g) * u).astype(BF16)
        part = _dot(h, wd_ref[cols, :])
        if c == 0:
            acc_ref[...] = part
        else:
            acc_ref[...] += part
    o_ref[...] += 0.5 * acc_ref[...]


def _ffn(x, nw, wg, wu, wd, tm, mixer=None):
    n, d = x.shape
    row = lambda w: pl.BlockSpec((tm, w), lambda i: (i, 0))
    args, specs = [x], [row(d)]
    if mixer is not None:
        args += list(mixer)
        specs += [row(a.shape[1]) for a in mixer[:3]] + [_resident(mixer[3].shape)]
    args += [nw, wg, wu, wd]
    specs += [_resident((1, d))] + [_resident(w.shape, single_buffer=True) for w in (wg, wu, wd)]
    return pl.pallas_call(
        functools.partial(_ffn_body, fuse_out_proj=mixer is not None),
        grid=(n // tm,),
        in_specs=specs,
        out_specs=row(d),
        out_shape=jax.ShapeDtypeStruct((n, d), F32),
        scratch_shapes=[pltpu.VMEM((tm, d), BF16), pltpu.VMEM((tm, d), F32)],
        compiler_params=pltpu.CompilerParams(
            dimension_semantics=("parallel",), vmem_limit_bytes=56 * 1024 * 1024),
        name="ffn",
    )(*args)


def _rope(y, c, sa, sb, shift):
    return (y * c + pltpu.roll(y, LANES - shift, 1) * sa + pltpu.roll(y, shift, 1) * sb)


def _head_sum_matrix(width):
    r = lax.broadcasted_iota(jnp.int32, (LANES, LANES), 0)
    c = lax.broadcasted_iota(jnp.int32, (LANES, LANES), 1)
    same = (r < width) if width > HALF else ((r < HALF) == (c < HALF))
    return jnp.where(same, 1.0, 0.0).astype(BF16)


def _head_norm_rope(x, sumsq, gain, c, sa, sb, width, rope_dim, post_scale):
    y = x * lax.rsqrt(sumsq / float(width) + EPS) * gain
    out = _rope(y, c, sa, sb, rope_dim // 2)
    return out if post_scale == 1.0 else out * post_scale


def _inproj_body(x_ref, xp_ref, xnx_ref, tab_ref, w_ref, mixw_ref, convw_ref, convb_ref,
                 dtb_ref, swqg_ref, swkg_ref, qlw_ref, wuq_ref, kvw_ref, wukv_ref,
                 mqg_ref, mkg_ref,
                 z_ref, xbc_ref, dt_ref, swq_ref, swk_ref, swv_ref, mq_ref, mk_ref, mv_ref,
                 *, tiles_per_seq):
    i = pl.program_id(0)
    tm = x_ref.shape[0]
    pos_tile = i % tiles_per_seq
    mixw = mixw_ref[...]
    hn = _rms(x_ref[...], mixw).astype(BF16)
    hp = _rms(xp_ref[...], mixw).astype(BF16)
    hx = _rms(xnx_ref[...], mixw).astype(BF16)
    groups = lambda v: [v[:, g * LANES:(g + 1) * LANES] for g in range(v.shape[1] // LANES)]

    wx = w_ref[:, C_XBC:C_SWQ]
    z = _dot(hn, w_ref[:, C_Z:C_XBC])
    xbc_prev, xbc_main, xbc_next = _dot(hp, wx), _dot(hn, wx), _dot(hx, wx)
    ckv, misc = groups(_dot(hn, w_ref[:, C_CKV:C_END]))
    swa_q = groups(_dot(hn, w_ref[:, C_SWQ:C_SWK]))
    swa_k, swa_v = groups(_dot(hn, w_ref[:, C_SWK:C_MLQ]))
    ql = _dot(hn, w_ref[:, C_MLQ:C_CKV])

    q_up = groups(_dot(_rms(ql, qlw_ref[...]).astype(BF16), wuq_ref[...]))
    kv_up = groups(_dot(_rms(ckv, kvw_ref[...]).astype(BF16), wukv_ref[...]))
    lane = _lane_iota(misc.shape)
    kr = jnp.where((lane >= MLA_NOPE) & (lane < MLA_QK), misc, 0.0)
    mla_q = q_up
    mla_k = [k_nope + kr for k_nope in kv_up[:MLA_HEADS]]
    mla_v = kv_up[MLA_HEADS:]

    sum64, sum96 = _head_sum_matrix(SWA_HEAD_DIM), _head_sum_matrix(MLA_QK)
    sumsq = lambda xs, m: [_dot((v * v).astype(BF16), m) for v in xs]
    ss_swa_q, ss_swa_k = sumsq(swa_q, sum64), sumsq([swa_k], sum64)
    ss_mla_q, ss_mla_k = sumsq(mla_q, sum96), sumsq(mla_k, sum96)

    z_ref[...] = z.astype(BF16)

    keep_prev = (pos_tile > 0).astype(F32)
    keep_next = (pos_tile < tiles_per_seq - 1).astype(F32)
    padded = jnp.concatenate([xbc_prev * keep_prev, xbc_main, xbc_next * keep_next], axis=0)
    n_pad = padded.shape[0]
    conv = jnp.broadcast_to(convb_ref[...], (tm, SSD_CONV_DIM))
    for k in range(SSD_CONV):
        shift = SSD_CONV // 2 - k
        tap = padded if shift == 0 else pltpu.roll(padded, shift % n_pad, 0)
        conv = conv + convw_ref[k:k + 1, :] * tap[CONV_HALO:CONV_HALO + tm, :]
    xbc_ref[...] = _silu(conv).astype(BF16)

    dtv = misc + dtb_ref[...]
    dt_ref[...] = jnp.maximum(dtv, 0.0) + jnp.log1p(jnp.exp(-jnp.abs(dtv)))

    swa_rope = (tab_ref[0], tab_ref[1], tab_ref[2], SWA_HEAD_DIM, SWA_HEAD_DIM)
    for g, (v, ss) in enumerate(zip(swa_q, ss_swa_q)):
        swq_ref[:, g * LANES:(g + 1) * LANES] = _head_norm_rope(
            v, ss, swqg_ref[...], *swa_rope, SWA_HEAD_DIM ** -0.5 * LOG2E).astype(BF16)
    swk_ref[...] = _head_norm_rope(swa_k, ss_swa_k[0], swkg_ref[...], *swa_rope, 1.0).astype(BF16)
    swv_ref[...] = swa_v.astype(BF16)

    mla_rope = (tab_ref[3], tab_ref[4], tab_ref[5], MLA_QK, MLA_ROPE)
    for h in range(MLA_HEADS):
        sl = slice(h * LANES, (h + 1) * LANES)
        mq_ref[:, sl] = _head_norm_rope(mla_q[h], ss_mla_q[h], mqg_ref[...], *mla_rope,
                                        MLA_QK ** -0.5 * LOG2E).astype(BF16)
        mk_ref[:, sl] = _head_norm_rope(mla_k[h], ss_mla_k[h], mkg_ref[...], *mla_rope,
                                        1.0).astype(BF16)
        ones_lane = HALF if h % 2 == 0 else 0
        mv_ref[:, sl] = jnp.where(lane == ones_lane, 1.0, mla_v[h]).astype(BF16)


def _in_proj(x, tabs, p, seq, tm):
    n, d = x.shape
    tiles_per_seq = seq // tm
    halo_per_tile = tm // CONV_HALO
    n_halo = n // CONV_HALO
    row = lambda w: pl.BlockSpec((tm, w), lambda i: (i, 0))
    in_specs = [
        row(d),
        pl.BlockSpec((CONV_HALO, d), lambda i: (jnp.maximum(i * halo_per_tile - 1, 0), 0)),
        pl.BlockSpec((CONV_HALO, d),
                     lambda i: (jnp.minimum((i + 1) * halo_per_tile, n_halo - 1), 0)),
        pl.BlockSpec((6, tm, LANES), lambda i: (0, i % tiles_per_seq, 0)),
    ]
    names = ("w_in", "mix_norm", "conv_w", "conv_b", "dt_bias", "swa_q_gain", "swa_k_gain",
             "mla_qlat_norm", "w_uq", "mla_kv_norm", "w_ukv", "mla_q_gain", "mla_k_gain")
    args = [p[k] for k in names]
    in_specs += [_resident(a.shape) for a in args]
    widths = (SSD_INNER, SSD_CONV_DIM, LANES, SWA_WIDTH, LANES, LANES,
              MLA_HEADS * LANES, MLA_HEADS * LANES, MLA_HEADS * LANES)
    dts = (BF16, BF16, F32, BF16, BF16, BF16, BF16, BF16, BF16)
    return pl.pallas_call(
        functools.partial(_inproj_body, tiles_per_seq=tiles_per_seq),
        grid=(n // tm,),
        in_specs=in_specs,
        out_specs=[row(w) for w in widths],
        out_shape=[jax.ShapeDtypeStruct((n, w), t) for w, t in zip(widths, dts)],
        compiler_params=pltpu.CompilerParams(
            dimension_semantics=("parallel",), vmem_limit_bytes=48 * 1024 * 1024),
        name="in_proj",
    )(x, x, x, tabs, *args)


def _split_bf16(v, n):
    pieces = []
    for _ in range(n - 1):
        p = v.astype(BF16)
        pieces.append(p)
        v = v - p.astype(F32)
    pieces.append(v.astype(BF16))
    return pieces


def _pack_lanes(pieces, width):
    lane = _lane_iota(pieces[0].shape)
    out = jnp.zeros(pieces[0].shape, F32)
    for k, p in enumerate(pieces):
        pf = p.astype(F32)
        if k:
            pf = pltpu.roll(pf, k * width, 1)
        out = jnp.where((lane >= k * width) & (lane < (k + 1) * width), pf, out)
    return out.astype(BF16)


def _ssd_expand_matrices():
    nd = 2 * SSD_HEADS
    rows = jnp.arange(LANES)[:, None]
    head = jnp.arange(SSD_INNER)[None, :] // SSD_HEAD_DIM
    mats = []
    for direction in range(2):
        for first_piece in (0, 2):
            d = direction * SSD_HEADS + head
            hit = (rows == first_piece * nd + d) | (rows == (first_piece + 1) * nd + d)
            mats.append(hit)
    return jnp.stack(mats).astype(BF16)


def _ssd_body(xbc_ref, dt_ref, z_ref, alog_ref, dskip_ref, nw_ref, expand_ref, o_ref,
              yacc_ref, sf_ref, sb_ref, cols_ref, *, n_chunks):
    pas = pl.program_id(1)
    j = pl.program_id(2)
    t = SSD_CHUNK
    cps = xbc_ref.shape[0] // t
    nh = SSD_HEADS

    gw = SSD_INNER // SSD_GROUPS
    hpg = nh // SSD_GROUPS
    nd = 2 * nh
    lane = _lane_iota((t, LANES))

    def b_of(rows, g):
        return xbc_ref[rows, SSD_INNER + g * SSD_STATE:SSD_INNER + (g + 1) * SSD_STATE]

    def c_of(rows, g):
        return xbc_ref[rows, SSD_INNER + (SSD_GROUPS + g) * SSD_STATE:
                       SSD_INNER + (SSD_GROUPS + g + 1) * SSD_STATE]

    def forward_early(k):
        rows = slice(k * t, (k + 1) * t)
        fwd = lane < nh
        dt = dt_ref[rows, :]
        avec = jnp.where(_lane_iota((1, LANES)) < nd, -jnp.exp(alog_ref[...]), 0.0)
        a = dt * avec
        row = lax.broadcasted_iota(jnp.int32, (t, t), 0)
        col = lax.broadcasted_iota(jnp.int32, (t, t), 1)
        tri = jnp.where(col <= row, 1.0, 0.0).astype(BF16)
        cum = sum(_dot(tri, part) for part in _split_bf16(a, 3))
        excl = cum - a
        tot = cum[t - 1:t, :]

        e_in = jnp.exp(jnp.where(fwd, cum, tot - excl))
        w_st = jnp.exp(jnp.where(fwd, tot - cum, excl)) * dt
        e1, e2 = _split_bf16(e_in, 2)
        w1, w2 = _split_bf16(w_st, 2)
        cols = _pack_lanes([e1, e2, w1, w2], nd)
        cols_ref[j * cps + k] = cols

        r = jnp.where(fwd, cum, -excl) * LOG2E
        cc = r - jnp.maximum(jnp.log(dt), LOG_FLOOR) * LOG2E
        ones = jnp.ones((t, LANES), F32)
        lhs = jnp.where(lane < 3 * nd, _pack_lanes(_split_bf16(r, 3), nd),
                        jnp.where(lane < 6 * nd, ones, 0.0).astype(BF16))
        cc_t = cc.T[0:nd, :]
        c1, c2, c3 = _split_bf16(cc_t, 3)
        rhs = jnp.concatenate(
            [jnp.ones((3 * nd, t), BF16), -c1, -c2, -c3, jnp.zeros((LANES - 6 * nd, t), BF16)],
            axis=0)
        sub = lax.broadcasted_iota(jnp.int32, (LANES, t), 0) % nd

        e_x = _dot(cols, expand_ref[0])
        w_x = _dot(cols, expand_ref[1])
        cbs = [_dot_nt(c_of(rows, g), b_of(rows, g)) for g in range(SSD_GROUPS)]
        ys = []
        for g in range(SSD_GROUPS):
            gs = slice(g * gw, (g + 1) * gw)
            s_in = sf_ref[g]
            xgf = xbc_ref[rows, gs].astype(F32)
            ys.append(dskip_ref[:, gs] * xgf + _dot(c_of(rows, g), s_in.astype(BF16)) * e_x[:, gs])
            xw = (xgf * w_x[:, gs]).astype(BF16)
            sf_ref[g] = s_in * e_x[t - 1:t, gs] + _dot_tn(b_of(rows, g), xw)
        exps = [_dot(lhs, jnp.where(sub == idx, rhs, jnp.zeros_like(rhs))) for idx in range(nd)]
        return cbs, ys, exps

    def forward_late(k, cbs, ys, exps):
        rows = slice(k * t, (k + 1) * t)
        th = t // 2
        quad = lambda v, qi, qj: v[qi * th:(qi + 1) * th, qj * th:(qj + 1) * th]
        qrow = lax.broadcasted_iota(jnp.int32, (th, th), 0)
        qcol = lax.broadcasted_iota(jnp.int32, (th, th), 1)
        on_low = qrow >= qcol
        on_up = qcol >= qrow

        def mixing(h, cb):
            df, db = exps[h], exps[nh + h]
            diag = [quad(cb, q, q) * (jnp.exp2(jnp.where(on_low, quad(df, q, q), NEG_INF))
                                      + jnp.exp2(jnp.where(on_up, quad(db, q, q), NEG_INF)))
                    for q in range(2)]
            upper = quad(cb, 0, 1) * jnp.exp2(quad(db, 0, 1))
            lower = quad(cb, 1, 0) * jnp.exp2(quad(df, 1, 0))
            return jnp.concatenate([jnp.concatenate([diag[0], upper], axis=1),
                                    jnp.concatenate([lower, diag[1]], axis=1)], axis=0)

        glane = _lane_iota((t, gw))
        for g in range(SSD_GROUPS):
            gs = slice(g * gw, (g + 1) * gw)
            xg = xbc_ref[rows, gs]
            y = ys[g]
            for hh in range(hpg):
                m = mixing(g * hpg + hh, cbs[g]).astype(BF16)
                mine = (glane >= hh * SSD_HEAD_DIM) & (glane < (hh + 1) * SSD_HEAD_DIM)
                y = y + _dot(m, jnp.where(mine, xg, jnp.zeros_like(xg)))
            yacc_ref[j * cps + k, :, gs] = y

    @pl.when(pas == 0)
    def _forward():
        @pl.when(j == 0)
        def _():
            sf_ref[...] = jnp.zeros_like(sf_ref)

        early = [forward_early(k) for k in range(cps)]
        for k in range(cps):
            forward_late(k, *early[k])

    @pl.when(pas == 1)
    def _backward():
        @pl.when(j == 0)
        def _():
            sb_ref[...] = jnp.zeros_like(sb_ref)

        for k in reversed(range(cps)):
            rows = slice(k * t, (k + 1) * t)
            c = n_chunks - 1 - j * cps - (cps - 1 - k)
            cols = cols_ref[c]
            e_x = _dot(cols, expand_ref[2])
            w_x = _dot(cols, expand_ref[3])
            gate = _silu(z_ref[rows, :].astype(F32))
            for g in range(SSD_GROUPS):
                gs = slice(g * gw, (g + 1) * gw)
                s_in = sb_ref[g]
                y = yacc_ref[c, :, gs] + _dot(c_of(rows, g), s_in.astype(BF16)) * e_x[:, gs]
                o_ref[rows, gs] = _rms(y * gate[:, gs], nw_ref[:, gs]).astype(BF16)
                xw = (xbc_ref[rows, gs].astype(F32) * w_x[:, gs]).astype(BF16)
                sb_ref[g] = s_in * e_x[0:1, gs] + _dot_tn(b_of(rows, g), xw)


def _ssd(xbc, dt, z, p, batch, seq, cps):
    t = SSD_CHUNK
    nc = seq // t
    ns = nc // cps

    def chunk_idx(b, pas, j):
        return b * ns + j + pas * (ns - 1 - 2 * j)

    def out_idx(b, pas, j):
        return b * ns + (ns - 1) - pas * j

    blk = lambda w, f: pl.BlockSpec((cps * t, w), lambda b, pas, j: (f(b, pas, j), 0))
    expand = _ssd_expand_matrices()
    state = pltpu.VMEM((SSD_GROUPS, SSD_STATE, SSD_INNER // SSD_GROUPS), F32)
    return pl.pallas_call(
        functools.partial(_ssd_body, n_chunks=nc),
        grid=(batch, 2, ns),
        in_specs=[blk(SSD_CONV_DIM, chunk_idx), blk(LANES, chunk_idx), blk(SSD_INNER, chunk_idx),
                  _resident((1, LANES)), _resident((1, SSD_INNER)), _resident((1, SSD_INNER)),
                  _resident(expand.shape)],
        out_specs=blk(SSD_INNER, out_idx),
        out_shape=jax.ShapeDtypeStruct((batch * seq, SSD_INNER), BF16),
        scratch_shapes=[pltpu.VMEM((nc, t, SSD_INNER), F32), state, state,
                        pltpu.VMEM((nc, t, LANES), BF16)],
        compiler_params=pltpu.CompilerParams(
            dimension_semantics=("arbitrary", "arbitrary", "arbitrary"),
            vmem_limit_bytes=40 * 1024 * 1024),
        name="ssd",
    )(xbc, dt, z, p["ssd_a_log"], p["ssd_d"], p["ssd_norm"], expand)


def _swa_body(sink_ref, q_ref, kp_ref, kc_ref, kn_ref, vp_ref, vc_ref, vn_ref, nw_ref, o_ref):
    assert SWA_WINDOW == SWA_BLOCK
    j = pl.program_id(1)
    blk = SWA_BLOCK
    n_blk = q_ref.shape[0] // blk
    kb = jnp.concatenate([kp_ref[...], kc_ref[...], kn_ref[...]], axis=0)
    vb = jnp.concatenate([vp_ref[...], vc_ref[...], vn_ref[...]], axis=0)
    qi = lax.broadcasted_iota(jnp.int32, (blk, blk), 0)
    kj = lax.broadcasted_iota(jnp.int32, (blk, blk), 1)
    lo = _lane_iota((blk, LANES)) < HALF
    lo_v = _lane_iota(vb.shape) < HALF
    zero_q = jnp.zeros((blk, LANES), BF16)
    lane_v = _lane_iota(vb.shape)
    unit = lambda at: jnp.where(lane_v == at, 1.0, 0.0).astype(BF16)
    v_half = (jnp.where(lo_v, vb, unit(HALF)), jnp.where(lo_v, unit(0), vb))
    first = jnp.where(j == 0, blk, 0)
    last = jnp.where(j == pl.num_programs(1) - 1, blk, 0)
    items = [(t, g, half) for t in range(n_blk) for g in range(SWA_WIDTH // LANES)
             for half in range(2)]
    logits = {}
    for t, g, half in items:
        qg = q_ref[t * blk:(t + 1) * blk, g * LANES:(g + 1) * LANES]
        qm = jnp.where(lo, qg, zero_q) if half == 0 else jnp.where(lo, zero_q, qg)
        logits[t, g, half] = _dot_nt(qm, kb[t * blk:(t + 3) * blk])
    scaled = {}
    for t, g, half in items:
        keep_prev = kj >= (qi + first if t == 0 else qi)
        keep_next = kj <= (qi - last if t == n_blk - 1 else qi)
        s = logits[t, g, half]
        s_prev = jnp.where(keep_prev, s[:, :blk], NEG_INF)
        s_own = s[:, blk:2 * blk]
        s_next = jnp.where(keep_next, s[:, 2 * blk:], NEG_INF)
        sk = sink_ref[g + 2 * half] * LOG2E
        m = jnp.max(jnp.maximum(jnp.maximum(s_prev, s_own), s_next), axis=-1, keepdims=True)
        m = jnp.maximum(m, sk)
        pexp = jnp.concatenate([jnp.exp2(v - m) for v in (s_prev, s_own, s_next)],
                               axis=-1).astype(BF16)
        acc = _dot(pexp, v_half[half][t * blk:(t + 3) * blk])
        ones_lane = HALF if half == 0 else 0
        den = acc[:, ones_lane:ones_lane + 1] + jnp.exp2(sk - m)
        scaled[t, g, half] = acc / den
    for t in range(n_blk):
        y = jnp.concatenate([jnp.where(lo, scaled[t, g, 0], scaled[t, g, 1])
                             for g in range(SWA_WIDTH // LANES)], axis=-1)
        o_ref[t * blk:(t + 1) * blk, :] = _rms(y, nw_ref[...]).astype(BF16)


def _swa(q, k, v, sink, nw, batch, seq, n_blk):
    blk = SWA_BLOCK
    nb = seq // blk
    ns = nb // n_blk
    cur = lambda b, j: (b * ns + j, 0)
    prv = lambda b, j: (b * nb + jnp.maximum(j * n_blk - 1, 0), 0)
    nxt = lambda b, j: (b * nb + jnp.minimum((j + 1) * n_blk, nb - 1), 0)
    edge = lambda f: pl.BlockSpec((blk, LANES), f)
    main = pl.BlockSpec((n_blk * blk, LANES), cur)
    return pl.pallas_call(
        _swa_body,
        grid=(batch, ns),
        in_specs=[pl.BlockSpec(memory_space=pltpu.SMEM),
                  pl.BlockSpec((n_blk * blk, SWA_WIDTH), cur),
                  edge(prv), main, edge(nxt), edge(prv), main, edge(nxt),
                  _resident((1, SWA_WIDTH))],
        out_specs=pl.BlockSpec((n_blk * blk, SWA_WIDTH), cur),
        out_shape=jax.ShapeDtypeStruct((batch * seq, SWA_WIDTH), BF16),
        compiler_params=pltpu.CompilerParams(dimension_semantics=("parallel", "parallel")),
        name="swa",
    )(sink, q, k, k, k, v, v, v, nw)


def _mla_body(q_ref, k_ref, v_ref, nw_ref, o_ref, knorm_ref, *, key_chunk):
    i = pl.program_id(1)
    tq = q_ref.shape[0]
    seq = k_ref.shape[0]
    heads = [slice(h * LANES, (h + 1) * LANES) for h in range(MLA_HEADS)]
    chunks = [slice(c * key_chunk, (c + 1) * key_chunk) for c in range(seq // key_chunk)]
    lo = _lane_iota((tq, LANES)) < HALF

    @pl.when(i == 0)
    def _key_norms():
        for h, sl in enumerate(heads):
            kf = k_ref[:, sl].astype(F32)
            ss = jnp.max(jnp.sum(kf * kf, axis=-1, keepdims=True), axis=0, keepdims=True)
            knorm_ref[h:h + 1, :] = jnp.broadcast_to(jnp.sqrt(ss), (1, LANES))

    bounds = []
    for h, sl in enumerate(heads):
        qf = q_ref[:, sl].astype(F32)
        qn = jnp.sqrt(jnp.sum(qf * qf, axis=-1, keepdims=True))
        bounds.append(qn * knorm_ref[h:h + 1, 0:1] * MLA_BOUND_SLACK)
    bmax = jnp.max(jnp.maximum(jnp.maximum(bounds[0], bounds[1]),
                               jnp.maximum(bounds[2], bounds[3])))
    use_bound = bmax <= MLA_BOUND_LIMIT

    def denom(acc, h):
        return acc[:, HALF:HALF + 1] if h % 2 == 0 else acc[:, 0:1]

    def finish(accs, dens):
        outs = [jnp.where(lo, accs[2 * p] / dens[2 * p], accs[2 * p + 1] / dens[2 * p + 1])
                for p in range(MLA_HEADS // 2)]
        y = jnp.concatenate(outs, axis=-1)
        o_ref[...] = _rms(y, nw_ref[...]).astype(BF16)

    @pl.when(use_bound)
    def _bounded():
        accs = []
        for h, sl in enumerate(heads):
            qh = q_ref[:, sl]
            acc = None
            for ks in chunks:
                pexp = jnp.exp2(_dot_nt(qh, k_ref[ks, sl]) - bounds[h]).astype(BF16)
                part = _dot(pexp, v_ref[ks, sl])
                acc = part if acc is None else acc + part
            accs.append(acc)
        finish(accs, [denom(a, h) for h, a in enumerate(accs)])

    @pl.when(jnp.logical_not(use_bound))
    def _online():
        accs, dens = [], []
        for h, sl in enumerate(heads):
            qh = q_ref[:, sl]
            m = jnp.full((tq, 1), NEG_INF, F32)
            acc = jnp.zeros((tq, LANES), F32)
            for ks in chunks:
                s = _dot_nt(qh, k_ref[ks, sl])
                m_new = jnp.maximum(m, jnp.max(s, axis=-1, keepdims=True))
                pexp = jnp.exp2(s - m_new).astype(BF16)
                acc = jnp.exp2(m - m_new) * acc + _dot(pexp, v_ref[ks, sl])
                m = m_new
            accs.append(acc)
            dens.append(denom(acc, h))
        finish(accs, dens)


def _mla(q, k, v, nw, batch, seq, tq, key_chunk):
    nq = seq // tq
    w = MLA_HEADS * LANES
    full = pl.BlockSpec((seq, w), lambda b, i: (b, 0))
    return pl.pallas_call(
        functools.partial(_mla_body, key_chunk=key_chunk),
        grid=(batch, nq),
        in_specs=[pl.BlockSpec((tq, w), lambda b, i: (b * nq + i, 0)), full, full,
                  _resident((1, MLA_WIDTH))],
        out_specs=pl.BlockSpec((tq, MLA_WIDTH), lambda b, i: (b * nq + i, 0)),
        out_shape=jax.ShapeDtypeStruct((batch * seq, MLA_WIDTH), BF16),
        scratch_shapes=[pltpu.VMEM((8, LANES), F32)],
        compiler_params=pltpu.CompilerParams(
            dimension_semantics=("arbitrary", "arbitrary"),
            vmem_limit_bytes=48 * 1024 * 1024),
        name="mla",
    )(q, k, v, nw)


def _rope_tables(seq):
    def angles(dim):
        inv = 1.0 / jnp.power(ROPE_THETA, jnp.arange(0, dim, 2, dtype=F32) / dim)
        return jnp.arange(seq, dtype=F32)[:, None] * inv[None, :]

    a64 = angles(SWA_HEAD_DIM)
    c, s = jnp.cos(a64), jnp.sin(a64)
    zero = jnp.zeros_like(c)
    c64 = jnp.concatenate([c, c, c, c], axis=-1)
    sa64 = jnp.concatenate([-s, zero, -s, zero], axis=-1)
    sb64 = jnp.concatenate([zero, s, zero, s], axis=-1)
    a32 = angles(MLA_ROPE)
    c, s = jnp.cos(a32), jnp.sin(a32)
    zero = jnp.zeros_like(c)
    ones = jnp.ones((seq, MLA_NOPE), F32)
    pad = jnp.zeros((seq, LANES - MLA_QK), F32)
    zn = jnp.zeros((seq, MLA_NOPE), F32)
    cm = jnp.concatenate([ones, c, c, pad], axis=-1)
    sam = jnp.concatenate([zn, -s, zero, pad], axis=-1)
    sbm = jnp.concatenate([zn, zero, s, pad], axis=-1)
    return jnp.stack([c64, sa64, sb64, cm, sam, sbm])


def _swa_head_perm(t, axis):
    parts = jnp.split(t, SWA_HEADS, axis=axis)
    return jnp.concatenate([parts[0], parts[2], parts[1], parts[3]], axis=axis)


def _prep_layer_params(l, ffn1_norm, ffn1_gate, ffn1_up, ffn1_down, mix_norm, w_in,
                       ssd_conv_w, ssd_conv_b, ssd_dt_bias, ssd_a_log, ssd_d, ssd_norm,
                       swa_q_norm, swa_k_norm, swa_sink, swa_out_norm,
                       mla_q_lat_norm, mla_w_uq, mla_kv_norm, mla_w_ukv, mla_q_norm, mla_k_norm,
                       mla_out_norm, w_out, ffn2_norm, ffn2_gate, ffn2_up, ffn2_down):
    d = w_in.shape[1]
    row = lambda v: v.reshape(1, -1).astype(F32)
    pad_lanes = lambda v, n: jnp.pad(v, [(0, 0)] * (v.ndim - 1) + [(0, n - v.shape[-1])])

    def ffn_w(gate, up, down):
        assert gate.shape[-1] % FFN_CHUNK == 0
        return tuple(_cast_ffn_weights(l, gate, up, down))

    wi = w_in[l]
    o = 0
    offs = {}
    for name, size in (("z", SSD_INNER), ("xbc", SSD_CONV_DIM), ("dt", 2 * SSD_HEADS),
                       ("swq", SWA_WIDTH), ("swk", LANES), ("swv", LANES),
                       ("mlq", MLA_Q_RANK), ("ckv", MLA_KV_RANK), ("kr", MLA_ROPE)):
        offs[name] = wi[:, o:o + size]
        o += size
    zc = lambda n: jnp.zeros((d, n), wi.dtype)
    misc = jnp.concatenate([offs["dt"], zc(MLA_NOPE - 2 * SSD_HEADS), offs["kr"],
                            zc(LANES - MLA_QK)], axis=-1)
    w_in_p = jnp.concatenate([offs["z"], offs["xbc"], _swa_head_perm(offs["swq"], 1),
                              offs["swk"], offs["swv"], offs["mlq"], offs["ckv"], misc],
                             axis=-1).astype(BF16)

    wuq = mla_w_uq[l].reshape(MLA_Q_RANK, MLA_HEADS, MLA_QK)
    wuq = pad_lanes(wuq, LANES).reshape(MLA_Q_RANK, MLA_HEADS * LANES).astype(BF16)
    wukv = mla_w_ukv[l].reshape(MLA_KV_RANK, MLA_HEADS, MLA_NOPE + MLA_V)
    knope = pad_lanes(wukv[:, :, :MLA_NOPE], LANES)
    vv = wukv[:, :, MLA_NOPE:]
    zv = jnp.zeros_like(vv)
    even = (jnp.arange(MLA_HEADS) % 2 == 0)[None, :, None]
    vpad = jnp.concatenate([jnp.where(even, vv, zv), jnp.where(even, zv, vv)], axis=-1)
    wukv_p = jnp.concatenate([knope.reshape(MLA_KV_RANK, -1), vpad.reshape(MLA_KV_RANK, -1)],
                             axis=-1).astype(BF16)

    wo = w_out[l]
    a, b = SSD_INNER, SSD_INNER + SWA_WIDTH
    w_out_p = jnp.concatenate([wo[:a], _swa_head_perm(wo[a:b], 0), wo[b:]], axis=0).astype(BF16)

    two = lambda v: jnp.concatenate([v, v], axis=-1)
    return {
        "ffn1": (row(ffn1_norm[l]),) + ffn_w(ffn1_gate, ffn1_up, ffn1_down),
        "ffn2": (row(ffn2_norm[l]),) + ffn_w(ffn2_gate, ffn2_up, ffn2_down),
        "w_in": w_in_p,
        "mix_norm": row(mix_norm[l]),
        "conv_w": ssd_conv_w[l].astype(F32),
        "conv_b": row(ssd_conv_b[l]),
        "dt_bias": pad_lanes(row(ssd_dt_bias[l]), LANES),
        "swa_q_gain": row(two(swa_q_norm[l])),
        "swa_k_gain": row(two(swa_k_norm[l])),
        "mla_qlat_norm": row(mla_q_lat_norm[l]),
        "w_uq": wuq,
        "mla_kv_norm": row(mla_kv_norm[l]),
        "w_ukv": wukv_p,
        "mla_q_gain": pad_lanes(row(mla_q_norm[l]), LANES),
        "mla_k_gain": pad_lanes(row(mla_k_norm[l]), LANES),
        "ssd_a_log": pad_lanes(row(ssd_a_log[l]), LANES),
        "ssd_d": row(jnp.repeat(ssd_d[l], SSD_HEAD_DIM)),
        "ssd_norm": row(ssd_norm[l]),
        "swa_sink": swa_sink[l].astype(F32),
        "swa_out_norm": row(_swa_head_perm(swa_out_norm[l], 0)),
        "mla_out_norm": row(mla_out_norm[l]),
        "w_out": w_out_p,
    }


def _tiles(seq):
    pick = lambda pref: math.gcd(seq, pref)
    return {"ffn": pick(1024), "proj": pick(512), "mla_q": pick(256), "mla_k": pick(1024),
            "swa_blocks": pick(4 * SWA_BLOCK) // SWA_BLOCK,
            "ssd_chunks": pick(2 * SSD_CHUNK) // SSD_CHUNK}


def kernel(x, ffn1_norm, ffn1_gate, ffn1_up, ffn1_down, mix_norm, w_in, ssd_conv_w, ssd_conv_b, ssd_dt_bias, ssd_a_log, ssd_d, ssd_norm, swa_q_norm, swa_k_norm, swa_sink, swa_out_norm, mla_q_lat_norm, mla_w_uq, mla_kv_norm, mla_w_ukv, mla_q_norm, mla_k_norm, mla_out_norm, w_out, ffn2_norm, ffn2_gate, ffn2_up, ffn2_down):
    batch, seq, d = x.shape
    assert seq % SSD_CHUNK == 0 and seq % SWA_BLOCK == 0
    params = (ffn1_norm, ffn1_gate, ffn1_up, ffn1_down, mix_norm, w_in, ssd_conv_w, ssd_conv_b,
              ssd_dt_bias, ssd_a_log, ssd_d, ssd_norm, swa_q_norm, swa_k_norm, swa_sink,
              swa_out_norm, mla_q_lat_norm, mla_w_uq, mla_kv_norm, mla_w_ukv, mla_q_norm,
              mla_k_norm, mla_out_norm, w_out, ffn2_norm, ffn2_gate, ffn2_up, ffn2_down)
    tl = _tiles(seq)
    tabs = _rope_tables(seq)
    xf = x.reshape(batch * seq, d).astype(F32)
    for l in range(w_in.shape[0]):
        p = _prep_layer_params(l, *params)
        xf = _ffn(xf, *p["ffn1"], tm=tl["ffn"])
        z, xbc, dt, swq, swk, swv, mq, mk, mv = _in_proj(xf, tabs, p, seq, tl["proj"])
        y_ssd = _ssd(xbc, dt, z, p, batch, seq, tl["ssd_chunks"])
        y_swa = _swa(swq, swk, swv, p["swa_sink"], p["swa_out_norm"], batch, seq,
                     tl["swa_blocks"])
        y_mla = _mla(mq, mk, mv, p["mla_out_norm"], batch, seq, tl["mla_q"], tl["mla_k"])
        xf = _ffn(xf, *p["ffn2"], tm=tl["ffn"], mixer=(y_ssd, y_swa, y_mla, p["w_out"]))
    return xf.reshape(batch, seq, d).astype(x.dtype)
```

```python
import functools
import math

import jax
import jax.numpy as jnp
import numpy as np
from jax import lax
from jax.experimental import pallas as pl
from jax.experimental.pallas import tpu as pltpu

F32 = jnp.float32
BF16 = jnp.bfloat16

LANES = 128
HALF = LANES // 2

SSD_HEADS = 8
SSD_HEAD_DIM = 64
SSD_INNER = SSD_HEADS * SSD_HEAD_DIM
SSD_GROUPS = 2
SSD_STATE = 128
SSD_CONV = 5
SSD_CHUNK = 256
SSD_CONV_DIM = SSD_INNER + 2 * SSD_GROUPS * SSD_STATE
SWA_HEADS = 4
SWA_KV_HEADS = 2
SWA_HEAD_DIM = 64
SWA_WINDOW = 128
SWA_BLOCK = 128
SWA_WIDTH = SWA_HEADS * SWA_HEAD_DIM
MLA_HEADS = 4
MLA_Q_RANK = 256
MLA_KV_RANK = 128
MLA_NOPE = 64
MLA_ROPE = 32
MLA_QK = MLA_NOPE + MLA_ROPE
MLA_V = 64
MLA_WIDTH = MLA_HEADS * MLA_V
ROPE_THETA = 10000.0
EPS = 1e-6

C_Z = 0
C_XBC = C_Z + SSD_INNER
C_SWQ = C_XBC + SSD_CONV_DIM
C_SWK = C_SWQ + SWA_WIDTH
C_SWV = C_SWK + LANES
C_MLQ = C_SWV + LANES
C_CKV = C_MLQ + MLA_Q_RANK
C_MISC = C_CKV + MLA_KV_RANK
C_END = C_MISC + LANES
CONV_HALO = 16
FFN_CHUNK = 256
NEG_INF = float("-inf")
LOG_FLOOR = -1e30
LOG2E = math.log2(math.e)
MLA_BOUND_LIMIT = 48.0
MLA_BOUND_SLACK = 1.01


def _rms(x, w):
    ms = jnp.mean(x * x, axis=-1, keepdims=True)
    return x * lax.rsqrt(ms + EPS) * w


def _silu(x):
    return x / (1.0 + jnp.exp(-x))


def _dot(a, b):
    return jnp.dot(a, b, preferred_element_type=F32)


def _dot_nt(a, b):
    return lax.dot_general(a, b, (((1,), (1,)), ((), ())), preferred_element_type=F32)


def _dot_tn(a, b):
    return lax.dot_general(a, b, (((0,), (0,)), ((), ())), preferred_element_type=F32)


def _lane_iota(shape):
    return lax.broadcasted_iota(jnp.int32, shape, len(shape) - 1)


def _resident(param, single_buffer=False):
    mode = {"pipeline_mode": pl.Buffered(1)} if single_buffer else {}
    if isinstance(param, tuple):
        arr, layer = param
        rest = (0,) * (arr.ndim - 1)
        return arr, pl.BlockSpec((None,) + arr.shape[1:], lambda *_: (layer,) + rest, **mode)
    zeros = (0,) * param.ndim
    return param, pl.BlockSpec(param.shape, lambda *_: zeros, **mode)


def _cast_body(*refs):
    n = len(refs) // 2
    for src, dst in zip(refs[:n], refs[n:]):
        dst[...] = src[...].astype(dst.dtype)


def _cast_ffn_weights(l, gate, up, down, steps=4):
    _, d, f = gate.shape
    col = pl.BlockSpec((None, d // steps, f), lambda i: (l, i, 0))
    rowb = pl.BlockSpec((None, f // steps, d), lambda i: (l, i, 0))
    out_col = pl.BlockSpec((d // steps, f), lambda i: (i, 0))
    out_row = pl.BlockSpec((f // steps, d), lambda i: (i, 0))
    return pl.pallas_call(
        _cast_body,
        grid=(steps,),
        in_specs=[col, col, rowb],
        out_specs=[out_col, out_col, out_row],
        out_shape=[jax.ShapeDtypeStruct((d, f), BF16), jax.ShapeDtypeStruct((d, f), BF16),
                   jax.ShapeDtypeStruct((f, d), BF16)],
        compiler_params=pltpu.CompilerParams(
            dimension_semantics=("parallel",), vmem_limit_bytes=40 * 1024 * 1024),
        name="cast_ffn_weights",
    )(gate, up, down)


def _ffn_body(*refs, fuse_out_proj):
    if fuse_out_proj:
        x_ref, ys_ref, yw_ref, ym_ref, wo_ref = refs[:5]
        a, b = SSD_INNER, SSD_INNER + SWA_WIDTH
        x = (x_ref[...] + _dot(ys_ref[...], wo_ref[0:a, :]) + _dot(yw_ref[...], wo_ref[a:b, :])
             + _dot(ym_ref[...], wo_ref[b:, :]))
        refs = refs[5:]
    else:
        x_ref = refs[0]
        x = x_ref[...]
        refs = refs[1:]
    nw_ref, wg_ref, wu_ref, wd_ref, o_ref, xn_ref, acc_ref = refs
    o_ref[...] = x
    xn_ref[...] = _rms(x, nw_ref[...]).astype(BF16)
    for c in range(wg_ref.shape[1] // FFN_CHUNK):
        cols = slice(c * FFN_CHUNK, (c + 1) * FFN_CHUNK)
        xn = xn_ref[...]
        g = _dot(xn, wg_ref[:, cols])
        u = _dot(xn, wu_ref[:, cols])
        h = (_silu(g) * u).astype(BF16)
        part = _dot(h, wd_ref[cols, :])
        if c == 0:
            acc_ref[...] = part
        else:
            acc_ref[...] += part
    o_ref[...] += 0.5 * acc_ref[...]


def _ffn(x, nw, wg, wu, wd, tm, mixer=None):
    n, d = x.shape
    row = lambda w: pl.BlockSpec((tm, w), lambda i: (i, 0))
    args, specs = [x], [row(d)]
    resident = [_resident(nw)] + [_resident(w, single_buffer=True) for w in (wg, wu, wd)]
    if mixer is not None:
        args += list(mixer[:3])
        specs += [row(a.shape[1]) for a in mixer[:3]]
        resident.insert(0, _resident(mixer[3]))
    args += [a for a, _ in resident]
    specs += [s for _, s in resident]
    return pl.pallas_call(
        functools.partial(_ffn_body, fuse_out_proj=mixer is not None),
        grid=(n // tm,),
        in_specs=specs,
        out_specs=row(d),
        out_shape=jax.ShapeDtypeStruct((n, d), F32),
        scratch_shapes=[pltpu.VMEM((tm, d), BF16), pltpu.VMEM((tm, d), F32)],
        compiler_params=pltpu.CompilerParams(
            dimension_semantics=("parallel",), vmem_limit_bytes=56 * 1024 * 1024),
        name="ffn",
    )(*args)


def _rope(y, c, sa, sb, shift):
    return (y * c + pltpu.roll(y, LANES - shift, 1) * sa + pltpu.roll(y, shift, 1) * sb)


def _head_sum_matrix(width):
    r = lax.broadcasted_iota(jnp.int32, (LANES, LANES), 0)
    c = lax.broadcasted_iota(jnp.int32, (LANES, LANES), 1)
    same = (r < width) if width > HALF else ((r < HALF) == (c < HALF))
    return jnp.where(same, 1.0, 0.0).astype(BF16)


def _head_norm_rope(x, sumsq, gain, c, sa, sb, width, rope_dim, post_scale):
    y = x * lax.rsqrt(sumsq / float(width) + EPS) * gain
    out = _rope(y, c, sa, sb, rope_dim // 2)
    return out if post_scale == 1.0 else out * post_scale


def _inproj_body(x_ref, xp_ref, xnx_ref, tab_ref, w_ref, mixw_ref, convw_ref, convb_ref,
                 dtb_ref, swqg_ref, swkg_ref, qlw_ref, wuq_ref, kvw_ref, wukv_ref,
                 mqg_ref, mkg_ref,
                 z_ref, xbc_ref, dt_ref, swq_ref, swk_ref, swv_ref, mq_ref, mk_ref, mv_ref,
                 *, tiles_per_seq):
    i = pl.program_id(0)
    tm = x_ref.shape[0]
    pos_tile = i % tiles_per_seq
    mixw = mixw_ref[...]
    hn = _rms(x_ref[...], mixw).astype(BF16)
    hp = _rms(xp_ref[...], mixw).astype(BF16)
    hx = _rms(xnx_ref[...], mixw).astype(BF16)
    groups = lambda v: [v[:, g * LANES:(g + 1) * LANES] for g in range(v.shape[1] // LANES)]

    wx = w_ref[:, C_XBC:C_SWQ]
    z = _dot(hn, w_ref[:, C_Z:C_XBC])
    xbc_prev, xbc_main, xbc_next = _dot(hp, wx), _dot(hn, wx), _dot(hx, wx)
    ckv, misc = groups(_dot(hn, w_ref[:, C_CKV:C_END]))
    swa_q = groups(_dot(hn, w_ref[:, C_SWQ:C_SWK]))
    swa_k, swa_v = groups(_dot(hn, w_ref[:, C_SWK:C_MLQ]))
    ql = _dot(hn, w_ref[:, C_MLQ:C_CKV])

    q_up = groups(_dot(_rms(ql, qlw_ref[...]).astype(BF16), wuq_ref[...]))
    kv_up = groups(_dot(_rms(ckv, kvw_ref[...]).astype(BF16), wukv_ref[...]))
    lane = _lane_iota(misc.shape)
    kr = jnp.where((lane >= MLA_NOPE) & (lane < MLA_QK), misc, 0.0)
    mla_q = q_up
    mla_k = [k_nope + kr for k_nope in kv_up[:MLA_HEADS]]
    mla_v = kv_up[MLA_HEADS:]

    sum64, sum96 = _head_sum_matrix(SWA_HEAD_DIM), _head_sum_matrix(MLA_QK)
    sumsq = lambda xs, m: [_dot((v * v).astype(BF16), m) for v in xs]
    ss_swa_q, ss_swa_k = sumsq(swa_q, sum64), sumsq([swa_k], sum64)
    ss_mla_q, ss_mla_k = sumsq(mla_q, sum96), sumsq(mla_k, sum96)

    z_ref[...] = z.astype(BF16)

    keep_prev = (pos_tile > 0).astype(F32)
    keep_next = (pos_tile < tiles_per_seq - 1).astype(F32)
    padded = jnp.concatenate([xbc_prev * keep_prev, xbc_main, xbc_next * keep_next], axis=0)
    n_pad = padded.shape[0]
    conv = jnp.broadcast_to(convb_ref[...], (tm, SSD_CONV_DIM))
    for k in range(SSD_CONV):
        shift = SSD_CONV // 2 - k
        tap = padded if shift == 0 else pltpu.roll(padded, shift % n_pad, 0)
        conv = conv + convw_ref[k:k + 1, :] * tap[CONV_HALO:CONV_HALO + tm, :]
    xbc_ref[...] = _silu(conv).astype(BF16)

    dtv = misc + dtb_ref[...]
    dt_ref[...] = jnp.maximum(dtv, 0.0) + jnp.log1p(jnp.exp(-jnp.abs(dtv)))

    swa_rope = (tab_ref[0], tab_ref[1], tab_ref[2], SWA_HEAD_DIM, SWA_HEAD_DIM)
    for g, (v, ss) in enumerate(zip(swa_q, ss_swa_q)):
        swq_ref[:, g * LANES:(g + 1) * LANES] = _head_norm_rope(
            v, ss, swqg_ref[...], *swa_rope, SWA_HEAD_DIM ** -0.5 * LOG2E).astype(BF16)
    swk_ref[...] = _head_norm_rope(swa_k, ss_swa_k[0], swkg_ref[...], *swa_rope, 1.0).astype(BF16)
    swv_ref[...] = swa_v.astype(BF16)

    mla_rope = (tab_ref[3], tab_ref[4], tab_ref[5], MLA_QK, MLA_ROPE)
    for h in range(MLA_HEADS):
        sl = slice(h * LANES, (h + 1) * LANES)
        mq_ref[:, sl] = _head_norm_rope(mla_q[h], ss_mla_q[h], mqg_ref[...], *mla_rope,
                                        MLA_QK ** -0.5 * LOG2E).astype(BF16)
        mk_ref[:, sl] = _head_norm_rope(mla_k[h], ss_mla_k[h], mkg_ref[...], *mla_rope,
                                        1.0).astype(BF16)
        ones_lane = HALF if h % 2 == 0 else 0
        mv_ref[:, sl] = jnp.where(lane == ones_lane, 1.0, mla_v[h]).astype(BF16)


def _in_proj(x, tabs, p, seq, tm):
    n, d = x.shape
    tiles_per_seq = seq // tm
    halo_per_tile = tm // CONV_HALO
    n_halo = n // CONV_HALO
    row = lambda w: pl.BlockSpec((tm, w), lambda i: (i, 0))
    in_specs = [
        row(d),
        pl.BlockSpec((CONV_HALO, d), lambda i: (jnp.maximum(i * halo_per_tile - 1, 0), 0)),
        pl.BlockSpec((CONV_HALO, d),
                     lambda i: (jnp.minimum((i + 1) * halo_per_tile, n_halo - 1), 0)),
        pl.BlockSpec((6, tm, LANES), lambda i: (0, i % tiles_per_seq, 0)),
    ]
    names = ("w_in", "mix_norm", "conv_w", "conv_b", "dt_bias", "swa_q_gain", "swa_k_gain",
             "mla_qlat_norm", "w_uq", "mla_kv_norm", "w_ukv", "mla_q_gain", "mla_k_gain")
    resident = [_resident(p[k]) for k in names]
    args = [a for a, _ in resident]
    in_specs += [s for _, s in resident]
    widths = (SSD_INNER, SSD_CONV_DIM, LANES, SWA_WIDTH, LANES, LANES,
              MLA_HEADS * LANES, MLA_HEADS * LANES, MLA_HEADS * LANES)
    dts = (BF16, BF16, F32, BF16, BF16, BF16, BF16, BF16, BF16)
    return pl.pallas_call(
        functools.partial(_inproj_body, tiles_per_seq=tiles_per_seq),
        grid=(n // tm,),
        in_specs=in_specs,
        out_specs=[row(w) for w in widths],
        out_shape=[jax.ShapeDtypeStruct((n, w), t) for w, t in zip(widths, dts)],
        compiler_params=pltpu.CompilerParams(
            dimension_semantics=("parallel",), vmem_limit_bytes=48 * 1024 * 1024),
        name="in_proj",
    )(x, x, x, tabs, *args)


def _split_bf16(v, n):
    pieces = []
    for _ in range(n - 1):
        p = v.astype(BF16)
        pieces.append(p)
        v = v - p.astype(F32)
    pieces.append(v.astype(BF16))
    return pieces


def _pack_lanes(pieces, width):
    lane = _lane_iota(pieces[0].shape)
    out = jnp.zeros(pieces[0].shape, F32)
    for k, p in enumerate(pieces):
        pf = p.astype(F32)
        if k:
            pf = pltpu.roll(pf, k * width, 1)
        out = jnp.where((lane >= k * width) & (lane < (k + 1) * width), pf, out)
    return out.astype(BF16)


def _ssd_expand_matrices():
    nd = 2 * SSD_HEADS
    rows = np.arange(LANES)[:, None]
    head = np.arange(SSD_INNER)[None, :] // SSD_HEAD_DIM
    mats = []
    for direction in range(2):
        for first_piece in (0, 2):
            d = direction * SSD_HEADS + head
            hit = (rows == first_piece * nd + d) | (rows == (first_piece + 1) * nd + d)
            mats.append(hit)
    return jnp.asarray(np.stack(mats).astype(np.float32), BF16)


def _ssd_body(xbc_ref, dt_ref, z_ref, alog_ref, dskip_ref, nw_ref, expand_ref, o_ref,
              yacc_ref, sf_ref, sb_ref, cols_ref, *, n_chunks):
    pas = pl.program_id(1)
    j = pl.program_id(2)
    t = SSD_CHUNK
    cps = xbc_ref.shape[0] // t
    nh = SSD_HEADS

    gw = SSD_INNER // SSD_GROUPS
    hpg = nh // SSD_GROUPS
    nd = 2 * nh
    lane = _lane_iota((t, LANES))

    def b_of(rows, g):
        return xbc_ref[rows, SSD_INNER + g * SSD_STATE:SSD_INNER + (g + 1) * SSD_STATE]

    def c_of(rows, g):
        return xbc_ref[rows, SSD_INNER + (SSD_GROUPS + g) * SSD_STATE:
                       SSD_INNER + (SSD_GROUPS + g + 1) * SSD_STATE]

    def forward_early(k):
        rows = slice(k * t, (k + 1) * t)
        fwd = lane < nh
        dt = dt_ref[rows, :]
        avec = jnp.where(_lane_iota((1, LANES)) < nd, -jnp.exp(alog_ref[...]), 0.0)
        a = dt * avec
        row = lax.broadcasted_iota(jnp.int32, (t, t), 0)
        col = lax.broadcasted_iota(jnp.int32, (t, t), 1)
        tri = jnp.where(col <= row, 1.0, 0.0).astype(BF16)
        cum = sum(_dot(tri, part) for part in _split_bf16(a, 3))
        excl = cum - a
        tot = cum[t - 1:t, :]

        e_in = jnp.exp(jnp.where(fwd, cum, tot - excl))
        w_st = jnp.exp(jnp.where(fwd, tot - cum, excl)) * dt
        e1, e2 = _split_bf16(e_in, 2)
        w1, w2 = _split_bf16(w_st, 2)
        cols = _pack_lanes([e1, e2, w1, w2], nd)
        cols_ref[j * cps + k] = cols

        r = jnp.where(fwd, cum, -excl) * LOG2E
        cc = r - jnp.maximum(jnp.log(dt), LOG_FLOOR) * LOG2E
        ones = jnp.ones((t, LANES), F32)
        lhs = jnp.where(lane < 3 * nd, _pack_lanes(_split_bf16(r, 3), nd),
                        jnp.where(lane < 6 * nd, ones, 0.0).astype(BF16))
        cc_t = cc.T[0:nd, :]
        c1, c2, c3 = _split_bf16(cc_t, 3)
        rhs = jnp.concatenate(
            [jnp.ones((3 * nd, t), BF16), -c1, -c2, -c3, jnp.zeros((LANES - 6 * nd, t), BF16)],
            axis=0)
        sub = lax.broadcasted_iota(jnp.int32, (LANES, t), 0) % nd

        e_x = _dot(cols, expand_ref[0])
        w_x = _dot(cols, expand_ref[1])
        cbs = [_dot_nt(c_of(rows, g), b_of(rows, g)) for g in range(SSD_GROUPS)]
        ys = []
        for g in range(SSD_GROUPS):
            gs = slice(g * gw, (g + 1) * gw)
            s_in = sf_ref[g]
            xgf = xbc_ref[rows, gs].astype(F32)
            ys.append(dskip_ref[:, gs] * xgf + _dot(c_of(rows, g), s_in.astype(BF16)) * e_x[:, gs])
            xw = (xgf * w_x[:, gs]).astype(BF16)
            sf_ref[g] = s_in * e_x[t - 1:t, gs] + _dot_tn(b_of(rows, g), xw)
        exps = [_dot(lhs, jnp.where(sub == idx, rhs, jnp.zeros_like(rhs))) for idx in range(nd)]
        return cbs, ys, exps

    def forward_late(k, cbs, ys, exps):
        rows = slice(k * t, (k + 1) * t)
        th = t // 2
        quad = lambda v, qi, qj: v[qi * th:(qi + 1) * th, qj * th:(qj + 1) * th]
        qrow = lax.broadcasted_iota(jnp.int32, (th, th), 0)
        qcol = lax.broadcasted_iota(jnp.int32, (th, th), 1)
        on_low = qrow >= qcol
        on_up = qcol >= qrow

        def mixing(h, cb):
            df, db = exps[h], exps[nh + h]
            diag = [quad(cb, q, q) * (jnp.exp2(jnp.where(on_low, quad(df, q, q), NEG_INF))
                                      + jnp.exp2(jnp.where(on_up, quad(db, q, q), NEG_INF)))
                    for q in range(2)]
            upper = quad(cb, 0, 1) * jnp.exp2(quad(db, 0, 1))
            lower = quad(cb, 1, 0) * jnp.exp2(quad(df, 1, 0))
            return jnp.concatenate([jnp.concatenate([diag[0], upper], axis=1),
                                    jnp.concatenate([lower, diag[1]], axis=1)], axis=0)

        glane = _lane_iota((t, gw))
        for g in range(SSD_GROUPS):
            gs = slice(g * gw, (g + 1) * gw)
            xg = xbc_ref[rows, gs]
            y = ys[g]
            for hh in range(hpg):
                m = mixing(g * hpg + hh, cbs[g]).astype(BF16)
                mine = (glane >= hh * SSD_HEAD_DIM) & (glane < (hh + 1) * SSD_HEAD_DIM)
                y = y + _dot(m, jnp.where(mine, xg, jnp.zeros_like(xg)))
            yacc_ref[j * cps + k, :, gs] = y

    @pl.when(pas == 0)
    def _forward():
        @pl.when(j == 0)
        def _():
            sf_ref[...] = jnp.zeros_like(sf_ref)

        early = [forward_early(k) for k in range(cps)]
        for k in range(cps):
            forward_late(k, *early[k])

    @pl.when(pas == 1)
    def _backward():
        @pl.when(j == 0)
        def _():
            sb_ref[...] = jnp.zeros_like(sb_ref)

        for k in reversed(range(cps)):
            rows = slice(k * t, (k + 1) * t)
            c = n_chunks - 1 - j * cps - (cps - 1 - k)
            cols = cols_ref[c]
            e_x = _dot(cols, expand_ref[2])
            w_x = _dot(cols, expand_ref[3])
            gate = _silu(z_ref[rows, :].astype(F32))
            for g in range(SSD_GROUPS):
                gs = slice(g * gw, (g + 1) * gw)
                s_in = sb_ref[g]
                y = yacc_ref[c, :, gs] + _dot(c_of(rows, g), s_in.astype(BF16)) * e_x[:, gs]
                o_ref[rows, gs] = _rms(y * gate[:, gs], nw_ref[:, gs]).astype(BF16)
                xw = (xbc_ref[rows, gs].astype(F32) * w_x[:, gs]).astype(BF16)
                sb_ref[g] = s_in * e_x[0:1, gs] + _dot_tn(b_of(rows, g), xw)


def _ssd(xbc, dt, z, p, batch, seq, cps):
    t = SSD_CHUNK
    nc = seq // t
    ns = nc // cps

    def chunk_idx(b, pas, j):
        return b * ns + j + pas * (ns - 1 - 2 * j)

    def out_idx(b, pas, j):
        return b * ns + (ns - 1) - pas * j

    blk = lambda w, f: pl.BlockSpec((cps * t, w), lambda b, pas, j: (f(b, pas, j), 0))
    resident = [_resident(a) for a in (p["ssd_a_log"], p["ssd_d"], p["ssd_norm"],
                                       _ssd_expand_matrices())]
    state = pltpu.VMEM((SSD_GROUPS, SSD_STATE, SSD_INNER // SSD_GROUPS), F32)
    return pl.pallas_call(
        functools.partial(_ssd_body, n_chunks=nc),
        grid=(batch, 2, ns),
        in_specs=[blk(SSD_CONV_DIM, chunk_idx), blk(LANES, chunk_idx), blk(SSD_INNER, chunk_idx)]
                 + [s for _, s in resident],
        out_specs=blk(SSD_INNER, out_idx),
        out_shape=jax.ShapeDtypeStruct((batch * seq, SSD_INNER), BF16),
        scratch_shapes=[pltpu.VMEM((nc, t, SSD_INNER), F32), state, state,
                        pltpu.VMEM((nc, t, LANES), BF16)],
        compiler_params=pltpu.CompilerParams(
            dimension_semantics=("arbitrary", "arbitrary", "arbitrary"),
            vmem_limit_bytes=40 * 1024 * 1024),
        name="ssd",
    )(xbc, dt, z, *[a for a, _ in resident])


def _swa_body(sink_ref, q_ref, kp_ref, kc_ref, kn_ref, vp_ref, vc_ref, vn_ref, nw_ref, o_ref,
              *, layer):
    assert SWA_WINDOW == SWA_BLOCK
    j = pl.program_id(1)
    blk = SWA_BLOCK
    n_blk = q_ref.shape[0] // blk
    kb = jnp.concatenate([kp_ref[...], kc_ref[...], kn_ref[...]], axis=0)
    vb = jnp.concatenate([vp_ref[...], vc_ref[...], vn_ref[...]], axis=0)
    qi = lax.broadcasted_iota(jnp.int32, (blk, blk), 0)
    kj = lax.broadcasted_iota(jnp.int32, (blk, blk), 1)
    lo = _lane_iota((blk, LANES)) < HALF
    lo_v = _lane_iota(vb.shape) < HALF
    zero_q = jnp.zeros((blk, LANES), BF16)
    lane_v = _lane_iota(vb.shape)
    unit = lambda at: jnp.where(lane_v == at, 1.0, 0.0).astype(BF16)
    v_half = (jnp.where(lo_v, vb, unit(HALF)), jnp.where(lo_v, unit(0), vb))
    first = jnp.where(j == 0, blk, 0)
    last = jnp.where(j == pl.num_programs(1) - 1, blk, 0)
    items = [(t, g, half) for t in range(n_blk) for g in range(SWA_WIDTH // LANES)
             for half in range(2)]
    logits = {}
    for t, g, half in items:
        qg = q_ref[t * blk:(t + 1) * blk, g * LANES:(g + 1) * LANES]
        qm = jnp.where(lo, qg, zero_q) if half == 0 else jnp.where(lo, zero_q, qg)
        logits[t, g, half] = _dot_nt(qm, kb[t * blk:(t + 3) * blk])
    scaled = {}
    for t, g, half in items:
        keep_prev = kj >= (qi + first if t == 0 else qi)
        keep_next = kj <= (qi - last if t == n_blk - 1 else qi)
        s = logits[t, g, half]
        s_prev = jnp.where(keep_prev, s[:, :blk], NEG_INF)
        s_own = s[:, blk:2 * blk]
        s_next = jnp.where(keep_next, s[:, 2 * blk:], NEG_INF)
        sk = sink_ref[layer, g + 2 * half] * LOG2E
        m = jnp.max(jnp.maximum(jnp.maximum(s_prev, s_own), s_next), axis=-1, keepdims=True)
        m = jnp.maximum(m, sk)
        pexp = jnp.concatenate([jnp.exp2(v - m) for v in (s_prev, s_own, s_next)],
                               axis=-1).astype(BF16)
        acc = _dot(pexp, v_half[half][t * blk:(t + 3) * blk])
        ones_lane = HALF if half == 0 else 0
        den = acc[:, ones_lane:ones_lane + 1] + jnp.exp2(sk - m)
        scaled[t, g, half] = acc / den
    for t in range(n_blk):
        y = jnp.concatenate([jnp.where(lo, scaled[t, g, 0], scaled[t, g, 1])
                             for g in range(SWA_WIDTH // LANES)], axis=-1)
        o_ref[t * blk:(t + 1) * blk, :] = _rms(y, nw_ref[...]).astype(BF16)


def _swa(q, k, v, sink, nw, batch, seq, n_blk):
    blk = SWA_BLOCK
    nb = seq // blk
    ns = nb // n_blk
    cur = lambda b, j: (b * ns + j, 0)
    prv = lambda b, j: (b * nb + jnp.maximum(j * n_blk - 1, 0), 0)
    nxt = lambda b, j: (b * nb + jnp.minimum((j + 1) * n_blk, nb - 1), 0)
    edge = lambda f: pl.BlockSpec((blk, LANES), f)
    main = pl.BlockSpec((n_blk * blk, LANES), cur)
    sinks, layer = sink
    nw_arr, nw_spec = _resident(nw)
    return pl.pallas_call(
        functools.partial(_swa_body, layer=layer),
        grid=(batch, ns),
        in_specs=[pl.BlockSpec(memory_space=pltpu.SMEM),
                  pl.BlockSpec((n_blk * blk, SWA_WIDTH), cur),
                  edge(prv), main, edge(nxt), edge(prv), main, edge(nxt), nw_spec],
        out_specs=pl.BlockSpec((n_blk * blk, SWA_WIDTH), cur),
        out_shape=jax.ShapeDtypeStruct((batch * seq, SWA_WIDTH), BF16),
        compiler_params=pltpu.CompilerParams(dimension_semantics=("parallel", "parallel")),
        name="swa",
    )(sinks, q, k, k, k, v, v, v, nw_arr)


def _mla_body(q_ref, k_ref, v_ref, nw_ref, o_ref, knorm_ref, *, key_chunk):
    i = pl.program_id(1)
    tq = q_ref.shape[0]
    seq = k_ref.shape[0]
    heads = [slice(h * LANES, (h + 1) * LANES) for h in range(MLA_HEADS)]
    chunks = [slice(c * key_chunk, (c + 1) * key_chunk) for c in range(seq // key_chunk)]
    lo = _lane_iota((tq, LANES)) < HALF

    @pl.when(i == 0)
    def _key_norms():
        for h, sl in enumerate(heads):
            kf = k_ref[:, sl].astype(F32)
            ss = jnp.max(jnp.sum(kf * kf, axis=-1, keepdims=True), axis=0, keepdims=True)
            knorm_ref[h:h + 1, :] = jnp.broadcast_to(jnp.sqrt(ss), (1, LANES))

    bounds = []
    for h, sl in enumerate(heads):
        qf = q_ref[:, sl].astype(F32)
        qn = jnp.sqrt(jnp.sum(qf * qf, axis=-1, keepdims=True))
        bounds.append(qn * knorm_ref[h:h + 1, 0:1] * MLA_BOUND_SLACK)
    bmax = jnp.max(jnp.maximum(jnp.maximum(bounds[0], bounds[1]),
                               jnp.maximum(bounds[2], bounds[3])))
    use_bound = bmax <= MLA_BOUND_LIMIT

    def denom(acc, h):
        return acc[:, HALF:HALF + 1] if h % 2 == 0 else acc[:, 0:1]

    def finish(accs, dens):
        outs = [jnp.where(lo, accs[2 * p] / dens[2 * p], accs[2 * p + 1] / dens[2 * p + 1])
                for p in range(MLA_HEADS // 2)]
        y = jnp.concatenate(outs, axis=-1)
        o_ref[...] = _rms(y, nw_ref[...]).astype(BF16)

    @pl.when(use_bound)
    def _bounded():
        accs = []
        for h, sl in enumerate(heads):
            qh = q_ref[:, sl]
            acc = None
            for ks in chunks:
                pexp = jnp.exp2(_dot_nt(qh, k_ref[ks, sl]) - bounds[h]).astype(BF16)
                part = _dot(pexp, v_ref[ks, sl])
                acc = part if acc is None else acc + part
            accs.append(acc)
        finish(accs, [denom(a, h) for h, a in enumerate(accs)])

    @pl.when(jnp.logical_not(use_bound))
    def _online():
        accs, dens = [], []
        for h, sl in enumerate(heads):
            qh = q_ref[:, sl]
            m = jnp.full((tq, 1), NEG_INF, F32)
            acc = jnp.zeros((tq, LANES), F32)
            for ks in chunks:
                s = _dot_nt(qh, k_ref[ks, sl])
                m_new = jnp.maximum(m, jnp.max(s, axis=-1, keepdims=True))
                pexp = jnp.exp2(s - m_new).astype(BF16)
                acc = jnp.exp2(m - m_new) * acc + _dot(pexp, v_ref[ks, sl])
                m = m_new
            accs.append(acc)
            dens.append(denom(acc, h))
        finish(accs, dens)


def _mla(q, k, v, nw, batch, seq, tq, key_chunk):
    nq = seq // tq
    w = MLA_HEADS * LANES
    full = pl.BlockSpec((seq, w), lambda b, i: (b, 0))
    nw, nw_spec = _resident(nw)
    return pl.pallas_call(
        functools.partial(_mla_body, key_chunk=key_chunk),
        grid=(batch, nq),
        in_specs=[pl.BlockSpec((tq, w), lambda b, i: (b * nq + i, 0)), full, full, nw_spec],
        out_specs=pl.BlockSpec((tq, MLA_WIDTH), lambda b, i: (b * nq + i, 0)),
        out_shape=jax.ShapeDtypeStruct((batch * seq, MLA_WIDTH), BF16),
        scratch_shapes=[pltpu.VMEM((8, LANES), F32)],
        compiler_params=pltpu.CompilerParams(
            dimension_semantics=("arbitrary", "arbitrary"),
            vmem_limit_bytes=48 * 1024 * 1024),
        name="mla",
    )(q, k, v, nw)


def _rope_tables(seq):
    def angles(dim):
        inv = 1.0 / np.power(ROPE_THETA, np.arange(0, dim, 2, dtype=np.float64) / dim)
        return np.arange(seq, dtype=np.float64)[:, None] * inv[None, :]

    a64 = angles(SWA_HEAD_DIM)
    c, s = np.cos(a64), np.sin(a64)
    zero = np.zeros_like(c)
    c64 = np.concatenate([c, c, c, c], axis=-1)
    sa64 = np.concatenate([-s, zero, -s, zero], axis=-1)
    sb64 = np.concatenate([zero, s, zero, s], axis=-1)
    a32 = angles(MLA_ROPE)
    c, s = np.cos(a32), np.sin(a32)
    zero = np.zeros_like(c)
    ones = np.ones((seq, MLA_NOPE))
    pad = np.zeros((seq, LANES - MLA_QK))
    zn = np.zeros((seq, MLA_NOPE))
    cm = np.concatenate([ones, c, c, pad], axis=-1)
    sam = np.concatenate([zn, -s, zero, pad], axis=-1)
    sbm = np.concatenate([zn, zero, s, pad], axis=-1)
    return jnp.asarray(np.stack([c64, sa64, sb64, cm, sam, sbm]).astype(np.float32))


def _swa_head_perm(t, axis):
    parts = jnp.split(t, SWA_HEADS, axis=axis)
    return jnp.concatenate([parts[0], parts[2], parts[1], parts[3]], axis=axis)


def _prep_params(ffn1_norm, mix_norm, w_in, ssd_conv_w, ssd_conv_b, ssd_dt_bias, ssd_a_log,
                 ssd_d, ssd_norm, swa_q_norm, swa_k_norm, swa_sink, swa_out_norm,
                 mla_q_lat_norm, mla_w_uq, mla_kv_norm, mla_w_ukv, mla_q_norm, mla_k_norm,
                 mla_out_norm, w_out, ffn2_norm):
    nl, d = w_in.shape[:2]
    row = lambda v: v.reshape(nl, 1, -1).astype(F32)
    pad_lanes = lambda v, n: jnp.pad(v, [(0, 0)] * (v.ndim - 1) + [(0, n - v.shape[-1])])

    wi = w_in.astype(BF16)
    o = 0
    cols = {}
    for name, size in (("z", SSD_INNER), ("xbc", SSD_CONV_DIM), ("dt", 2 * SSD_HEADS),
                       ("swq", SWA_WIDTH), ("swk", LANES), ("swv", LANES),
                       ("mlq", MLA_Q_RANK), ("ckv", MLA_KV_RANK), ("kr", MLA_ROPE)):
        cols[name] = wi[:, :, o:o + size]
        o += size
    zc = lambda n: jnp.zeros((nl, d, n), BF16)
    misc = jnp.concatenate([cols["dt"], zc(MLA_NOPE - 2 * SSD_HEADS), cols["kr"],
                            zc(LANES - MLA_QK)], axis=-1)
    w_in_p = jnp.concatenate([cols["z"], cols["xbc"], _swa_head_perm(cols["swq"], 2),
                              cols["swk"], cols["swv"], cols["mlq"], cols["ckv"], misc], axis=-1)

    wuq = mla_w_uq.reshape(nl, MLA_Q_RANK, MLA_HEADS, MLA_QK)
    wuq = pad_lanes(wuq, LANES).reshape(nl, MLA_Q_RANK, MLA_HEADS * LANES).astype(BF16)
    wukv = mla_w_ukv.reshape(nl, MLA_KV_RANK, MLA_HEADS, MLA_NOPE + MLA_V)
    knope = pad_lanes(wukv[..., :MLA_NOPE], LANES)
    vv = wukv[..., MLA_NOPE:]
    zv = jnp.zeros_like(vv)
    even = (np.arange(MLA_HEADS) % 2 == 0)[None, None, :, None]
    vpad = jnp.concatenate([jnp.where(even, vv, zv), jnp.where(even, zv, vv)], axis=-1)
    wukv_p = jnp.concatenate([knope.reshape(nl, MLA_KV_RANK, -1),
                              vpad.reshape(nl, MLA_KV_RANK, -1)], axis=-1).astype(BF16)

    a, b = SSD_INNER, SSD_INNER + SWA_WIDTH
    w_out_p = jnp.concatenate([w_out[:, :a], _swa_head_perm(w_out[:, a:b], 1), w_out[:, b:]],
                              axis=1).astype(BF16)

    two = lambda v: jnp.concatenate([v, v], axis=-1)
    return {
        "ffn1_norm": row(ffn1_norm),
        "ffn2_norm": row(ffn2_norm),
        "w_in": w_in_p,
        "mix_norm": row(mix_norm),
        "conv_w": ssd_conv_w.astype(F32),
        "conv_b": row(ssd_conv_b),
        "dt_bias": pad_lanes(row(ssd_dt_bias), LANES),
        "swa_q_gain": row(two(swa_q_norm)),
        "swa_k_gain": row(two(swa_k_norm)),
        "mla_qlat_norm": row(mla_q_lat_norm),
        "w_uq": wuq,
        "mla_kv_norm": row(mla_kv_norm),
        "w_ukv": wukv_p,
        "mla_q_gain": pad_lanes(row(mla_q_norm), LANES),
        "mla_k_gain": pad_lanes(row(mla_k_norm), LANES),
        "ssd_a_log": pad_lanes(row(ssd_a_log), LANES),
        "ssd_d": row(jnp.repeat(ssd_d, SSD_HEAD_DIM, axis=-1)),
        "ssd_norm": row(ssd_norm),
        "swa_sink": swa_sink.astype(F32),
        "swa_out_norm": row(_swa_head_perm(swa_out_norm, 1)),
        "mla_out_norm": row(mla_out_norm),
        "w_out": w_out_p,
    }


def _tiles(seq):
    pick = lambda pref: math.gcd(seq, pref)
    return {"ffn": pick(1024), "proj": pick(512), "mla_q": pick(256), "mla_k": pick(1024),
            "swa_blocks": pick(16 * SWA_BLOCK) // SWA_BLOCK,
            "ssd_chunks": pick(4 * SSD_CHUNK) // SSD_CHUNK}


def kernel(x, ffn1_norm, ffn1_gate, ffn1_up, ffn1_down, mix_norm, w_in, ssd_conv_w, ssd_conv_b, ssd_dt_bias, ssd_a_log, ssd_d, ssd_norm, swa_q_norm, swa_k_norm, swa_sink, swa_out_norm, mla_q_lat_norm, mla_w_uq, mla_kv_norm, mla_w_ukv, mla_q_norm, mla_k_norm, mla_out_norm, w_out, ffn2_norm, ffn2_gate, ffn2_up, ffn2_down):
    batch, seq, d = x.shape
    assert seq % SSD_CHUNK == 0 and seq % SWA_BLOCK == 0
    assert ffn1_gate.shape[-1] % FFN_CHUNK == 0
    stacked = _prep_params(ffn1_norm, mix_norm, w_in, ssd_conv_w, ssd_conv_b, ssd_dt_bias,
                           ssd_a_log, ssd_d, ssd_norm, swa_q_norm, swa_k_norm, swa_sink,
                           swa_out_norm, mla_q_lat_norm, mla_w_uq, mla_kv_norm, mla_w_ukv,
                           mla_q_norm, mla_k_norm, mla_out_norm, w_out, ffn2_norm)
    tl = _tiles(seq)
    tabs = _rope_tables(seq)
    xf = x.reshape(batch * seq, d).astype(F32)
    for l in range(w_in.shape[0]):
        p = {k: (v, l) for k, v in stacked.items()}
        ffn1_w = _cast_ffn_weights(l, ffn1_gate, ffn1_up, ffn1_down)
        ffn2_w = _cast_ffn_weights(l, ffn2_gate, ffn2_up, ffn2_down)
        xf = _ffn(xf, p["ffn1_norm"], *ffn1_w, tm=tl["ffn"])
        z, xbc, dt, swq, swk, swv, mq, mk, mv = _in_proj(xf, tabs, p, seq, tl["proj"])
        y_ssd = _ssd(xbc, dt, z, p, batch, seq, tl["ssd_chunks"])
        y_swa = _swa(swq, swk, swv, p["swa_sink"], p["swa_out_norm"], batch, seq,
                     tl["swa_blocks"])
        y_mla = _mla(mq, mk, mv, p["mla_out_norm"], batch, seq, tl["mla_q"], tl["mla_k"])
        xf = _ffn(xf, p["ffn2_norm"], *ffn2_w, tm=tl["ffn"],
                  mixer=(y_ssd, y_swa, y_mla, p["w_out"]))
    return xf.reshape(batch, seq, d).astype(x.dtype)
```

```python
import functools
import math

import jax
import jax.numpy as jnp
import numpy as np
from jax import lax
from jax.experimental import pallas as pl
from jax.experimental.pallas import tpu as pltpu

F32 = jnp.float32
BF16 = jnp.bfloat16

LANES = 128
HALF = LANES // 2

SSD_HEADS = 8
SSD_HEAD_DIM = 64
SSD_INNER = SSD_HEADS * SSD_HEAD_DIM
SSD_GROUPS = 2
SSD_STATE = 128
SSD_CONV = 5
SSD_CHUNK = 256
SSD_CONV_DIM = SSD_INNER + 2 * SSD_GROUPS * SSD_STATE
SWA_HEADS = 4
SWA_KV_HEADS = 2
SWA_HEAD_DIM = 64
SWA_WINDOW = 128
SWA_BLOCK = 128
SWA_WIDTH = SWA_HEADS * SWA_HEAD_DIM
MLA_HEADS = 4
MLA_Q_RANK = 256
MLA_KV_RANK = 128
MLA_NOPE = 64
MLA_ROPE = 32
MLA_QK = MLA_NOPE + MLA_ROPE
MLA_V = 64
MLA_WIDTH = MLA_HEADS * MLA_V
ROPE_THETA = 10000.0
EPS = 1e-6

C_Z = 0
C_XBC = C_Z + SSD_INNER
C_SWQ = C_XBC + SSD_CONV_DIM
C_SWK = C_SWQ + SWA_WIDTH
C_SWV = C_SWK + LANES
C_MLQ = C_SWV + LANES
C_CKV = C_MLQ + MLA_Q_RANK
C_MISC = C_CKV + MLA_KV_RANK
C_END = C_MISC + LANES
CONV_HALO = 16
FFN_CHUNK = 256
FFN_RESIDUAL_SCALE = 0.5
NEG_INF = float("-inf")
LOG_FLOOR = -1e30
LOG2E = math.log2(math.e)
MLA_BOUND_LIMIT = 48.0
MLA_BOUND_SLACK = 1.01


def _rms(x, w):
    ms = jnp.mean(x * x, axis=-1, keepdims=True)
    return x * lax.rsqrt(ms + EPS) * w


def _silu(x):
    return x / (1.0 + jnp.exp(-x))


def _dot(a, b):
    return jnp.dot(a, b, preferred_element_type=F32)


def _dot_nt(a, b):
    return lax.dot_general(a, b, (((1,), (1,)), ((), ())), preferred_element_type=F32)


def _dot_tn(a, b):
    return lax.dot_general(a, b, (((0,), (0,)), ((), ())), preferred_element_type=F32)


def _lane_iota(shape):
    return lax.broadcasted_iota(jnp.int32, shape, len(shape) - 1)


def _resident(param, single_buffer=False):
    mode = {"pipeline_mode": pl.Buffered(1)} if single_buffer else {}
    if isinstance(param, tuple):
        arr, layer = param
        rest = (0,) * (arr.ndim - 1)
        return arr, pl.BlockSpec((None,) + arr.shape[1:], lambda *_: (layer,) + rest, **mode)
    zeros = (0,) * param.ndim
    return param, pl.BlockSpec(param.shape, lambda *_: zeros, **mode)


def _cast_body(gate_ref, up_ref, down_ref, gate_out, up_out, down_out):
    gate_out[...] = gate_ref[...].astype(BF16)
    up_out[...] = up_ref[...].astype(BF16)
    down_out[...] = (down_ref[...] * FFN_RESIDUAL_SCALE).astype(BF16)


def _cast_ffn_weights(l, gate, up, down, steps=4):
    _, d, f = gate.shape
    col = pl.BlockSpec((None, d // steps, f), lambda i: (l, i, 0))
    rowb = pl.BlockSpec((None, f // steps, d), lambda i: (l, i, 0))
    out_col = pl.BlockSpec((d // steps, f), lambda i: (i, 0))
    out_row = pl.BlockSpec((f // steps, d), lambda i: (i, 0))
    return pl.pallas_call(
        _cast_body,
        grid=(steps,),
        in_specs=[col, col, rowb],
        out_specs=[out_col, out_col, out_row],
        out_shape=[jax.ShapeDtypeStruct((d, f), BF16), jax.ShapeDtypeStruct((d, f), BF16),
                   jax.ShapeDtypeStruct((f, d), BF16)],
        compiler_params=pltpu.CompilerParams(
            dimension_semantics=("parallel",), vmem_limit_bytes=40 * 1024 * 1024),
        name="cast_ffn_weights",
    )(gate, up, down)


def _ffn_body(*refs, fuse_out_proj):
    if fuse_out_proj:
        x_ref, ys_ref, yw_ref, ym_ref, wo_ref = refs[:5]
        a, b = SSD_INNER, SSD_INNER + SWA_WIDTH
        x = (x_ref[...] + _dot(ys_ref[...], wo_ref[0:a, :]) + _dot(yw_ref[...], wo_ref[a:b, :])
             + _dot(ym_ref[...], wo_ref[b:, :]))
        refs = refs[5:]
    else:
        x_ref = refs[0]
        x = x_ref[...]
        refs = refs[1:]
    nw_ref, wg_ref, wu_ref, wd_ref, o_ref, xn_ref = refs
    o_ref[...] = x
    xn_ref[...] = _rms(x, nw_ref[...]).astype(BF16)
    for c in range(wg_ref.shape[1] // FFN_CHUNK):
        cols = slice(c * FFN_CHUNK, (c + 1) * FFN_CHUNK)
        xn = xn_ref[...]
        g = _dot(xn, wg_ref[:, cols])
        u = _dot(xn, wu_ref[:, cols])
        h = (_silu(g) * u).astype(BF16)
        o_ref[...] += _dot(h, wd_ref[cols, :])


def _ffn(x, nw, wg, wu, wd, tm, mixer=None):
    n, d = x.shape
    row = lambda w: pl.BlockSpec((tm, w), lambda i: (i, 0))
    args, specs = [x], [row(d)]
    resident = [_resident(nw)] + [_resident(w, single_buffer=True) for w in (wg, wu, wd)]
    if mixer is not None:
        args += list(mixer[:3])
        specs += [row(a.shape[1]) for a in mixer[:3]]
        resident.insert(0, _resident(mixer[3]))
    args += [a for a, _ in resident]
    specs += [s for _, s in resident]
    return pl.pallas_call(
        functools.partial(_ffn_body, fuse_out_proj=mixer is not None),
        grid=(n // tm,),
        in_specs=specs,
        out_specs=row(d),
        out_shape=jax.ShapeDtypeStruct((n, d), F32),
        scratch_shapes=[pltpu.VMEM((tm, d), BF16)],
        compiler_params=pltpu.CompilerParams(
            dimension_semantics=("parallel",), vmem_limit_bytes=56 * 1024 * 1024),
        name="ffn",
    )(*args)


def _rope(y, c, sa, sb, shift):
    return (y * c + pltpu.roll(y, LANES - shift, 1) * sa + pltpu.roll(y, shift, 1) * sb)


def _head_sum_matrix(width):
    r = lax.broadcasted_iota(jnp.int32, (LANES, LANES), 0)
    c = lax.broadcasted_iota(jnp.int32, (LANES, LANES), 1)
    same = (r < width) if width > HALF else ((r < HALF) == (c < HALF))
    return jnp.where(same, 1.0, 0.0).astype(BF16)


def _head_norm_rope(x, sumsq, gain, c, sa, sb, width, rope_dim, post_scale):
    y = x * lax.rsqrt(sumsq / float(width) + EPS) * gain
    out = _rope(y, c, sa, sb, rope_dim // 2)
    return out if post_scale == 1.0 else out * post_scale


def _inproj_body(x_ref, xp_ref, xnx_ref, tab_ref, w_ref, mixw_ref, convw_ref, convb_ref,
                 dtb_ref, swqg_ref, swkg_ref, qlw_ref, wuq_ref, kvw_ref, wukv_ref,
                 mqg_ref, mkg_ref,
                 z_ref, xbc_ref, dt_ref, swq_ref, swk_ref, swv_ref, mq_ref, mk_ref, mv_ref,
                 *, tiles_per_seq):
    i = pl.program_id(0)
    tm = x_ref.shape[0]
    pos_tile = i % tiles_per_seq
    mixw = mixw_ref[...]
    hn = _rms(x_ref[...], mixw).astype(BF16)
    hp = _rms(xp_ref[...], mixw).astype(BF16)
    hx = _rms(xnx_ref[...], mixw).astype(BF16)
    groups = lambda v: [v[:, g * LANES:(g + 1) * LANES] for g in range(v.shape[1] // LANES)]

    z = _dot(hn, w_ref[:, C_Z:C_XBC])
    conv_w = 2 * LANES
    xbc_parts = []
    for c0 in range(C_XBC, C_SWQ, conv_w):
        wx = w_ref[:, c0:c0 + conv_w]
        xbc_parts.append((_dot(hp, wx), _dot(hn, wx), _dot(hx, wx)))
    ckv, misc = groups(_dot(hn, w_ref[:, C_CKV:C_END]))
    swa_q = groups(_dot(hn, w_ref[:, C_SWQ:C_SWK]))
    swa_k, swa_v = groups(_dot(hn, w_ref[:, C_SWK:C_MLQ]))
    ql = _dot(hn, w_ref[:, C_MLQ:C_CKV])

    q_up = groups(_dot(_rms(ql, qlw_ref[...]).astype(BF16), wuq_ref[...]))
    kv_up = groups(_dot(_rms(ckv, kvw_ref[...]).astype(BF16), wukv_ref[...]))
    lane = _lane_iota(misc.shape)
    kr = jnp.where((lane >= MLA_NOPE) & (lane < MLA_QK), misc, 0.0)
    mla_q = q_up
    mla_k = [k_nope + kr for k_nope in kv_up[:MLA_HEADS]]
    mla_v = kv_up[MLA_HEADS:]

    sum64, sum96 = _head_sum_matrix(SWA_HEAD_DIM), _head_sum_matrix(MLA_QK)
    sumsq = lambda xs, m: [_dot((v * v).astype(BF16), m) for v in xs]
    ss_swa_q, ss_swa_k = sumsq(swa_q, sum64), sumsq([swa_k], sum64)
    ss_mla_q, ss_mla_k = sumsq(mla_q, sum96), sumsq(mla_k, sum96)

    z_ref[...] = z.astype(BF16)

    keep_prev = (pos_tile > 0).astype(F32)
    keep_next = (pos_tile < tiles_per_seq - 1).astype(F32)
    for part, (xbc_prev, xbc_main, xbc_next) in enumerate(xbc_parts):
        cs = slice(part * conv_w, (part + 1) * conv_w)
        padded = jnp.concatenate([xbc_prev * keep_prev, xbc_main, xbc_next * keep_next], axis=0)
        n_pad = padded.shape[0]
        conv = jnp.broadcast_to(convb_ref[:, cs], (tm, conv_w))
        for k in range(SSD_CONV):
            shift = SSD_CONV // 2 - k
            tap = padded if shift == 0 else pltpu.roll(padded, shift % n_pad, 0)
            conv = conv + convw_ref[k:k + 1, cs] * tap[CONV_HALO:CONV_HALO + tm, :]
        xbc_ref[:, cs] = _silu(conv).astype(BF16)

    dtv = misc + dtb_ref[...]
    dt_ref[...] = jnp.maximum(dtv, 0.0) + jnp.log1p(jnp.exp(-jnp.abs(dtv)))

    swa_rope = (tab_ref[0], tab_ref[1], tab_ref[2], SWA_HEAD_DIM, SWA_HEAD_DIM)
    for g, (v, ss) in enumerate(zip(swa_q, ss_swa_q)):
        swq_ref[:, g * LANES:(g + 1) * LANES] = _head_norm_rope(
            v, ss, swqg_ref[...], *swa_rope, SWA_HEAD_DIM ** -0.5 * LOG2E).astype(BF16)
    swk_ref[...] = _head_norm_rope(swa_k, ss_swa_k[0], swkg_ref[...], *swa_rope, 1.0).astype(BF16)
    swv_ref[...] = swa_v.astype(BF16)

    mla_rope = (tab_ref[3], tab_ref[4], tab_ref[5], MLA_QK, MLA_ROPE)
    for h in range(MLA_HEADS):
        sl = slice(h * LANES, (h + 1) * LANES)
        mq_ref[:, sl] = _head_norm_rope(mla_q[h], ss_mla_q[h], mqg_ref[...], *mla_rope,
                                        MLA_QK ** -0.5 * LOG2E).astype(BF16)
        mk_ref[:, sl] = _head_norm_rope(mla_k[h], ss_mla_k[h], mkg_ref[...], *mla_rope,
                                        1.0).astype(BF16)
        ones_lane = HALF if h % 2 == 0 else 0
        mv_ref[:, sl] = jnp.where(lane == ones_lane, 1.0, mla_v[h]).astype(BF16)


def _in_proj(x, tabs, p, seq, tm):
    n, d = x.shape
    tiles_per_seq = seq // tm
    halo_per_tile = tm // CONV_HALO
    n_halo = n // CONV_HALO
    row = lambda w: pl.BlockSpec((tm, w), lambda i: (i, 0))
    in_specs = [
        row(d),
        pl.BlockSpec((CONV_HALO, d), lambda i: (jnp.maximum(i * halo_per_tile - 1, 0), 0)),
        pl.BlockSpec((CONV_HALO, d),
                     lambda i: (jnp.minimum((i + 1) * halo_per_tile, n_halo - 1), 0)),
        pl.BlockSpec((6, tm, LANES), lambda i: (0, i % tiles_per_seq, 0)),
    ]
    names = ("w_in", "mix_norm", "conv_w", "conv_b", "dt_bias", "swa_q_gain", "swa_k_gain",
             "mla_qlat_norm", "w_uq", "mla_kv_norm", "w_ukv", "mla_q_gain", "mla_k_gain")
    resident = [_resident(p[k]) for k in names]
    args = [a for a, _ in resident]
    in_specs += [s for _, s in resident]
    widths = (SSD_INNER, SSD_CONV_DIM, LANES, SWA_WIDTH, LANES, LANES,
              MLA_HEADS * LANES, MLA_HEADS * LANES, MLA_HEADS * LANES)
    dts = (BF16, BF16, F32, BF16, BF16, BF16, BF16, BF16, BF16)
    return pl.pallas_call(
        functools.partial(_inproj_body, tiles_per_seq=tiles_per_seq),
        grid=(n // tm,),
        in_specs=in_specs,
        out_specs=[row(w) for w in widths],
        out_shape=[jax.ShapeDtypeStruct((n, w), t) for w, t in zip(widths, dts)],
        compiler_params=pltpu.CompilerParams(
            dimension_semantics=("parallel",), vmem_limit_bytes=48 * 1024 * 1024),
        name="in_proj",
    )(x, x, x, tabs, *args)


def _split_bf16(v, n):
    pieces = []
    for _ in range(n - 1):
        p = v.astype(BF16)
        pieces.append(p)
        v = v - p.astype(F32)
    pieces.append(v.astype(BF16))
    return pieces


def _pack_lanes(pieces, width):
    lane = _lane_iota(pieces[0].shape)
    out = jnp.zeros(pieces[0].shape, F32)
    for k, p in enumerate(pieces):
        pf = p.astype(F32)
        if k:
            pf = pltpu.roll(pf, k * width, 1)
        out = jnp.where((lane >= k * width) & (lane < (k + 1) * width), pf, out)
    return out.astype(BF16)


def _ssd_expand_matrices():
    nd = 2 * SSD_HEADS
    rows = np.arange(LANES)[:, None]
    head = np.arange(SSD_INNER)[None, :] // SSD_HEAD_DIM
    mats = []
    for direction in range(2):
        for first_piece in (0, 2):
            d = direction * SSD_HEADS + head
            hit = (rows == first_piece * nd + d) | (rows == (first_piece + 1) * nd + d)
            mats.append(hit)
    return jnp.asarray(np.stack(mats).astype(np.float32), BF16)


def _ssd_body(xbc_ref, dt_ref, z_ref, alog_ref, dskip_ref, nw_ref, expand_ref, o_ref,
              yacc_ref, sf_ref, sb_ref, cols_ref, *, n_chunks):
    pas = pl.program_id(1)
    j = pl.program_id(2)
    t = SSD_CHUNK
    cps = xbc_ref.shape[0] // t
    nh = SSD_HEADS

    gw = SSD_INNER // SSD_GROUPS
    hpg = nh // SSD_GROUPS
    nd = 2 * nh
    lane = _lane_iota((t, LANES))

    def b_of(rows, g):
        return xbc_ref[rows, SSD_INNER + g * SSD_STATE:SSD_INNER + (g + 1) * SSD_STATE]

    def c_of(rows, g):
        return xbc_ref[rows, SSD_INNER + (SSD_GROUPS + g) * SSD_STATE:
                       SSD_INNER + (SSD_GROUPS + g + 1) * SSD_STATE]

    def forward_early(k):
        rows = slice(k * t, (k + 1) * t)
        fwd = lane < nh
        dt = dt_ref[rows, :]
        avec = jnp.where(_lane_iota((1, LANES)) < nd, -jnp.exp(alog_ref[...]), 0.0)
        a = dt * avec
        row = lax.broadcasted_iota(jnp.int32, (t, t), 0)
        col = lax.broadcasted_iota(jnp.int32, (t, t), 1)
        tri = jnp.where(col <= row, 1.0, 0.0).astype(BF16)
        cum = sum(_dot(tri, part) for part in _split_bf16(a, 3))
        excl = cum - a
        tot = cum[t - 1:t, :]

        e_in = jnp.exp(jnp.where(fwd, cum, tot - excl))
        w_st = jnp.exp(jnp.where(fwd, tot - cum, excl)) * dt
        e1, e2 = _split_bf16(e_in, 2)
        w1, w2 = _split_bf16(w_st, 2)
        cols = _pack_lanes([e1, e2, w1, w2], nd)
        cols_ref[j * cps + k] = cols

        r = jnp.where(fwd, cum, -excl) * LOG2E
        cc = r - jnp.maximum(jnp.log(dt), LOG_FLOOR) * LOG2E
        ones = jnp.ones((t, LANES), F32)
        lhs = jnp.where(lane < 3 * nd, _pack_lanes(_split_bf16(r, 3), nd),
                        jnp.where(lane < 6 * nd, ones, 0.0).astype(BF16))
        cc_t = cc.T[0:nd, :]
        c1, c2, c3 = _split_bf16(cc_t, 3)
        rhs = jnp.concatenate(
            [jnp.ones((3 * nd, t), BF16), -c1, -c2, -c3, jnp.zeros((LANES - 6 * nd, t), BF16)],
            axis=0)
        sub = lax.broadcasted_iota(jnp.int32, (LANES, t), 0) % nd

        e_x = _dot(cols, expand_ref[0])
        w_x = _dot(cols, expand_ref[1])
        cbs = [_dot_nt(c_of(rows, g), b_of(rows, g)) for g in range(SSD_GROUPS)]
        ys = []
        for g in range(SSD_GROUPS):
            gs = slice(g * gw, (g + 1) * gw)
            s_in = sf_ref[g]
            xgf = xbc_ref[rows, gs].astype(F32)
            ys.append(dskip_ref[:, gs] * xgf + _dot(c_of(rows, g), s_in.astype(BF16)) * e_x[:, gs])
            xw = (xgf * w_x[:, gs]).astype(BF16)
            sf_ref[g] = s_in * e_x[t - 1:t, gs] + _dot_tn(b_of(rows, g), xw)
        exps = [_dot(lhs, jnp.where(sub == idx, rhs, jnp.zeros_like(rhs))) for idx in range(nd)]
        return cbs, ys, exps

    def forward_late(k, cbs, ys, exps):
        rows = slice(k * t, (k + 1) * t)
        th = t // 2
        quad = lambda v, qi, qj: v[qi * th:(qi + 1) * th, qj * th:(qj + 1) * th]
        qrow = lax.broadcasted_iota(jnp.int32, (th, th), 0)
        qcol = lax.broadcasted_iota(jnp.int32, (th, th), 1)
        on_low = qrow >= qcol
        on_up = qcol >= qrow

        def mixing(h, cb):
            df, db = exps[h], exps[nh + h]
            diag = [quad(cb, q, q) * (jnp.exp2(jnp.where(on_low, quad(df, q, q), NEG_INF))
                                      + jnp.exp2(jnp.where(on_up, quad(db, q, q), NEG_INF)))
                    for q in range(2)]
            upper = quad(cb, 0, 1) * jnp.exp2(quad(db, 0, 1))
            lower = quad(cb, 1, 0) * jnp.exp2(quad(df, 1, 0))
            return jnp.concatenate([jnp.concatenate([diag[0], upper], axis=1),
                                    jnp.concatenate([lower, diag[1]], axis=1)], axis=0)

        glane = _lane_iota((t, gw))
        for g in range(SSD_GROUPS):
            gs = slice(g * gw, (g + 1) * gw)
            xg = xbc_ref[rows, gs]
            y = ys[g]
            for hh in range(hpg):
                m = mixing(g * hpg + hh, cbs[g]).astype(BF16)
                mine = (glane >= hh * SSD_HEAD_DIM) & (glane < (hh + 1) * SSD_HEAD_DIM)
                y = y + _dot(m, jnp.where(mine, xg, jnp.zeros_like(xg)))
            yacc_ref[j * cps + k, :, gs] = y

    @pl.when(pas == 0)
    def _forward():
        @pl.when(j == 0)
        def _():
            sf_ref[...] = jnp.zeros_like(sf_ref)

        early = [forward_early(k) for k in range(cps)]
        for k in range(cps):
            forward_late(k, *early[k])

    @pl.when(pas == 1)
    def _backward():
        @pl.when(j == 0)
        def _():
            sb_ref[...] = jnp.zeros_like(sb_ref)

        for k in reversed(range(cps)):
            rows = slice(k * t, (k + 1) * t)
            c = n_chunks - 1 - j * cps - (cps - 1 - k)
            cols = cols_ref[c]
            e_x = _dot(cols, expand_ref[2])
            w_x = _dot(cols, expand_ref[3])
            gate = _silu(z_ref[rows, :].astype(F32))
            for g in range(SSD_GROUPS):
                gs = slice(g * gw, (g + 1) * gw)
                s_in = sb_ref[g]
                y = yacc_ref[c, :, gs] + _dot(c_of(rows, g), s_in.astype(BF16)) * e_x[:, gs]
                o_ref[rows, gs] = _rms(y * gate[:, gs], nw_ref[:, gs]).astype(BF16)
                xw = (xbc_ref[rows, gs].astype(F32) * w_x[:, gs]).astype(BF16)
                sb_ref[g] = s_in * e_x[0:1, gs] + _dot_tn(b_of(rows, g), xw)


def _ssd(xbc, dt, z, p, batch, seq, cps):
    t = SSD_CHUNK
    nc = seq // t
    ns = nc // cps

    def chunk_idx(b, pas, j):
        return b * ns + j + pas * (ns - 1 - 2 * j)

    def out_idx(b, pas, j):
        return b * ns + (ns - 1) - pas * j

    blk = lambda w, f: pl.BlockSpec((cps * t, w), lambda b, pas, j: (f(b, pas, j), 0))
    resident = [_resident(a) for a in (p["ssd_a_log"], p["ssd_d"], p["ssd_norm"],
                                       _ssd_expand_matrices())]
    state = pltpu.VMEM((SSD_GROUPS, SSD_STATE, SSD_INNER // SSD_GROUPS), F32)
    return pl.pallas_call(
        functools.partial(_ssd_body, n_chunks=nc),
        grid=(batch, 2, ns),
        in_specs=[blk(SSD_CONV_DIM, chunk_idx), blk(LANES, chunk_idx), blk(SSD_INNER, chunk_idx)]
                 + [s for _, s in resident],
        out_specs=blk(SSD_INNER, out_idx),
        out_shape=jax.ShapeDtypeStruct((batch * seq, SSD_INNER), BF16),
        scratch_shapes=[pltpu.VMEM((nc, t, SSD_INNER), F32), state, state,
                        pltpu.VMEM((nc, t, LANES), BF16)],
        compiler_params=pltpu.CompilerParams(
            dimension_semantics=("arbitrary", "arbitrary", "arbitrary"),
            vmem_limit_bytes=40 * 1024 * 1024),
        name="ssd",
    )(xbc, dt, z, *[a for a, _ in resident])


def _swa_body(sink_ref, q_ref, kp_ref, kc_ref, kn_ref, vp_ref, vc_ref, vn_ref, nw_ref, o_ref,
              *, layer):
    assert SWA_WINDOW == SWA_BLOCK
    j = pl.program_id(1)
    blk = SWA_BLOCK
    n_blk = q_ref.shape[0] // blk
    kb = jnp.concatenate([kp_ref[...], kc_ref[...], kn_ref[...]], axis=0)
    vb = jnp.concatenate([vp_ref[...], vc_ref[...], vn_ref[...]], axis=0)
    qi = lax.broadcasted_iota(jnp.int32, (blk, blk), 0)
    kj = lax.broadcasted_iota(jnp.int32, (blk, blk), 1)
    lo = _lane_iota((blk, LANES)) < HALF
    lo_v = _lane_iota(vb.shape) < HALF
    zero_q = jnp.zeros((blk, LANES), BF16)
    lane_v = _lane_iota(vb.shape)
    unit = lambda at: jnp.where(lane_v == at, 1.0, 0.0).astype(BF16)
    v_half = (jnp.where(lo_v, vb, unit(HALF)), jnp.where(lo_v, unit(0), vb))
    first = jnp.where(j == 0, blk, 0)
    last = jnp.where(j == pl.num_programs(1) - 1, blk, 0)
    items = [(t, g, half) for t in range(n_blk) for g in range(SWA_WIDTH // LANES)
             for half in range(2)]
    logits = {}
    for t, g, half in items:
        qg = q_ref[t * blk:(t + 1) * blk, g * LANES:(g + 1) * LANES]
        qm = jnp.where(lo, qg, zero_q) if half == 0 else jnp.where(lo, zero_q, qg)
        logits[t, g, half] = _dot_nt(qm, kb[t * blk:(t + 3) * blk])
    scaled = {}
    for t, g, half in items:
        keep_prev = kj >= (qi + first if t == 0 else qi)
        keep_next = kj <= (qi - last if t == n_blk - 1 else qi)
        s = logits[t, g, half]
        s_prev = jnp.where(keep_prev, s[:, :blk], NEG_INF)
        s_own = s[:, blk:2 * blk]
        s_next = jnp.where(keep_next, s[:, 2 * blk:], NEG_INF)
        sk = sink_ref[layer, g + 2 * half] * LOG2E
        m = jnp.max(jnp.maximum(jnp.maximum(s_prev, s_own), s_next), axis=-1, keepdims=True)
        m = jnp.maximum(m, sk)
        pexp = jnp.concatenate([jnp.exp2(v - m) for v in (s_prev, s_own, s_next)],
                               axis=-1).astype(BF16)
        acc = _dot(pexp, v_half[half][t * blk:(t + 3) * blk])
        ones_lane = HALF if half == 0 else 0
        den = acc[:, ones_lane:ones_lane + 1] + jnp.exp2(sk - m)
        scaled[t, g, half] = acc / den
    for t in range(n_blk):
        y = jnp.concatenate([jnp.where(lo, scaled[t, g, 0], scaled[t, g, 1])
                             for g in range(SWA_WIDTH // LANES)], axis=-1)
        o_ref[t * blk:(t + 1) * blk, :] = _rms(y, nw_ref[...]).astype(BF16)


def _swa(q, k, v, sink, nw, batch, seq, n_blk):
    blk = SWA_BLOCK
    nb = seq // blk
    ns = nb // n_blk
    cur = lambda b, j: (b * ns + j, 0)
    prv = lambda b, j: (b * nb + jnp.maximum(j * n_blk - 1, 0), 0)
    nxt = lambda b, j: (b * nb + jnp.minimum((j + 1) * n_blk, nb - 1), 0)
    edge = lambda f: pl.BlockSpec((blk, LANES), f)
    main = pl.BlockSpec((n_blk * blk, LANES), cur)
    sinks, layer = sink
    nw_arr, nw_spec = _resident(nw)
    return pl.pallas_call(
        functools.partial(_swa_body, layer=layer),
        grid=(batch, ns),
        in_specs=[pl.BlockSpec(memory_space=pltpu.SMEM),
                  pl.BlockSpec((n_blk * blk, SWA_WIDTH), cur),
                  edge(prv), main, edge(nxt), edge(prv), main, edge(nxt), nw_spec],
        out_specs=pl.BlockSpec((n_blk * blk, SWA_WIDTH), cur),
        out_shape=jax.ShapeDtypeStruct((batch * seq, SWA_WIDTH), BF16),
        compiler_params=pltpu.CompilerParams(dimension_semantics=("parallel", "parallel")),
        name="swa",
    )(sinks, q, k, k, k, v, v, v, nw_arr)


def _mla_body(q_ref, qall_ref, k_ref, v_ref, nw_ref, o_ref, knorm_ref, flag_ref, *, key_chunk):
    i = pl.program_id(1)
    tq = q_ref.shape[0]
    seq = k_ref.shape[0]
    heads = [slice(h * LANES, (h + 1) * LANES) for h in range(MLA_HEADS)]
    chunks = [slice(c * key_chunk, (c + 1) * key_chunk) for c in range(seq // key_chunk)]
    lo = _lane_iota((tq, LANES)) < HALF

    def max_norm(ref, sl):
        vf = ref[:, sl].astype(F32)
        ss = jnp.max(jnp.sum(vf * vf, axis=-1, keepdims=True), axis=0, keepdims=True)
        return jnp.sqrt(ss)

    @pl.when(i == 0)
    def _sequence_norms():
        worst = jnp.zeros((1, 1), F32)
        for h, sl in enumerate(heads):
            kn = max_norm(k_ref, sl)
            knorm_ref[h:h + 1, :] = jnp.broadcast_to(kn, (1, LANES))
            worst = jnp.maximum(worst, kn * max_norm(qall_ref, sl))
        ok = jnp.max(worst) * MLA_BOUND_SLACK <= MLA_BOUND_LIMIT
        flag_ref[0] = jnp.where(ok, 1, 0)

    use_bound = flag_ref[0] == 1

    def denom(acc, h):
        return acc[:, HALF:HALF + 1] if h % 2 == 0 else acc[:, 0:1]

    def finish(accs, dens):
        outs = [jnp.where(lo, accs[2 * p] / dens[2 * p], accs[2 * p + 1] / dens[2 * p + 1])
                for p in range(MLA_HEADS // 2)]
        y = jnp.concatenate(outs, axis=-1)
        o_ref[...] = _rms(y, nw_ref[...]).astype(BF16)

    @pl.when(use_bound)
    def _bounded():
        accs = []
        for h, sl in enumerate(heads):
            qh = q_ref[:, sl]
            qf = qh.astype(F32)
            bound = (jnp.sqrt(jnp.sum(qf * qf, axis=-1, keepdims=True))
                     * knorm_ref[h:h + 1, 0:1] * MLA_BOUND_SLACK)
            acc = None
            for ks in chunks:
                pexp = jnp.exp2(_dot_nt(qh, k_ref[ks, sl]) - bound).astype(BF16)
                part = _dot(pexp, v_ref[ks, sl])
                acc = part if acc is None else acc + part
            accs.append(acc)
        finish(accs, [denom(a, h) for h, a in enumerate(accs)])

    @pl.when(jnp.logical_not(use_bound))
    def _online():
        accs, dens = [], []
        for h, sl in enumerate(heads):
            qh = q_ref[:, sl]
            m = jnp.full((tq, 1), NEG_INF, F32)
            acc = jnp.zeros((tq, LANES), F32)
            for ks in chunks:
                s = _dot_nt(qh, k_ref[ks, sl])
                m_new = jnp.maximum(m, jnp.max(s, axis=-1, keepdims=True))
                pexp = jnp.exp2(s - m_new).astype(BF16)
                acc = jnp.exp2(m - m_new) * acc + _dot(pexp, v_ref[ks, sl])
                m = m_new
            accs.append(acc)
            dens.append(denom(acc, h))
        finish(accs, dens)


def _mla(q, k, v, nw, batch, seq, tq, key_chunk):
    nq = seq // tq
    w = MLA_HEADS * LANES
    full = pl.BlockSpec((seq, w), lambda b, i: (b, 0))
    nw, nw_spec = _resident(nw)
    return pl.pallas_call(
        functools.partial(_mla_body, key_chunk=key_chunk),
        grid=(batch, nq),
        in_specs=[pl.BlockSpec((tq, w), lambda b, i: (b * nq + i, 0)), full, full, full, nw_spec],
        out_specs=pl.BlockSpec((tq, MLA_WIDTH), lambda b, i: (b * nq + i, 0)),
        out_shape=jax.ShapeDtypeStruct((batch * seq, MLA_WIDTH), BF16),
        scratch_shapes=[pltpu.VMEM((8, LANES), F32), pltpu.SMEM((1,), jnp.int32)],
        compiler_params=pltpu.CompilerParams(
            dimension_semantics=("arbitrary", "arbitrary"),
            vmem_limit_bytes=56 * 1024 * 1024),
        name="mla",
    )(q, q, k, v, nw)


def _rope_tables(seq):
    def angles(dim):
        inv = 1.0 / np.power(ROPE_THETA, np.arange(0, dim, 2, dtype=np.float64) / dim)
        return np.arange(seq, dtype=np.float64)[:, None] * inv[None, :]

    a64 = angles(SWA_HEAD_DIM)
    c, s = np.cos(a64), np.sin(a64)
    zero = np.zeros_like(c)
    c64 = np.concatenate([c, c, c, c], axis=-1)
    sa64 = np.concatenate([-s, zero, -s, zero], axis=-1)
    sb64 = np.concatenate([zero, s, zero, s], axis=-1)
    a32 = angles(MLA_ROPE)
    c, s = np.cos(a32), np.sin(a32)
    zero = np.zeros_like(c)
    ones = np.ones((seq, MLA_NOPE))
    pad = np.zeros((seq, LANES - MLA_QK))
    zn = np.zeros((seq, MLA_NOPE))
    cm = np.concatenate([ones, c, c, pad], axis=-1)
    sam = np.concatenate([zn, -s, zero, pad], axis=-1)
    sbm = np.concatenate([zn, zero, s, pad], axis=-1)
    return jnp.asarray(np.stack([c64, sa64, sb64, cm, sam, sbm]).astype(np.float32))


def _swa_head_perm(t, axis):
    parts = jnp.split(t, SWA_HEADS, axis=axis)
    return jnp.concatenate([parts[0], parts[2], parts[1], parts[3]], axis=axis)


def _prep_params(ffn1_norm, mix_norm, w_in, ssd_conv_w, ssd_conv_b, ssd_dt_bias, ssd_a_log,
                 ssd_d, ssd_norm, swa_q_norm, swa_k_norm, swa_sink, swa_out_norm,
                 mla_q_lat_norm, mla_w_uq, mla_kv_norm, mla_w_ukv, mla_q_norm, mla_k_norm,
                 mla_out_norm, w_out, ffn2_norm):
    nl, d = w_in.shape[:2]
    row = lambda v: v.reshape(nl, 1, -1).astype(F32)
    pad_lanes = lambda v, n: jnp.pad(v, [(0, 0)] * (v.ndim - 1) + [(0, n - v.shape[-1])])

    wi = w_in.astype(BF16)
    o = 0
    cols = {}
    for name, size in (("z", SSD_INNER), ("xbc", SSD_CONV_DIM), ("dt", 2 * SSD_HEADS),
                       ("swq", SWA_WIDTH), ("swk", LANES), ("swv", LANES),
                       ("mlq", MLA_Q_RANK), ("ckv", MLA_KV_RANK), ("kr", MLA_ROPE)):
        cols[name] = wi[:, :, o:o + size]
        o += size
    zc = lambda n: jnp.zeros((nl, d, n), BF16)
    misc = jnp.concatenate([cols["dt"], zc(MLA_NOPE - 2 * SSD_HEADS), cols["kr"],
                            zc(LANES - MLA_QK)], axis=-1)
    w_in_p = jnp.concatenate([cols["z"], cols["xbc"], _swa_head_perm(cols["swq"], 2),
                              cols["swk"], cols["swv"], cols["mlq"], cols["ckv"], misc], axis=-1)

    wuq = mla_w_uq.reshape(nl, MLA_Q_RANK, MLA_HEADS, MLA_QK)
    wuq = pad_lanes(wuq, LANES).reshape(nl, MLA_Q_RANK, MLA_HEADS * LANES).astype(BF16)
    wukv = mla_w_ukv.reshape(nl, MLA_KV_RANK, MLA_HEADS, MLA_NOPE + MLA_V)
    knope = pad_lanes(wukv[..., :MLA_NOPE], LANES)
    vv = wukv[..., MLA_NOPE:]
    zv = jnp.zeros_like(vv)
    even = (np.arange(MLA_HEADS) % 2 == 0)[None, None, :, None]
    vpad = jnp.concatenate([jnp.where(even, vv, zv), jnp.where(even, zv, vv)], axis=-1)
    wukv_p = jnp.concatenate([knope.reshape(nl, MLA_KV_RANK, -1),
                              vpad.reshape(nl, MLA_KV_RANK, -1)], axis=-1).astype(BF16)

    a, b = SSD_INNER, SSD_INNER + SWA_WIDTH
    w_out_p = jnp.concatenate([w_out[:, :a], _swa_head_perm(w_out[:, a:b], 1), w_out[:, b:]],
                              axis=1).astype(BF16)

    two = lambda v: jnp.concatenate([v, v], axis=-1)
    return {
        "ffn1_norm": row(ffn1_norm),
        "ffn2_norm": row(ffn2_norm),
        "w_in": w_in_p,
        "mix_norm": row(mix_norm),
        "conv_w": ssd_conv_w.astype(F32),
        "conv_b": row(ssd_conv_b),
        "dt_bias": pad_lanes(row(ssd_dt_bias), LANES),
        "swa_q_gain": row(two(swa_q_norm)),
        "swa_k_gain": row(two(swa_k_norm)),
        "mla_qlat_norm": row(mla_q_lat_norm),
        "w_uq": wuq,
        "mla_kv_norm": row(mla_kv_norm),
        "w_ukv": wukv_p,
        "mla_q_gain": pad_lanes(row(mla_q_norm), LANES),
        "mla_k_gain": pad_lanes(row(mla_k_norm), LANES),
        "ssd_a_log": pad_lanes(row(ssd_a_log), LANES),
        "ssd_d": row(jnp.repeat(ssd_d, SSD_HEAD_DIM, axis=-1)),
        "ssd_norm": row(ssd_norm),
        "swa_sink": swa_sink.astype(F32),
        "swa_out_norm": row(_swa_head_perm(swa_out_norm, 1)),
        "mla_out_norm": row(mla_out_norm),
        "w_out": w_out_p,
    }


def _tiles(seq):
    pick = lambda pref: math.gcd(seq, pref)
    return {"ffn": pick(1024), "proj": pick(512), "mla_q": pick(256), "mla_k": pick(1024),
            "swa_blocks": pick(16 * SWA_BLOCK) // SWA_BLOCK,
            "ssd_chunks": pick(4 * SSD_CHUNK) // SSD_CHUNK}


def kernel(x, ffn1_norm, ffn1_gate, ffn1_up, ffn1_down, mix_norm, w_in, ssd_conv_w, ssd_conv_b, ssd_dt_bias, ssd_a_log, ssd_d, ssd_norm, swa_q_norm, swa_k_norm, swa_sink, swa_out_norm, mla_q_lat_norm, mla_w_uq, mla_kv_norm, mla_w_ukv, mla_q_norm, mla_k_norm, mla_out_norm, w_out, ffn2_norm, ffn2_gate, ffn2_up, ffn2_down):
    batch, seq, d = x.shape
    assert seq % SSD_CHUNK == 0 and seq % SWA_BLOCK == 0
    assert ffn1_gate.shape[-1] % FFN_CHUNK == 0
    stacked = _prep_params(ffn1_norm, mix_norm, w_in, ssd_conv_w, ssd_conv_b, ssd_dt_bias,
                           ssd_a_log, ssd_d, ssd_norm, swa_q_norm, swa_k_norm, swa_sink,
                           swa_out_norm, mla_q_lat_norm, mla_w_uq, mla_kv_norm, mla_w_ukv,
                           mla_q_norm, mla_k_norm, mla_out_norm, w_out, ffn2_norm)
    tl = _tiles(seq)
    tabs = _rope_tables(seq)
    xf = x.reshape(batch * seq, d).astype(F32)
    for l in range(w_in.shape[0]):
        p = {k: (v, l) for k, v in stacked.items()}
        ffn1_w = _cast_ffn_weights(l, ffn1_gate, ffn1_up, ffn1_down)
        ffn2_w = _cast_ffn_weights(l, ffn2_gate, ffn2_up, ffn2_down)
        xf = _ffn(xf, p["ffn1_norm"], *ffn1_w, tm=tl["ffn"])
        z, xbc, dt, swq, swk, swv, mq, mk, mv = _in_proj(xf, tabs, p, seq, tl["proj"])
        y_ssd = _ssd(xbc, dt, z, p, batch, seq, tl["ssd_chunks"])
        y_swa = _swa(swq, swk, swv, p["swa_sink"], p["swa_out_norm"], batch, seq,
                     tl["swa_blocks"])
        y_mla = _mla(mq, mk, mv, p["mla_out_norm"], batch, seq, tl["mla_q"], tl["mla_k"])
        xf = _ffn(xf, p["ffn2_norm"], *ffn2_w, tm=tl["ffn"],
                  mixer=(y_ssd, y_swa, y_mla, p["w_out"]))
    return xf.reshape(batch, seq, d).astype(x.dtype)
```

```python
import functools
import math

import jax
import jax.numpy as jnp
import numpy as np
from jax import lax
from jax.experimental import pallas as pl
from jax.experimental.pallas import tpu as pltpu

F32 = jnp.float32
BF16 = jnp.bfloat16

LANES = 128
HALF = LANES // 2

SSD_HEADS = 8
SSD_HEAD_DIM = 64
SSD_INNER = SSD_HEADS * SSD_HEAD_DIM
SSD_GROUPS = 2
SSD_STATE = 128
SSD_CONV = 5
SSD_CHUNK = 256
SSD_CONV_DIM = SSD_INNER + 2 * SSD_GROUPS * SSD_STATE
SWA_HEADS = 4
SWA_KV_HEADS = 2
SWA_HEAD_DIM = 64
SWA_WINDOW = 128
SWA_BLOCK = 128
SWA_WIDTH = SWA_HEADS * SWA_HEAD_DIM
MLA_HEADS = 4
MLA_Q_RANK = 256
MLA_KV_RANK = 128
MLA_NOPE = 64
MLA_ROPE = 32
MLA_QK = MLA_NOPE + MLA_ROPE
MLA_V = 64
MLA_WIDTH = MLA_HEADS * MLA_V
ROPE_THETA = 10000.0
EPS = 1e-6

C_Z = 0
C_XBC = C_Z + SSD_INNER
C_SWQ = C_XBC + SSD_CONV_DIM
C_SWK = C_SWQ + SWA_WIDTH
C_SWV = C_SWK + LANES
C_MLQ = C_SWV + LANES
C_CKV = C_MLQ + MLA_Q_RANK
C_MISC = C_CKV + MLA_KV_RANK
C_END = C_MISC + LANES
CONV_HALO = 16
FFN_CHUNK = 256
FFN_RESIDUAL_SCALE = 0.5
NEG_INF = float("-inf")
LOG_FLOOR = -1e30
LOG2E = math.log2(math.e)
MLA_BOUND_LIMIT = 48.0
MLA_BOUND_SLACK = 1.02


def _rms(x, w):
    ms = jnp.mean(x * x, axis=-1, keepdims=True)
    return x * lax.rsqrt(ms + EPS) * w


def _silu(x):
    return x / (1.0 + jnp.exp(-x))


def _dot(a, b):
    return jnp.dot(a, b, preferred_element_type=F32)


def _dot_nt(a, b):
    return lax.dot_general(a, b, (((1,), (1,)), ((), ())), preferred_element_type=F32)


def _dot_tn(a, b):
    return lax.dot_general(a, b, (((0,), (0,)), ((), ())), preferred_element_type=F32)


def _lane_iota(shape):
    return lax.broadcasted_iota(jnp.int32, shape, len(shape) - 1)


def _resident(param, single_buffer=False):
    mode = {"pipeline_mode": pl.Buffered(1)} if single_buffer else {}
    if isinstance(param, tuple):
        arr, layer = param
        rest = (0,) * (arr.ndim - 1)
        return arr, pl.BlockSpec((None,) + arr.shape[1:], lambda *_: (layer,) + rest, **mode)
    zeros = (0,) * param.ndim
    return param, pl.BlockSpec(param.shape, lambda *_: zeros, **mode)


def _cast_body(gate_ref, up_ref, down_ref, gate_out, up_out, down_out):
    gate_out[...] = gate_ref[...].astype(BF16)
    up_out[...] = up_ref[...].astype(BF16)
    down_out[...] = (down_ref[...] * FFN_RESIDUAL_SCALE).astype(BF16)


def _cast_ffn_weights(l, gate, up, down, steps=4):
    _, d, f = gate.shape
    col = pl.BlockSpec((None, d // steps, f), lambda i: (l, i, 0))
    rowb = pl.BlockSpec((None, f // steps, d), lambda i: (l, i, 0))
    out_col = pl.BlockSpec((d // steps, f), lambda i: (i, 0))
    out_row = pl.BlockSpec((f // steps, d), lambda i: (i, 0))
    return pl.pallas_call(
        _cast_body,
        grid=(steps,),
        in_specs=[col, col, rowb],
        out_specs=[out_col, out_col, out_row],
        out_shape=[jax.ShapeDtypeStruct((d, f), BF16), jax.ShapeDtypeStruct((d, f), BF16),
                   jax.ShapeDtypeStruct((f, d), BF16)],
        compiler_params=pltpu.CompilerParams(
            dimension_semantics=("parallel",), vmem_limit_bytes=40 * 1024 * 1024),
        name="cast_ffn_weights",
    )(gate, up, down)


def _ffn_body(*refs, fuse_out_proj):
    if fuse_out_proj:
        x_ref, ys_ref, yw_ref, ym_ref, wo_ref = refs[:5]
        a, b = SSD_INNER, SSD_INNER + SWA_WIDTH
        x = (x_ref[...] + _dot(ys_ref[...], wo_ref[0:a, :]) + _dot(yw_ref[...], wo_ref[a:b, :])
             + _dot(ym_ref[...], wo_ref[b:, :]))
        refs = refs[5:]
    else:
        x_ref = refs[0]
        x = x_ref[...]
        refs = refs[1:]
    nw_ref, wg_ref, wu_ref, wd_ref, o_ref, xn_ref = refs
    o_ref[...] = x
    xn_ref[...] = _rms(x, nw_ref[...]).astype(BF16)
    for c in range(wg_ref.shape[1] // FFN_CHUNK):
        cols = slice(c * FFN_CHUNK, (c + 1) * FFN_CHUNK)
        xn = xn_ref[...]
        g = _dot(xn, wg_ref[:, cols])
        u = _dot(xn, wu_ref[:, cols])
        h = (_silu(g) * u).astype(BF16)
        o_ref[...] += _dot(h, wd_ref[cols, :])


def _ffn(x, nw, wg, wu, wd, tm, mixer=None):
    n, d = x.shape
    row = lambda w: pl.BlockSpec((tm, w), lambda i: (i, 0))
    args, specs = [x], [row(d)]
    resident = [_resident(nw)] + [_resident(w, single_buffer=True) for w in (wg, wu, wd)]
    if mixer is not None:
        args += list(mixer[:3])
        specs += [row(a.shape[1]) for a in mixer[:3]]
        resident.insert(0, _resident(mixer[3]))
    args += [a for a, _ in resident]
    specs += [s for _, s in resident]
    return pl.pallas_call(
        functools.partial(_ffn_body, fuse_out_proj=mixer is not None),
        grid=(n // tm,),
        in_specs=specs,
        out_specs=row(d),
        out_shape=jax.ShapeDtypeStruct((n, d), F32),
        scratch_shapes=[pltpu.VMEM((tm, d), BF16)],
        compiler_params=pltpu.CompilerParams(
            dimension_semantics=("parallel",), vmem_limit_bytes=56 * 1024 * 1024),
        name="ffn",
    )(*args)


def _permute_w_in_body(src_ref, dst_ref, tail_ref):
    rows, n_src = src_ref.shape
    lane = _lane_iota((rows, LANES))
    late = 2 * SSD_HEADS
    n_full = n_src // LANES
    first_tail = C_SWQ // LANES
    for w in range(first_tail):
        dst_ref[:, w * LANES:(w + 1) * LANES] = src_ref[:, w * LANES:(w + 1) * LANES].astype(BF16)
    tail_ref[...] = jnp.zeros_like(tail_ref)
    tail_ref[:, 0:n_src - n_full * LANES] = src_ref[:, n_full * LANES:n_src]
    tiles = [src_ref[:, w * LANES:(w + 1) * LANES] for w in range(first_tail, n_full)]
    tiles.append(tail_ref[...])
    rolled = [pltpu.roll(t, LANES - late, 1) for t in tiles]
    merged = [jnp.where(lane < LANES - late, rolled[t], rolled[t + 1])
              for t in range(len(tiles) - 1)]
    lo = lane < HALF
    q02 = jnp.where(lo, merged[0], pltpu.roll(merged[1], HALF, 1))
    q13 = jnp.where(lo, pltpu.roll(merged[0], HALF, 1), merged[1])
    kr_lane = (n_src - MLA_ROPE) - n_full * LANES
    k_rope = pltpu.roll(tiles[-1], MLA_NOPE - kr_lane, 1)
    misc = jnp.where(lane < late, tiles[0],
                     jnp.where((lane >= MLA_NOPE) & (lane < MLA_QK), k_rope, 0.0))
    groups = [q02, q13] + merged[2:] + [misc]
    for g, v in enumerate(groups):
        c0 = C_SWQ + g * LANES
        dst_ref[:, c0:c0 + LANES] = v.astype(BF16)


def _permute_w_in(w_in, row_block=256):
    nl, d, n_src = w_in.shape
    assert C_SWQ + (n_src // LANES - C_SWQ // LANES + 1) * LANES == C_END
    return pl.pallas_call(
        _permute_w_in_body,
        grid=(nl, d // row_block),
        in_specs=[pl.BlockSpec((None, row_block, n_src), lambda l, i: (l, i, 0))],
        out_specs=pl.BlockSpec((None, row_block, C_END), lambda l, i: (l, i, 0)),
        out_shape=jax.ShapeDtypeStruct((nl, d, C_END), BF16),
        scratch_shapes=[pltpu.VMEM((row_block, LANES), F32)],
        compiler_params=pltpu.CompilerParams(dimension_semantics=("parallel", "parallel")),
        name="permute_w_in",
    )(w_in)


def _rope(y, c, sa, sb, shift):
    return (y * c + pltpu.roll(y, LANES - shift, 1) * sa + pltpu.roll(y, shift, 1) * sb)


def _head_sum_matrix(width):
    r = lax.broadcasted_iota(jnp.int32, (LANES, LANES), 0)
    c = lax.broadcasted_iota(jnp.int32, (LANES, LANES), 1)
    same = (r < width) if width > HALF else ((r < HALF) == (c < HALF))
    return jnp.where(same, 1.0, 0.0).astype(BF16)


def _head_norm_rope(x, sumsq, gain, c, sa, sb, width, rope_dim, post_scale):
    y = x * lax.rsqrt(sumsq / float(width) + EPS) * gain
    out = _rope(y, c, sa, sb, rope_dim // 2)
    return out if post_scale == 1.0 else out * post_scale


def _inproj_body(x_ref, xp_ref, xnx_ref, tab_ref, w_ref, mixw_ref, convw_ref, convb_ref,
                 dtb_ref, swqg_ref, swkg_ref, qlw_ref, wuq_ref, kvw_ref, wukv_ref,
                 mqg_ref, mkg_ref,
                 z_ref, xbc_ref, dt_ref, swq_ref, swk_ref, swv_ref, mq_ref, mk_ref, mv_ref,
                 *, tiles_per_seq):
    i = pl.program_id(0)
    tm = x_ref.shape[0]
    pos_tile = i % tiles_per_seq
    mixw = mixw_ref[...]
    hn = _rms(x_ref[...], mixw).astype(BF16)
    hp = _rms(xp_ref[...], mixw).astype(BF16)
    hx = _rms(xnx_ref[...], mixw).astype(BF16)
    groups = lambda v: [v[:, g * LANES:(g + 1) * LANES] for g in range(v.shape[1] // LANES)]

    z = _dot(hn, w_ref[:, C_Z:C_XBC])
    conv_w = 2 * LANES
    xbc_parts = []
    for c0 in range(C_XBC, C_SWQ, conv_w):
        wx = w_ref[:, c0:c0 + conv_w]
        xbc_parts.append((_dot(hp, wx), _dot(hn, wx), _dot(hx, wx)))
    ckv, misc = groups(_dot(hn, w_ref[:, C_CKV:C_END]))
    swa_q = groups(_dot(hn, w_ref[:, C_SWQ:C_SWK]))
    swa_k, swa_v = groups(_dot(hn, w_ref[:, C_SWK:C_MLQ]))
    ql = _dot(hn, w_ref[:, C_MLQ:C_CKV])

    q_up = groups(_dot(_rms(ql, qlw_ref[...]).astype(BF16), wuq_ref[...]))
    kv_up = groups(_dot(_rms(ckv, kvw_ref[...]).astype(BF16), wukv_ref[...]))
    lane = _lane_iota(misc.shape)
    kr = jnp.where((lane >= MLA_NOPE) & (lane < MLA_QK), misc, 0.0)
    mla_q = q_up
    mla_k = [k_nope + kr for k_nope in kv_up[:MLA_HEADS]]
    mla_v = kv_up[MLA_HEADS:]

    sum64, sum96 = _head_sum_matrix(SWA_HEAD_DIM), _head_sum_matrix(MLA_QK)
    sumsq = lambda xs, m: [_dot((v * v).astype(BF16), m) for v in xs]
    ss_swa_q, ss_swa_k = sumsq(swa_q, sum64), sumsq([swa_k], sum64)
    ss_mla_q, ss_mla_k = sumsq(mla_q, sum96), sumsq(mla_k, sum96)

    z_ref[...] = z.astype(BF16)

    keep_prev = (pos_tile > 0).astype(F32)
    keep_next = (pos_tile < tiles_per_seq - 1).astype(F32)
    for part, (xbc_prev, xbc_main, xbc_next) in enumerate(xbc_parts):
        cs = slice(part * conv_w, (part + 1) * conv_w)
        padded = jnp.concatenate([xbc_prev * keep_prev, xbc_main, xbc_next * keep_next], axis=0)
        n_pad = padded.shape[0]
        conv = jnp.broadcast_to(convb_ref[:, cs], (tm, conv_w))
        for k in range(SSD_CONV):
            shift = SSD_CONV // 2 - k
            tap = padded if shift == 0 else pltpu.roll(padded, shift % n_pad, 0)
            conv = conv + convw_ref[k:k + 1, cs] * tap[CONV_HALO:CONV_HALO + tm, :]
        xbc_ref[:, cs] = _silu(conv).astype(BF16)

    dtv = misc + dtb_ref[...]
    dt_ref[...] = jnp.maximum(dtv, 0.0) + jnp.log1p(jnp.exp(-jnp.abs(dtv)))

    swa_rope = (tab_ref[0], tab_ref[1], tab_ref[2], SWA_HEAD_DIM, SWA_HEAD_DIM)
    for g, (v, ss) in enumerate(zip(swa_q, ss_swa_q)):
        swq_ref[:, g * LANES:(g + 1) * LANES] = _head_norm_rope(
            v, ss, swqg_ref[...], *swa_rope, SWA_HEAD_DIM ** -0.5 * LOG2E).astype(BF16)
    swk_ref[...] = _head_norm_rope(swa_k, ss_swa_k[0], swkg_ref[...], *swa_rope, 1.0).astype(BF16)
    swv_ref[...] = swa_v.astype(BF16)

    mla_rope = (tab_ref[3], tab_ref[4], tab_ref[5], MLA_QK, MLA_ROPE)
    for h in range(MLA_HEADS):
        sl = slice(h * LANES, (h + 1) * LANES)
        mq_ref[:, sl] = _head_norm_rope(mla_q[h], ss_mla_q[h], mqg_ref[...], *mla_rope,
                                        MLA_QK ** -0.5 * LOG2E).astype(BF16)
        mk_ref[:, sl] = _head_norm_rope(mla_k[h], ss_mla_k[h], mkg_ref[...], *mla_rope,
                                        1.0).astype(BF16)
        ones_lane = HALF if h % 2 == 0 else 0
        mv_ref[:, sl] = jnp.where(lane == ones_lane, 1.0, mla_v[h]).astype(BF16)


def _in_proj(x, tabs, p, seq, tm):
    n, d = x.shape
    tiles_per_seq = seq // tm
    halo_per_tile = tm // CONV_HALO
    n_halo = n // CONV_HALO
    row = lambda w: pl.BlockSpec((tm, w), lambda i: (i, 0))
    in_specs = [
        row(d),
        pl.BlockSpec((CONV_HALO, d), lambda i: (jnp.maximum(i * halo_per_tile - 1, 0), 0)),
        pl.BlockSpec((CONV_HALO, d),
                     lambda i: (jnp.minimum((i + 1) * halo_per_tile, n_halo - 1), 0)),
        pl.BlockSpec((6, tm, LANES), lambda i: (0, i % tiles_per_seq, 0)),
    ]
    names = ("w_in", "mix_norm", "conv_w", "conv_b", "dt_bias", "swa_q_gain", "swa_k_gain",
             "mla_qlat_norm", "w_uq", "mla_kv_norm", "w_ukv", "mla_q_gain", "mla_k_gain")
    resident = [_resident(p[k]) for k in names]
    args = [a for a, _ in resident]
    in_specs += [s for _, s in resident]
    widths = (SSD_INNER, SSD_CONV_DIM, LANES, SWA_WIDTH, LANES, LANES,
              MLA_HEADS * LANES, MLA_HEADS * LANES, MLA_HEADS * LANES)
    dts = (BF16, BF16, F32, BF16, BF16, BF16, BF16, BF16, BF16)
    return pl.pallas_call(
        functools.partial(_inproj_body, tiles_per_seq=tiles_per_seq),
        grid=(n // tm,),
        in_specs=in_specs,
        out_specs=[row(w) for w in widths],
        out_shape=[jax.ShapeDtypeStruct((n, w), t) for w, t in zip(widths, dts)],
        compiler_params=pltpu.CompilerParams(
            dimension_semantics=("parallel",), vmem_limit_bytes=48 * 1024 * 1024),
        name="in_proj",
    )(x, x, x, tabs, *args)


def _split_bf16(v, n):
    pieces = []
    for _ in range(n - 1):
        p = v.astype(BF16)
        pieces.append(p)
        v = v - p.astype(F32)
    pieces.append(v.astype(BF16))
    return pieces


def _pack_lanes(pieces, width):
    lane = _lane_iota(pieces[0].shape)
    out = jnp.zeros(pieces[0].shape, F32)
    for k, p in enumerate(pieces):
        pf = p.astype(F32)
        if k:
            pf = pltpu.roll(pf, k * width, 1)
        out = jnp.where((lane >= k * width) & (lane < (k + 1) * width), pf, out)
    return out.astype(BF16)


def _ssd_expand_matrices():
    nd = 2 * SSD_HEADS
    rows = np.arange(LANES)[:, None]
    head = np.arange(SSD_INNER)[None, :] // SSD_HEAD_DIM
    mats = []
    for direction in range(2):
        for first_piece in (0, 2):
            d = direction * SSD_HEADS + head
            hit = (rows == first_piece * nd + d) | (rows == (first_piece + 1) * nd + d)
            mats.append(hit)
    return jnp.asarray(np.stack(mats).astype(np.float32), BF16)


def _ssd_body(xbc_ref, dt_ref, z_ref, alog_ref, dskip_ref, nw_ref, expand_ref, o_ref,
              yacc_ref, sf_ref, sb_ref, cols_ref, *, n_chunks):
    pas = pl.program_id(1)
    j = pl.program_id(2)
    t = SSD_CHUNK
    cps = xbc_ref.shape[0] // t
    nh = SSD_HEADS

    gw = SSD_INNER // SSD_GROUPS
    hpg = nh // SSD_GROUPS
    nd = 2 * nh
    lane = _lane_iota((t, LANES))

    def b_of(rows, g):
        return xbc_ref[rows, SSD_INNER + g * SSD_STATE:SSD_INNER + (g + 1) * SSD_STATE]

    def c_of(rows, g):
        return xbc_ref[rows, SSD_INNER + (SSD_GROUPS + g) * SSD_STATE:
                       SSD_INNER + (SSD_GROUPS + g + 1) * SSD_STATE]

    def forward_early(k):
        rows = slice(k * t, (k + 1) * t)
        fwd = lane < nh
        dt = dt_ref[rows, :]
        avec = jnp.where(_lane_iota((1, LANES)) < nd, -jnp.exp(alog_ref[...]), 0.0)
        a = dt * avec
        row = lax.broadcasted_iota(jnp.int32, (t, t), 0)
        col = lax.broadcasted_iota(jnp.int32, (t, t), 1)
        tri = jnp.where(col <= row, 1.0, 0.0).astype(BF16)
        cum = sum(_dot(tri, part) for part in _split_bf16(a, 3))
        excl = cum - a
        tot = cum[t - 1:t, :]

        e_in = jnp.exp(jnp.where(fwd, cum, tot - excl))
        w_st = jnp.exp(jnp.where(fwd, tot - cum, excl)) * dt
        e1, e2 = _split_bf16(e_in, 2)
        w1, w2 = _split_bf16(w_st, 2)
        cols = _pack_lanes([e1, e2, w1, w2], nd)
        cols_ref[j * cps + k] = cols

        r = jnp.where(fwd, cum, -excl) * LOG2E
        cc = r - jnp.maximum(jnp.log(dt), LOG_FLOOR) * LOG2E
        ones = jnp.ones((t, LANES), F32)
        lhs = jnp.where(lane < 3 * nd, _pack_lanes(_split_bf16(r, 3), nd),
                        jnp.where(lane < 6 * nd, ones, 0.0).astype(BF16))
        cc_t = cc.T[0:nd, :]
        c1, c2, c3 = _split_bf16(cc_t, 3)
        rhs = jnp.concatenate(
            [jnp.ones((3 * nd, t), BF16), -c1, -c2, -c3, jnp.zeros((LANES - 6 * nd, t), BF16)],
            axis=0)
        sub = lax.broadcasted_iota(jnp.int32, (LANES, t), 0) % nd

        e_x = _dot(cols, expand_ref[0])
        w_x = _dot(cols, expand_ref[1])
        cbs = [_dot_nt(c_of(rows, g), b_of(rows, g)) for g in range(SSD_GROUPS)]
        ys = []
        for g in range(SSD_GROUPS):
            gs = slice(g * gw, (g + 1) * gw)
            s_in = sf_ref[g]
            xgf = xbc_ref[rows, gs].astype(F32)
            ys.append(dskip_ref[:, gs] * xgf + _dot(c_of(rows, g), s_in.astype(BF16)) * e_x[:, gs])
            xw = (xgf * w_x[:, gs]).astype(BF16)
            sf_ref[g] = s_in * e_x[t - 1:t, gs] + _dot_tn(b_of(rows, g), xw)
        exps = [_dot(lhs, jnp.where(sub == idx, rhs, jnp.zeros_like(rhs))) for idx in range(nd)]
        return cbs, ys, exps

    def forward_late(k, cbs, ys, exps):
        rows = slice(k * t, (k + 1) * t)
        th = t // 2
        quad = lambda v, qi, qj: v[qi * th:(qi + 1) * th, qj * th:(qj + 1) * th]
        qrow = lax.broadcasted_iota(jnp.int32, (th, th), 0)
        qcol = lax.broadcasted_iota(jnp.int32, (th, th), 1)
        on_low = qrow >= qcol
        on_up = qcol >= qrow

        def mixing(h, cb):
            df, db = exps[h], exps[nh + h]
            diag = [quad(cb, q, q) * (jnp.exp2(jnp.where(on_low, quad(df, q, q), NEG_INF))
                                      + jnp.exp2(jnp.where(on_up, quad(db, q, q), NEG_INF)))
                    for q in range(2)]
            upper = quad(cb, 0, 1) * jnp.exp2(quad(db, 0, 1))
            lower = quad(cb, 1, 0) * jnp.exp2(quad(df, 1, 0))
            return jnp.concatenate([jnp.concatenate([diag[0], upper], axis=1),
                                    jnp.concatenate([lower, diag[1]], axis=1)], axis=0)

        glane = _lane_iota((t, gw))
        for g in range(SSD_GROUPS):
            gs = slice(g * gw, (g + 1) * gw)
            xg = xbc_ref[rows, gs]
            y = ys[g]
            for hh in range(hpg):
                m = mixing(g * hpg + hh, cbs[g]).astype(BF16)
                mine = (glane >= hh * SSD_HEAD_DIM) & (glane < (hh + 1) * SSD_HEAD_DIM)
                y = y + _dot(m, jnp.where(mine, xg, jnp.zeros_like(xg)))
            yacc_ref[j * cps + k, :, gs] = y

    @pl.when(pas == 0)
    def _forward():
        @pl.when(j == 0)
        def _():
            sf_ref[...] = jnp.zeros_like(sf_ref)

        early = [forward_early(k) for k in range(cps)]
        for k in range(cps):
            forward_late(k, *early[k])

    @pl.when(pas == 1)
    def _backward():
        @pl.when(j == 0)
        def _():
            sb_ref[...] = jnp.zeros_like(sb_ref)

        for k in reversed(range(cps)):
            rows = slice(k * t, (k + 1) * t)
            c = n_chunks - 1 - j * cps - (cps - 1 - k)
            cols = cols_ref[c]
            e_x = _dot(cols, expand_ref[2])
            w_x = _dot(cols, expand_ref[3])
            gate = _silu(z_ref[rows, :].astype(F32))
            for g in range(SSD_GROUPS):
                gs = slice(g * gw, (g + 1) * gw)
                s_in = sb_ref[g]
                y = yacc_ref[c, :, gs] + _dot(c_of(rows, g), s_in.astype(BF16)) * e_x[:, gs]
                o_ref[rows, gs] = _rms(y * gate[:, gs], nw_ref[:, gs]).astype(BF16)
                xw = (xbc_ref[rows, gs].astype(F32) * w_x[:, gs]).astype(BF16)
                sb_ref[g] = s_in * e_x[0:1, gs] + _dot_tn(b_of(rows, g), xw)


def _ssd(xbc, dt, z, p, batch, seq, cps):
    t = SSD_CHUNK
    nc = seq // t
    ns = nc // cps

    def chunk_idx(b, pas, j):
        return b * ns + j + pas * (ns - 1 - 2 * j)

    def out_idx(b, pas, j):
        return b * ns + (ns - 1) - pas * j

    blk = lambda w, f: pl.BlockSpec((cps * t, w), lambda b, pas, j: (f(b, pas, j), 0))
    resident = [_resident(a) for a in (p["ssd_a_log"], p["ssd_d"], p["ssd_norm"],
                                       _ssd_expand_matrices())]
    state = pltpu.VMEM((SSD_GROUPS, SSD_STATE, SSD_INNER // SSD_GROUPS), F32)
    return pl.pallas_call(
        functools.partial(_ssd_body, n_chunks=nc),
        grid=(batch, 2, ns),
        in_specs=[blk(SSD_CONV_DIM, chunk_idx), blk(LANES, chunk_idx), blk(SSD_INNER, chunk_idx)]
                 + [s for _, s in resident],
        out_specs=blk(SSD_INNER, out_idx),
        out_shape=jax.ShapeDtypeStruct((batch * seq, SSD_INNER), BF16),
        scratch_shapes=[pltpu.VMEM((nc, t, SSD_INNER), F32), state, state,
                        pltpu.VMEM((nc, t, LANES), BF16)],
        compiler_params=pltpu.CompilerParams(
            dimension_semantics=("arbitrary", "arbitrary", "arbitrary"),
            vmem_limit_bytes=40 * 1024 * 1024),
        name="ssd",
    )(xbc, dt, z, *[a for a, _ in resident])


def _swa_body(sink_ref, q_ref, kp_ref, kc_ref, kn_ref, vp_ref, vc_ref, vn_ref, nw_ref, o_ref,
              *, layer):
    assert SWA_WINDOW == SWA_BLOCK
    j = pl.program_id(1)
    blk = SWA_BLOCK
    n_blk = q_ref.shape[0] // blk
    kb = jnp.concatenate([kp_ref[...], kc_ref[...], kn_ref[...]], axis=0)
    vb = jnp.concatenate([vp_ref[...], vc_ref[...], vn_ref[...]], axis=0)
    qi = lax.broadcasted_iota(jnp.int32, (blk, blk), 0)
    kj = lax.broadcasted_iota(jnp.int32, (blk, blk), 1)
    lo = _lane_iota((blk, LANES)) < HALF
    lo_v = _lane_iota(vb.shape) < HALF
    zero_q = jnp.zeros((blk, LANES), BF16)
    lane_v = _lane_iota(vb.shape)
    unit = lambda at: jnp.where(lane_v == at, 1.0, 0.0).astype(BF16)
    v_half = (jnp.where(lo_v, vb, unit(HALF)), jnp.where(lo_v, unit(0), vb))
    first = jnp.where(j == 0, blk, 0)
    last = jnp.where(j == pl.num_programs(1) - 1, blk, 0)
    items = [(t, g, half) for t in range(n_blk) for g in range(SWA_WIDTH // LANES)
             for half in range(2)]
    logits = {}
    for t, g, half in items:
        qg = q_ref[t * blk:(t + 1) * blk, g * LANES:(g + 1) * LANES]
        qm = jnp.where(lo, qg, zero_q) if half == 0 else jnp.where(lo, zero_q, qg)
        logits[t, g, half] = _dot_nt(qm, kb[t * blk:(t + 3) * blk])
    scaled = {}
    for t, g, half in items:
        keep_prev = kj >= (qi + first if t == 0 else qi)
        keep_next = kj <= (qi - last if t == n_blk - 1 else qi)
        s = logits[t, g, half]
        s_prev = jnp.where(keep_prev, s[:, :blk], NEG_INF)
        s_own = s[:, blk:2 * blk]
        s_next = jnp.where(keep_next, s[:, 2 * blk:], NEG_INF)
        sk = sink_ref[layer, g + 2 * half] * LOG2E
        m = jnp.max(jnp.maximum(jnp.maximum(s_prev, s_own), s_next), axis=-1, keepdims=True)
        m = jnp.maximum(m, sk)
        pexp = jnp.concatenate([jnp.exp2(v - m) for v in (s_prev, s_own, s_next)],
                               axis=-1).astype(BF16)
        acc = _dot(pexp, v_half[half][t * blk:(t + 3) * blk])
        ones_lane = HALF if half == 0 else 0
        den = acc[:, ones_lane:ones_lane + 1] + jnp.exp2(sk - m)
        scaled[t, g, half] = acc / den
    for t in range(n_blk):
        y = jnp.concatenate([jnp.where(lo, scaled[t, g, 0], scaled[t, g, 1])
                             for g in range(SWA_WIDTH // LANES)], axis=-1)
        o_ref[t * blk:(t + 1) * blk, :] = _rms(y, nw_ref[...]).astype(BF16)


def _swa(q, k, v, sink, nw, batch, seq, n_blk):
    blk = SWA_BLOCK
    nb = seq // blk
    ns = nb // n_blk
    cur = lambda b, j: (b * ns + j, 0)
    prv = lambda b, j: (b * nb + jnp.maximum(j * n_blk - 1, 0), 0)
    nxt = lambda b, j: (b * nb + jnp.minimum((j + 1) * n_blk, nb - 1), 0)
    edge = lambda f: pl.BlockSpec((blk, LANES), f)
    main = pl.BlockSpec((n_blk * blk, LANES), cur)
    sinks, layer = sink
    nw_arr, nw_spec = _resident(nw)
    return pl.pallas_call(
        functools.partial(_swa_body, layer=layer),
        grid=(batch, ns),
        in_specs=[pl.BlockSpec(memory_space=pltpu.SMEM),
                  pl.BlockSpec((n_blk * blk, SWA_WIDTH), cur),
                  edge(prv), main, edge(nxt), edge(prv), main, edge(nxt), nw_spec],
        out_specs=pl.BlockSpec((n_blk * blk, SWA_WIDTH), cur),
        out_shape=jax.ShapeDtypeStruct((batch * seq, SWA_WIDTH), BF16),
        compiler_params=pltpu.CompilerParams(dimension_semantics=("parallel", "parallel")),
        name="swa",
    )(sinks, q, k, k, k, v, v, v, nw_arr)


def _mla_body(bound_ref, q_ref, k_ref, v_ref, nw_ref, o_ref, *, key_chunk, layer):
    tq = q_ref.shape[0]
    seq = k_ref.shape[0]
    heads = [slice(h * LANES, (h + 1) * LANES) for h in range(MLA_HEADS)]
    chunks = [slice(c * key_chunk, (c + 1) * key_chunk) for c in range(seq // key_chunk)]
    lo = _lane_iota((tq, LANES)) < HALF
    bound = bound_ref[layer]
    use_bound = bound <= MLA_BOUND_LIMIT

    def denom(acc, h):
        return acc[:, HALF:HALF + 1] if h % 2 == 0 else acc[:, 0:1]

    def finish(accs, dens):
        outs = [jnp.where(lo, accs[2 * p] / dens[2 * p], accs[2 * p + 1] / dens[2 * p + 1])
                for p in range(MLA_HEADS // 2)]
        y = jnp.concatenate(outs, axis=-1)
        o_ref[...] = _rms(y, nw_ref[...]).astype(BF16)

    @pl.when(use_bound)
    def _bounded():
        accs = []
        for h, sl in enumerate(heads):
            qh = q_ref[:, sl]
            acc = None
            for ks in chunks:
                pexp = jnp.exp2(_dot_nt(qh, k_ref[ks, sl]) - bound).astype(BF16)
                part = _dot(pexp, v_ref[ks, sl])
                acc = part if acc is None else acc + part
            accs.append(acc)
        finish(accs, [denom(a, h) for h, a in enumerate(accs)])

    @pl.when(jnp.logical_not(use_bound))
    def _online():
        accs, dens = [], []
        for h, sl in enumerate(heads):
            qh = q_ref[:, sl]
            m = jnp.full((tq, 1), NEG_INF, F32)
            acc = jnp.zeros((tq, LANES), F32)
            for ks in chunks:
                s = _dot_nt(qh, k_ref[ks, sl])
                m_new = jnp.maximum(m, jnp.max(s, axis=-1, keepdims=True))
                pexp = jnp.exp2(s - m_new).astype(BF16)
                acc = jnp.exp2(m - m_new) * acc + _dot(pexp, v_ref[ks, sl])
                m = m_new
            accs.append(acc)
            dens.append(denom(acc, h))
        finish(accs, dens)


def _mla(q, k, v, bound, nw, batch, seq, tq, key_chunk):
    nq = seq // tq
    w = MLA_HEADS * LANES
    full = pl.BlockSpec((seq, w), lambda b, i: (b, 0))
    bounds, layer = bound
    nw, nw_spec = _resident(nw)
    return pl.pallas_call(
        functools.partial(_mla_body, key_chunk=key_chunk, layer=layer),
        grid=(batch, nq),
        in_specs=[pl.BlockSpec(memory_space=pltpu.SMEM),
                  pl.BlockSpec((tq, w), lambda b, i: (b * nq + i, 0)), full, full, nw_spec],
        out_specs=pl.BlockSpec((tq, MLA_WIDTH), lambda b, i: (b * nq + i, 0)),
        out_shape=jax.ShapeDtypeStruct((batch * seq, MLA_WIDTH), BF16),
        compiler_params=pltpu.CompilerParams(
            dimension_semantics=("parallel", "parallel"),
            vmem_limit_bytes=48 * 1024 * 1024),
        name="mla",
    )(bounds, q, k, v, nw)


def _rope_tables(seq):
    def angles(dim):
        inv = 1.0 / np.power(ROPE_THETA, np.arange(0, dim, 2, dtype=np.float64) / dim)
        return np.arange(seq, dtype=np.float64)[:, None] * inv[None, :]

    a64 = angles(SWA_HEAD_DIM)
    c, s = np.cos(a64), np.sin(a64)
    zero = np.zeros_like(c)
    c64 = np.concatenate([c, c, c, c], axis=-1)
    sa64 = np.concatenate([-s, zero, -s, zero], axis=-1)
    sb64 = np.concatenate([zero, s, zero, s], axis=-1)
    a32 = angles(MLA_ROPE)
    c, s = np.cos(a32), np.sin(a32)
    zero = np.zeros_like(c)
    ones = np.ones((seq, MLA_NOPE))
    pad = np.zeros((seq, LANES - MLA_QK))
    zn = np.zeros((seq, MLA_NOPE))
    cm = np.concatenate([ones, c, c, pad], axis=-1)
    sam = np.concatenate([zn, -s, zero, pad], axis=-1)
    sbm = np.concatenate([zn, zero, s, pad], axis=-1)
    return jnp.asarray(np.stack([c64, sa64, sb64, cm, sam, sbm]).astype(np.float32))


def _swa_head_perm(t, axis):
    parts = jnp.split(t, SWA_HEADS, axis=axis)
    return jnp.concatenate([parts[0], parts[2], parts[1], parts[3]], axis=axis)


def _prep_params(ffn1_norm, mix_norm, w_in, ssd_conv_w, ssd_conv_b, ssd_dt_bias, ssd_a_log,
                 ssd_d, ssd_norm, swa_q_norm, swa_k_norm, swa_sink, swa_out_norm,
                 mla_q_lat_norm, mla_w_uq, mla_kv_norm, mla_w_ukv, mla_q_norm, mla_k_norm,
                 mla_out_norm, w_out, ffn2_norm):
    nl, d = w_in.shape[:2]
    row = lambda v: v.reshape(nl, 1, -1).astype(F32)
    pad_lanes = lambda v, n: jnp.pad(v, [(0, 0)] * (v.ndim - 1) + [(0, n - v.shape[-1])])

    w_in_p = _permute_w_in(w_in)

    wuq = mla_w_uq.reshape(nl, MLA_Q_RANK, MLA_HEADS, MLA_QK)
    wuq = pad_lanes(wuq, LANES).reshape(nl, MLA_Q_RANK, MLA_HEADS * LANES).astype(BF16)
    wukv = mla_w_ukv.reshape(nl, MLA_KV_RANK, MLA_HEADS, MLA_NOPE + MLA_V)
    knope = pad_lanes(wukv[..., :MLA_NOPE], LANES)
    vv = wukv[..., MLA_NOPE:]
    zv = jnp.zeros_like(vv)
    even = (np.arange(MLA_HEADS) % 2 == 0)[None, None, :, None]
    vpad = jnp.concatenate([jnp.where(even, vv, zv), jnp.where(even, zv, vv)], axis=-1)
    wukv_p = jnp.concatenate([knope.reshape(nl, MLA_KV_RANK, -1),
                              vpad.reshape(nl, MLA_KV_RANK, -1)], axis=-1).astype(BF16)

    a, b = SSD_INNER, SSD_INNER + SWA_WIDTH
    w_out_p = jnp.concatenate([w_out[:, :a], _swa_head_perm(w_out[:, a:b], 1), w_out[:, b:]],
                              axis=1).astype(BF16)

    peak = lambda g: jnp.max(jnp.abs(g.astype(F32)), axis=-1)
    mla_bound = (math.sqrt(MLA_QK) * LOG2E * MLA_BOUND_SLACK) * peak(mla_q_norm) * peak(mla_k_norm)

    two = lambda v: jnp.concatenate([v, v], axis=-1)
    return {
        "mla_bound": mla_bound,
        "ffn1_norm": row(ffn1_norm),
        "ffn2_norm": row(ffn2_norm),
        "w_in": w_in_p,
        "mix_norm": row(mix_norm),
        "conv_w": ssd_conv_w.astype(F32),
        "conv_b": row(ssd_conv_b),
        "dt_bias": pad_lanes(row(ssd_dt_bias), LANES),
        "swa_q_gain": row(two(swa_q_norm)),
        "swa_k_gain": row(two(swa_k_norm)),
        "mla_qlat_norm": row(mla_q_lat_norm),
        "w_uq": wuq,
        "mla_kv_norm": row(mla_kv_norm),
        "w_ukv": wukv_p,
        "mla_q_gain": pad_lanes(row(mla_q_norm), LANES),
        "mla_k_gain": pad_lanes(row(mla_k_norm), LANES),
        "ssd_a_log": pad_lanes(row(ssd_a_log), LANES),
        "ssd_d": row(jnp.repeat(ssd_d, SSD_HEAD_DIM, axis=-1)),
        "ssd_norm": row(ssd_norm),
        "swa_sink": swa_sink.astype(F32),
        "swa_out_norm": row(_swa_head_perm(swa_out_norm, 1)),
        "mla_out_norm": row(mla_out_norm),
        "w_out": w_out_p,
    }


def _tiles(seq):
    pick = lambda pref: math.gcd(seq, pref)
    return {"ffn": pick(1024), "proj": pick(512), "mla_q": pick(256), "mla_k": pick(1024),
            "swa_blocks": pick(16 * SWA_BLOCK) // SWA_BLOCK,
            "ssd_chunks": pick(4 * SSD_CHUNK) // SSD_CHUNK}


def kernel(x, ffn1_norm, ffn1_gate, ffn1_up, ffn1_down, mix_norm, w_in, ssd_conv_w, ssd_conv_b, ssd_dt_bias, ssd_a_log, ssd_d, ssd_norm, swa_q_norm, swa_k_norm, swa_sink, swa_out_norm, mla_q_lat_norm, mla_w_uq, mla_kv_norm, mla_w_ukv, mla_q_norm, mla_k_norm, mla_out_norm, w_out, ffn2_norm, ffn2_gate, ffn2_up, ffn2_down):
    batch, seq, d = x.shape
    assert seq % SSD_CHUNK == 0 and seq % SWA_BLOCK == 0
    assert ffn1_gate.shape[-1] % FFN_CHUNK == 0
    stacked = _prep_params(ffn1_norm, mix_norm, w_in, ssd_conv_w, ssd_conv_b, ssd_dt_bias,
                           ssd_a_log, ssd_d, ssd_norm, swa_q_norm, swa_k_norm, swa_sink,
                           swa_out_norm, mla_q_lat_norm, mla_w_uq, mla_kv_norm, mla_w_ukv,
                           mla_q_norm, mla_k_norm, mla_out_norm, w_out, ffn2_norm)
    tl = _tiles(seq)
    tabs = _rope_tables(seq)
    xf = x.reshape(batch * seq, d).astype(F32)
    for l in range(w_in.shape[0]):
        p = {k: (v, l) for k, v in stacked.items()}
        ffn1_w = _cast_ffn_weights(l, ffn1_gate, ffn1_up, ffn1_down)
        ffn2_w = _cast_ffn_weights(l, ffn2_gate, ffn2_up, ffn2_down)
        xf = _ffn(xf, p["ffn1_norm"], *ffn1_w, tm=tl["ffn"])
        z, xbc, dt, swq, swk, swv, mq, mk, mv = _in_proj(xf, tabs, p, seq, tl["proj"])
        y_ssd = _ssd(xbc, dt, z, p, batch, seq, tl["ssd_chunks"])
        y_swa = _swa(swq, swk, swv, p["swa_sink"], p["swa_out_norm"], batch, seq,
                     tl["swa_blocks"])
        y_mla = _mla(mq, mk, mv, p["mla_bound"], p["mla_out_norm"], batch, seq, tl["mla_q"],
                     tl["mla_k"])
        xf = _ffn(xf, p["ffn2_norm"], *ffn2_w, tm=tl["ffn"],
                  mixer=(y_ssd, y_swa, y_mla, p["w_out"]))
    return xf.reshape(batch, seq, d).astype(x.dtype)
```

```python
import functools
import math

import jax
import jax.numpy as jnp
import numpy as np
from jax import lax
from jax.experimental import pallas as pl
from jax.experimental.pallas import tpu as pltpu

F32 = jnp.float32
BF16 = jnp.bfloat16

LANES = 128
HALF = LANES // 2

SSD_HEADS = 8
SSD_HEAD_DIM = 64
SSD_INNER = SSD_HEADS * SSD_HEAD_DIM
SSD_GROUPS = 2
SSD_STATE = 128
SSD_CONV = 5
SSD_CHUNK = 256
SSD_CONV_DIM = SSD_INNER + 2 * SSD_GROUPS * SSD_STATE
SWA_HEADS = 4
SWA_KV_HEADS = 2
SWA_HEAD_DIM = 64
SWA_WINDOW = 128
SWA_BLOCK = 128
SWA_WIDTH = SWA_HEADS * SWA_HEAD_DIM
MLA_HEADS = 4
MLA_Q_RANK = 256
MLA_KV_RANK = 128
MLA_NOPE = 64
MLA_ROPE = 32
MLA_QK = MLA_NOPE + MLA_ROPE
MLA_V = 64
MLA_WIDTH = MLA_HEADS * MLA_V
ROPE_THETA = 10000.0
EPS = 1e-6

C_Z = 0
C_XBC = C_Z + SSD_INNER
C_SWQ = C_XBC + SSD_CONV_DIM
C_SWK = C_SWQ + SWA_WIDTH
C_SWV = C_SWK + LANES
C_MLQ = C_SWV + LANES
C_CKV = C_MLQ + MLA_Q_RANK
C_MISC = C_CKV + MLA_KV_RANK
C_END = C_MISC + LANES
CONV_HALO = 16
FFN_CHUNK = 256
FFN_RESIDUAL_SCALE = 0.5
NEG_INF = float("-inf")
LOG_FLOOR = -1e30
LOG2E = math.log2(math.e)
MLA_BOUND_LIMIT = 48.0
MLA_BOUND_SLACK = 1.02


def _rms(x, w):
    ms = jnp.mean(x * x, axis=-1, keepdims=True)
    return x * lax.rsqrt(ms + EPS) * w


def _silu(x):
    return x / (1.0 + jnp.exp(-x))


def _dot(a, b):
    return jnp.dot(a, b, preferred_element_type=F32)


def _dot_nt(a, b):
    return lax.dot_general(a, b, (((1,), (1,)), ((), ())), preferred_element_type=F32)


def _dot_tn(a, b):
    return lax.dot_general(a, b, (((0,), (0,)), ((), ())), preferred_element_type=F32)


def _lane_iota(shape):
    return lax.broadcasted_iota(jnp.int32, shape, len(shape) - 1)


def _resident(param, single_buffer=False):
    mode = {"pipeline_mode": pl.Buffered(1)} if single_buffer else {}
    if isinstance(param, tuple):
        arr, layer = param
        rest = (0,) * (arr.ndim - 1)
        return arr, pl.BlockSpec((None,) + arr.shape[1:], lambda *_: (layer,) + rest, **mode)
    zeros = (0,) * param.ndim
    return param, pl.BlockSpec(param.shape, lambda *_: zeros, **mode)


def _cast_body(gate_ref, up_ref, down_ref, gate_out, up_out, down_out):
    gate_out[...] = gate_ref[...].astype(BF16)
    up_out[...] = up_ref[...].astype(BF16)
    down_out[...] = (down_ref[...] * FFN_RESIDUAL_SCALE).astype(BF16)


def _cast_ffn_weights(gate, up, down, steps=4):
    nl, d, f = gate.shape
    col = pl.BlockSpec((None, d // steps, f), lambda l, i: (l, i, 0))
    rowb = pl.BlockSpec((None, f // steps, d), lambda l, i: (l, i, 0))
    return pl.pallas_call(
        _cast_body,
        grid=(nl, steps),
        in_specs=[col, col, rowb],
        out_specs=[col, col, rowb],
        out_shape=[jax.ShapeDtypeStruct((nl, d, f), BF16), jax.ShapeDtypeStruct((nl, d, f), BF16),
                   jax.ShapeDtypeStruct((nl, f, d), BF16)],
        compiler_params=pltpu.CompilerParams(
            dimension_semantics=("parallel", "parallel"), vmem_limit_bytes=40 * 1024 * 1024),
        name="cast_ffn_weights",
    )(gate, up, down)


def _ffn_body(*refs, fuse_out_proj):
    if fuse_out_proj:
        x_ref, ys_ref, yw_ref, ym_ref, wo_ref = refs[:5]
        a, b = SSD_INNER, SSD_INNER + SWA_WIDTH
        x = (x_ref[...] + _dot(ys_ref[...], wo_ref[0:a, :]) + _dot(yw_ref[...], wo_ref[a:b, :])
             + _dot(ym_ref[...], wo_ref[b:, :]))
        refs = refs[5:]
    else:
        x_ref = refs[0]
        x = x_ref[...]
        refs = refs[1:]
    nw_ref, wg_ref, wu_ref, wd_ref, o_ref, xn_ref = refs
    o_ref[...] = x
    xn_ref[...] = _rms(x, nw_ref[...]).astype(BF16)
    for c in range(wg_ref.shape[1] // FFN_CHUNK):
        cols = slice(c * FFN_CHUNK, (c + 1) * FFN_CHUNK)
        xn = xn_ref[...]
        g = _dot(xn, wg_ref[:, cols])
        u = _dot(xn, wu_ref[:, cols])
        h = (_silu(g) * u).astype(BF16)
        o_ref[...] += _dot(h, wd_ref[cols, :])


def _ffn(x, nw, wg, wu, wd, tm, mixer=None):
    n, d = x.shape
    row = lambda w: pl.BlockSpec((tm, w), lambda i: (i, 0))
    args, specs = [x], [row(d)]
    resident = [_resident(nw)] + [_resident(w, single_buffer=True) for w in (wg, wu, wd)]
    if mixer is not None:
        args += list(mixer[:3])
        specs += [row(a.shape[1]) for a in mixer[:3]]
        resident.insert(0, _resident(mixer[3]))
    args += [a for a, _ in resident]
    specs += [s for _, s in resident]
    return pl.pallas_call(
        functools.partial(_ffn_body, fuse_out_proj=mixer is not None),
        grid=(n // tm,),
        in_specs=specs,
        out_specs=row(d),
        out_shape=jax.ShapeDtypeStruct((n, d), F32),
        scratch_shapes=[pltpu.VMEM((tm, d), BF16)],
        compiler_params=pltpu.CompilerParams(
            dimension_semantics=("parallel",), vmem_limit_bytes=56 * 1024 * 1024),
        name="ffn",
    )(*args)


def _permute_w_in_body(src_ref, dst_ref):
    n_src, cols = src_ref.shape
    hd = SWA_HEAD_DIM
    dt0 = C_SWQ
    q0 = dt0 + 2 * SSD_HEADS
    rest0 = q0 + SWA_WIDTH
    kr0 = n_src - MLA_ROPE
    zeros = lambda n: jnp.zeros((n, cols), F32)
    pieces = [src_ref[0:dt0, :]]
    pieces += [src_ref[q0 + h * hd:q0 + (h + 1) * hd, :] for h in (0, 2, 1, 3)]
    pieces += [src_ref[rest0:kr0, :],
               src_ref[dt0:q0, :], zeros(MLA_NOPE - 2 * SSD_HEADS),
               src_ref[kr0:n_src, :], zeros(LANES - MLA_QK)]
    dst_ref[...] = jnp.concatenate(pieces, axis=0).T.astype(BF16)


def _permute_w_in(w_in_t, col_block=256):
    nl, n_src, d = w_in_t.shape
    assert n_src + (MLA_NOPE - 2 * SSD_HEADS) + (LANES - MLA_QK) == C_END
    return pl.pallas_call(
        _permute_w_in_body,
        grid=(nl, d // col_block),
        in_specs=[pl.BlockSpec((None, n_src, col_block), lambda l, i: (l, 0, i))],
        out_specs=pl.BlockSpec((None, col_block, C_END), lambda l, i: (l, i, 0)),
        out_shape=jax.ShapeDtypeStruct((nl, d, C_END), BF16),
        compiler_params=pltpu.CompilerParams(dimension_semantics=("parallel", "parallel")),
        name="permute_w_in",
    )(w_in_t)


def _rope(y, c, sa, sb, shift):
    return (y * c + pltpu.roll(y, LANES - shift, 1) * sa + pltpu.roll(y, shift, 1) * sb)


def _head_sum_matrix(width):
    r = lax.broadcasted_iota(jnp.int32, (LANES, LANES), 0)
    c = lax.broadcasted_iota(jnp.int32, (LANES, LANES), 1)
    same = (r < width) if width > HALF else ((r < HALF) == (c < HALF))
    return jnp.where(same, 1.0, 0.0).astype(BF16)


def _head_norm_rope(x, sumsq, gain, c, sa, sb, width, rope_dim, post_scale):
    y = x * lax.rsqrt(sumsq / float(width) + EPS) * gain
    out = _rope(y, c, sa, sb, rope_dim // 2)
    return out if post_scale == 1.0 else out * post_scale


def _inproj_body(x_ref, xp_ref, xnx_ref, tab_ref, w_ref, mixw_ref, convw_ref, convb_ref,
                 dtb_ref, swqg_ref, swkg_ref, qlw_ref, wuq_ref, kvw_ref, wukv_ref,
                 mqg_ref, mkg_ref,
                 z_ref, xbc_ref, dt_ref, swq_ref, swk_ref, swv_ref, mq_ref, mk_ref, mv_ref,
                 *, tiles_per_seq):
    i = pl.program_id(0)
    tm = x_ref.shape[0]
    pos_tile = i % tiles_per_seq
    mixw = mixw_ref[...]
    hn = _rms(x_ref[...], mixw).astype(BF16)
    hp = _rms(xp_ref[...], mixw).astype(BF16)
    hx = _rms(xnx_ref[...], mixw).astype(BF16)
    groups = lambda v: [v[:, g * LANES:(g + 1) * LANES] for g in range(v.shape[1] // LANES)]

    z = _dot(hn, w_ref[:, C_Z:C_XBC])
    conv_w = 2 * LANES
    xbc_parts = []
    for c0 in range(C_XBC, C_SWQ, conv_w):
        wx = w_ref[:, c0:c0 + conv_w]
        xbc_parts.append((_dot(hp, wx), _dot(hn, wx), _dot(hx, wx)))
    ckv, misc = groups(_dot(hn, w_ref[:, C_CKV:C_END]))
    swa_q = groups(_dot(hn, w_ref[:, C_SWQ:C_SWK]))
    swa_k, swa_v = groups(_dot(hn, w_ref[:, C_SWK:C_MLQ]))
    ql = _dot(hn, w_ref[:, C_MLQ:C_CKV])

    q_up = groups(_dot(_rms(ql, qlw_ref[...]).astype(BF16), wuq_ref[...]))
    kv_up = groups(_dot(_rms(ckv, kvw_ref[...]).astype(BF16), wukv_ref[...]))
    lane = _lane_iota(misc.shape)
    kr = jnp.where((lane >= MLA_NOPE) & (lane < MLA_QK), misc, 0.0)
    mla_q = q_up
    mla_k = [k_nope + kr for k_nope in kv_up[:MLA_HEADS]]
    mla_v = kv_up[MLA_HEADS:]

    sum64, sum96 = _head_sum_matrix(SWA_HEAD_DIM), _head_sum_matrix(MLA_QK)
    sumsq = lambda xs, m: [_dot((v * v).astype(BF16), m) for v in xs]
    ss_swa_q, ss_swa_k = sumsq(swa_q, sum64), sumsq([swa_k], sum64)
    ss_mla_q, ss_mla_k = sumsq(mla_q, sum96), sumsq(mla_k, sum96)

    z_ref[...] = z.astype(BF16)

    keep_prev = (pos_tile > 0).astype(F32)
    keep_next = (pos_tile < tiles_per_seq - 1).astype(F32)
    for part, (xbc_prev, xbc_main, xbc_next) in enumerate(xbc_parts):
        cs = slice(part * conv_w, (part + 1) * conv_w)
        padded = jnp.concatenate([xbc_prev * keep_prev, xbc_main, xbc_next * keep_next], axis=0)
        n_pad = padded.shape[0]
        conv = jnp.broadcast_to(convb_ref[:, cs], (tm, conv_w))
        for k in range(SSD_CONV):
            shift = SSD_CONV // 2 - k
            tap = padded if shift == 0 else pltpu.roll(padded, shift % n_pad, 0)
            conv = conv + convw_ref[k:k + 1, cs] * tap[CONV_HALO:CONV_HALO + tm, :]
        xbc_ref[:, cs] = _silu(conv).astype(BF16)

    dtv = misc + dtb_ref[...]
    dt_ref[...] = jnp.maximum(dtv, 0.0) + jnp.log1p(jnp.exp(-jnp.abs(dtv)))

    swa_rope = (tab_ref[0], tab_ref[1], tab_ref[2], SWA_HEAD_DIM, SWA_HEAD_DIM)
    for g, (v, ss) in enumerate(zip(swa_q, ss_swa_q)):
        swq_ref[:, g * LANES:(g + 1) * LANES] = _head_norm_rope(
            v, ss, swqg_ref[...], *swa_rope, SWA_HEAD_DIM ** -0.5 * LOG2E).astype(BF16)
    swk_ref[...] = _head_norm_rope(swa_k, ss_swa_k[0], swkg_ref[...], *swa_rope, 1.0).astype(BF16)
    swv_ref[...] = swa_v.astype(BF16)

    mla_rope = (tab_ref[3], tab_ref[4], tab_ref[5], MLA_QK, MLA_ROPE)
    for h in range(MLA_HEADS):
        sl = slice(h * LANES, (h + 1) * LANES)
        mq_ref[:, sl] = _head_norm_rope(mla_q[h], ss_mla_q[h], mqg_ref[...], *mla_rope,
                                        MLA_QK ** -0.5 * LOG2E).astype(BF16)
        mk_ref[:, sl] = _head_norm_rope(mla_k[h], ss_mla_k[h], mkg_ref[...], *mla_rope,
                                        1.0).astype(BF16)
        ones_lane = HALF if h % 2 == 0 else 0
        mv_ref[:, sl] = jnp.where(lane == ones_lane, 1.0, mla_v[h]).astype(BF16)


def _in_proj(x, tabs, p, seq, tm):
    n, d = x.shape
    tiles_per_seq = seq // tm
    halo_per_tile = tm // CONV_HALO
    n_halo = n // CONV_HALO
    row = lambda w: pl.BlockSpec((tm, w), lambda i: (i, 0))
    in_specs = [
        row(d),
        pl.BlockSpec((CONV_HALO, d), lambda i: (jnp.maximum(i * halo_per_tile - 1, 0), 0)),
        pl.BlockSpec((CONV_HALO, d),
                     lambda i: (jnp.minimum((i + 1) * halo_per_tile, n_halo - 1), 0)),
        pl.BlockSpec((6, tm, LANES), lambda i: (0, i % tiles_per_seq, 0)),
    ]
    names = ("w_in", "mix_norm", "conv_w", "conv_b", "dt_bias", "swa_q_gain", "swa_k_gain",
             "mla_qlat_norm", "w_uq", "mla_kv_norm", "w_ukv", "mla_q_gain", "mla_k_gain")
    resident = [_resident(p[k]) for k in names]
    args = [a for a, _ in resident]
    in_specs += [s for _, s in resident]
    widths = (SSD_INNER, SSD_CONV_DIM, LANES, SWA_WIDTH, LANES, LANES,
              MLA_HEADS * LANES, MLA_HEADS * LANES, MLA_HEADS * LANES)
    dts = (BF16, BF16, F32, BF16, BF16, BF16, BF16, BF16, BF16)
    return pl.pallas_call(
        functools.partial(_inproj_body, tiles_per_seq=tiles_per_seq),
        grid=(n // tm,),
        in_specs=in_specs,
        out_specs=[row(w) for w in widths],
        out_shape=[jax.ShapeDtypeStruct((n, w), t) for w, t in zip(widths, dts)],
        compiler_params=pltpu.CompilerParams(
            dimension_semantics=("parallel",), vmem_limit_bytes=48 * 1024 * 1024),
        name="in_proj",
    )(x, x, x, tabs, *args)


def _split_bf16(v, n):
    pieces = []
    for _ in range(n - 1):
        p = v.astype(BF16)
        pieces.append(p)
        v = v - p.astype(F32)
    pieces.append(v.astype(BF16))
    return pieces


def _pack_lanes(pieces, width):
    lane = _lane_iota(pieces[0].shape)
    out = jnp.zeros(pieces[0].shape, F32)
    for k, p in enumerate(pieces):
        pf = p.astype(F32)
        if k:
            pf = pltpu.roll(pf, k * width, 1)
        out = jnp.where((lane >= k * width) & (lane < (k + 1) * width), pf, out)
    return out.astype(BF16)


def _ssd_expand_matrices():
    nd = 2 * SSD_HEADS
    rows = np.arange(LANES)[:, None]
    head = np.arange(SSD_INNER)[None, :] // SSD_HEAD_DIM
    mats = []
    for direction in range(2):
        for first_piece in (0, 2):
            d = direction * SSD_HEADS + head
            hit = (rows == first_piece * nd + d) | (rows == (first_piece + 1) * nd + d)
            mats.append(hit)
    return jnp.asarray(np.stack(mats).astype(np.float32), BF16)


def _ssd_body(xbc_ref, dt_ref, z_ref, alog_ref, dskip_ref, nw_ref, expand_ref, o_ref,
              yacc_ref, sf_ref, sb_ref, cols_ref, *, n_chunks):
    pas = pl.program_id(1)
    j = pl.program_id(2)
    t = SSD_CHUNK
    cps = xbc_ref.shape[0] // t
    nh = SSD_HEADS

    gw = SSD_INNER // SSD_GROUPS
    hpg = nh // SSD_GROUPS
    nd = 2 * nh
    lane = _lane_iota((t, LANES))

    def b_of(rows, g):
        return xbc_ref[rows, SSD_INNER + g * SSD_STATE:SSD_INNER + (g + 1) * SSD_STATE]

    def c_of(rows, g):
        return xbc_ref[rows, SSD_INNER + (SSD_GROUPS + g) * SSD_STATE:
                       SSD_INNER + (SSD_GROUPS + g + 1) * SSD_STATE]

    def forward_early(k):
        rows = slice(k * t, (k + 1) * t)
        fwd = lane < nh
        dt = dt_ref[rows, :]
        avec = jnp.where(_lane_iota((1, LANES)) < nd, -jnp.exp(alog_ref[...]), 0.0)
        a = dt * avec
        row = lax.broadcasted_iota(jnp.int32, (t, t), 0)
        col = lax.broadcasted_iota(jnp.int32, (t, t), 1)
        tri = jnp.where(col <= row, 1.0, 0.0).astype(BF16)
        cum = sum(_dot(tri, part) for part in _split_bf16(a, 3))
        excl = cum - a
        tot = cum[t - 1:t, :]

        e_in = jnp.exp(jnp.where(fwd, cum, tot - excl))
        w_st = jnp.exp(jnp.where(fwd, tot - cum, excl)) * dt
        e1, e2 = _split_bf16(e_in, 2)
        w1, w2 = _split_bf16(w_st, 2)
        cols = _pack_lanes([e1, e2, w1, w2], nd)
        cols_ref[j * cps + k] = cols

        r = jnp.where(fwd, cum, -excl) * LOG2E
        cc = r - jnp.maximum(jnp.log(dt), LOG_FLOOR) * LOG2E
        ones = jnp.ones((t, LANES), F32)
        lhs = jnp.where(lane < 3 * nd, _pack_lanes(_split_bf16(r, 3), nd),
                        jnp.where(lane < 6 * nd, ones, 0.0).astype(BF16))
        cc_t = cc.T[0:nd, :]
        c1, c2, c3 = _split_bf16(cc_t, 3)
        rhs = jnp.concatenate(
            [jnp.ones((3 * nd, t), BF16), -c1, -c2, -c3, jnp.zeros((LANES - 6 * nd, t), BF16)],
            axis=0)
        sub = lax.broadcasted_iota(jnp.int32, (LANES, t), 0) % nd

        e_x = _dot(cols, expand_ref[0])
        w_x = _dot(cols, expand_ref[1])
        cbs = [_dot_nt(c_of(rows, g), b_of(rows, g)) for g in range(SSD_GROUPS)]
        ys = []
        for g in range(SSD_GROUPS):
            gs = slice(g * gw, (g + 1) * gw)
            s_in = sf_ref[g]
            xgf = xbc_ref[rows, gs].astype(F32)
            ys.append(dskip_ref[:, gs] * xgf + _dot(c_of(rows, g), s_in.astype(BF16)) * e_x[:, gs])
            xw = (xgf * w_x[:, gs]).astype(BF16)
            sf_ref[g] = s_in * e_x[t - 1:t, gs] + _dot_tn(b_of(rows, g), xw)
        exps = [_dot(lhs, jnp.where(sub == idx, rhs, jnp.zeros_like(rhs))) for idx in range(nd)]
        return cbs, ys, exps

    def forward_late(k, cbs, ys, exps):
        rows = slice(k * t, (k + 1) * t)
        th = t // 2
        quad = lambda v, qi, qj: v[qi * th:(qi + 1) * th, qj * th:(qj + 1) * th]
        qrow = lax.broadcasted_iota(jnp.int32, (th, th), 0)
        qcol = lax.broadcasted_iota(jnp.int32, (th, th), 1)
        on_low = qrow >= qcol
        on_up = qcol >= qrow

        def mixing(h, cb):
            df, db = exps[h], exps[nh + h]
            diag = [quad(cb, q, q) * (jnp.exp2(jnp.where(on_low, quad(df, q, q), NEG_INF))
                                      + jnp.exp2(jnp.where(on_up, quad(db, q, q), NEG_INF)))
                    for q in range(2)]
            upper = quad(cb, 0, 1) * jnp.exp2(quad(db, 0, 1))
            lower = quad(cb, 1, 0) * jnp.exp2(quad(df, 1, 0))
            return jnp.concatenate([jnp.concatenate([diag[0], upper], axis=1),
                                    jnp.concatenate([lower, diag[1]], axis=1)], axis=0)

        glane = _lane_iota((t, gw))
        for g in range(SSD_GROUPS):
            gs = slice(g * gw, (g + 1) * gw)
            xg = xbc_ref[rows, gs]
            y = ys[g]
            for hh in range(hpg):
                m = mixing(g * hpg + hh, cbs[g]).astype(BF16)
                mine = (glane >= hh * SSD_HEAD_DIM) & (glane < (hh + 1) * SSD_HEAD_DIM)
                y = y + _dot(m, jnp.where(mine, xg, jnp.zeros_like(xg)))
            yacc_ref[j * cps + k, :, gs] = y

    @pl.when(pas == 0)
    def _forward():
        @pl.when(j == 0)
        def _():
            sf_ref[...] = jnp.zeros_like(sf_ref)

        early = [forward_early(k) for k in range(cps)]
        for k in range(cps):
            forward_late(k, *early[k])

    @pl.when(pas == 1)
    def _backward():
        @pl.when(j == 0)
        def _():
            sb_ref[...] = jnp.zeros_like(sb_ref)

        for k in reversed(range(cps)):
            rows = slice(k * t, (k + 1) * t)
            c = n_chunks - 1 - j * cps - (cps - 1 - k)
            cols = cols_ref[c]
            e_x = _dot(cols, expand_ref[2])
            w_x = _dot(cols, expand_ref[3])
            gate = _silu(z_ref[rows, :].astype(F32))
            for g in range(SSD_GROUPS):
                gs = slice(g * gw, (g + 1) * gw)
                s_in = sb_ref[g]
                y = yacc_ref[c, :, gs] + _dot(c_of(rows, g), s_in.astype(BF16)) * e_x[:, gs]
                o_ref[rows, gs] = _rms(y * gate[:, gs], nw_ref[:, gs]).astype(BF16)
                xw = (xbc_ref[rows, gs].astype(F32) * w_x[:, gs]).astype(BF16)
                sb_ref[g] = s_in * e_x[0:1, gs] + _dot_tn(b_of(rows, g), xw)


def _ssd(xbc, dt, z, p, batch, seq, cps):
    t = SSD_CHUNK
    nc = seq // t
    ns = nc // cps

    def chunk_idx(b, pas, j):
        return b * ns + j + pas * (ns - 1 - 2 * j)

    def out_idx(b, pas, j):
        return b * ns + (ns - 1) - pas * j

    blk = lambda w, f: pl.BlockSpec((cps * t, w), lambda b, pas, j: (f(b, pas, j), 0))
    resident = [_resident(a) for a in (p["ssd_a_log"], p["ssd_d"], p["ssd_norm"],
                                       _ssd_expand_matrices())]
    state = pltpu.VMEM((SSD_GROUPS, SSD_STATE, SSD_INNER // SSD_GROUPS), F32)
    return pl.pallas_call(
        functools.partial(_ssd_body, n_chunks=nc),
        grid=(batch, 2, ns),
        in_specs=[blk(SSD_CONV_DIM, chunk_idx), blk(LANES, chunk_idx), blk(SSD_INNER, chunk_idx)]
                 + [s for _, s in resident],
        out_specs=blk(SSD_INNER, out_idx),
        out_shape=jax.ShapeDtypeStruct((batch * seq, SSD_INNER), BF16),
        scratch_shapes=[pltpu.VMEM((nc, t, SSD_INNER), F32), state, state,
                        pltpu.VMEM((nc, t, LANES), BF16)],
        compiler_params=pltpu.CompilerParams(
            dimension_semantics=("arbitrary", "arbitrary", "arbitrary"),
            vmem_limit_bytes=40 * 1024 * 1024),
        name="ssd",
    )(xbc, dt, z, *[a for a, _ in resident])


def _swa_body(sink_ref, q_ref, kp_ref, kc_ref, kn_ref, vp_ref, vc_ref, vn_ref, nw_ref, o_ref,
              *, layer):
    assert SWA_WINDOW == SWA_BLOCK
    j = pl.program_id(1)
    blk = SWA_BLOCK
    n_blk = q_ref.shape[0] // blk
    kb = jnp.concatenate([kp_ref[...], kc_ref[...], kn_ref[...]], axis=0)
    vb = jnp.concatenate([vp_ref[...], vc_ref[...], vn_ref[...]], axis=0)
    qi = lax.broadcasted_iota(jnp.int32, (blk, blk), 0)
    kj = lax.broadcasted_iota(jnp.int32, (blk, blk), 1)
    lo = _lane_iota((blk, LANES)) < HALF
    lo_v = _lane_iota(vb.shape) < HALF
    zero_q = jnp.zeros((blk, LANES), BF16)
    lane_v = _lane_iota(vb.shape)
    unit = lambda at: jnp.where(lane_v == at, 1.0, 0.0).astype(BF16)
    v_half = (jnp.where(lo_v, vb, unit(HALF)), jnp.where(lo_v, unit(0), vb))
    first = jnp.where(j == 0, blk, 0)
    last = jnp.where(j == pl.num_programs(1) - 1, blk, 0)
    items = [(t, g, half) for t in range(n_blk) for g in range(SWA_WIDTH // LANES)
             for half in range(2)]
    logits = {}
    for t, g, half in items:
        qg = q_ref[t * blk:(t + 1) * blk, g * LANES:(g + 1) * LANES]
        qm = jnp.where(lo, qg, zero_q) if half == 0 else jnp.where(lo, zero_q, qg)
        logits[t, g, half] = _dot_nt(qm, kb[t * blk:(t + 3) * blk])
    scaled = {}
    for t, g, half in items:
        keep_prev = kj >= (qi + first if t == 0 else qi)
        keep_next = kj <= (qi - last if t == n_blk - 1 else qi)
        s = logits[t, g, half]
        s_prev = jnp.where(keep_prev, s[:, :blk], NEG_INF)
        s_own = s[:, blk:2 * blk]
        s_next = jnp.where(keep_next, s[:, 2 * blk:], NEG_INF)
        sk = sink_ref[layer, g + 2 * half] * LOG2E
        m = jnp.max(jnp.maximum(jnp.maximum(s_prev, s_own), s_next), axis=-1, keepdims=True)
        m = jnp.maximum(m, sk)
        pexp = jnp.concatenate([jnp.exp2(v - m) for v in (s_prev, s_own, s_next)],
                               axis=-1).astype(BF16)
        acc = _dot(pexp, v_half[half][t * blk:(t + 3) * blk])
        ones_lane = HALF if half == 0 else 0
        den = acc[:, ones_lane:ones_lane + 1] + jnp.exp2(sk - m)
        scaled[t, g, half] = acc / den
    for t in range(n_blk):
        y = jnp.concatenate([jnp.where(lo, scaled[t, g, 0], scaled[t, g, 1])
                             for g in range(SWA_WIDTH // LANES)], axis=-1)
        o_ref[t * blk:(t + 1) * blk, :] = _rms(y, nw_ref[...]).astype(BF16)


def _swa(q, k, v, sink, nw, batch, seq, n_blk):
    blk = SWA_BLOCK
    nb = seq // blk
    ns = nb // n_blk
    cur = lambda b, j: (b * ns + j, 0)
    prv = lambda b, j: (b * nb + jnp.maximum(j * n_blk - 1, 0), 0)
    nxt = lambda b, j: (b * nb + jnp.minimum((j + 1) * n_blk, nb - 1), 0)
    edge = lambda f: pl.BlockSpec((blk, LANES), f)
    main = pl.BlockSpec((n_blk * blk, LANES), cur)
    sinks, layer = sink
    nw_arr, nw_spec = _resident(nw)
    return pl.pallas_call(
        functools.partial(_swa_body, layer=layer),
        grid=(batch, ns),
        in_specs=[pl.BlockSpec(memory_space=pltpu.SMEM),
                  pl.BlockSpec((n_blk * blk, SWA_WIDTH), cur),
                  edge(prv), main, edge(nxt), edge(prv), main, edge(nxt), nw_spec],
        out_specs=pl.BlockSpec((n_blk * blk, SWA_WIDTH), cur),
        out_shape=jax.ShapeDtypeStruct((batch * seq, SWA_WIDTH), BF16),
        compiler_params=pltpu.CompilerParams(dimension_semantics=("parallel", "parallel")),
        name="swa",
    )(sinks, q, k, k, k, v, v, v, nw_arr)


def _mla_body(bound_ref, q_ref, k_ref, v_ref, nw_ref, o_ref, *, key_chunk, layer):
    tq = q_ref.shape[0]
    seq = k_ref.shape[0]
    heads = [slice(h * LANES, (h + 1) * LANES) for h in range(MLA_HEADS)]
    chunks = [slice(c * key_chunk, (c + 1) * key_chunk) for c in range(seq // key_chunk)]
    lo = _lane_iota((tq, LANES)) < HALF
    bound = bound_ref[layer]
    use_bound = bound <= MLA_BOUND_LIMIT

    def denom(acc, h):
        return acc[:, HALF:HALF + 1] if h % 2 == 0 else acc[:, 0:1]

    def finish(accs, dens):
        outs = [jnp.where(lo, accs[2 * p] / dens[2 * p], accs[2 * p + 1] / dens[2 * p + 1])
                for p in range(MLA_HEADS // 2)]
        y = jnp.concatenate(outs, axis=-1)
        o_ref[...] = _rms(y, nw_ref[...]).astype(BF16)

    @pl.when(use_bound)
    def _bounded():
        accs = []
        for h, sl in enumerate(heads):
            qh = q_ref[:, sl]
            acc = None
            for ks in chunks:
                pexp = jnp.exp2(_dot_nt(qh, k_ref[ks, sl]) - bound).astype(BF16)
                part = _dot(pexp, v_ref[ks, sl])
                acc = part if acc is None else acc + part
            accs.append(acc)
        finish(accs, [denom(a, h) for h, a in enumerate(accs)])

    @pl.when(jnp.logical_not(use_bound))
    def _online():
        accs, dens = [], []
        for h, sl in enumerate(heads):
            qh = q_ref[:, sl]
            m = jnp.full((tq, 1), NEG_INF, F32)
            acc = jnp.zeros((tq, LANES), F32)
            for ks in chunks:
                s = _dot_nt(qh, k_ref[ks, sl])
                m_new = jnp.maximum(m, jnp.max(s, axis=-1, keepdims=True))
                pexp = jnp.exp2(s - m_new).astype(BF16)
                acc = jnp.exp2(m - m_new) * acc + _dot(pexp, v_ref[ks, sl])
                m = m_new
            accs.append(acc)
            dens.append(denom(acc, h))
        finish(accs, dens)


def _mla(q, k, v, bound, nw, batch, seq, tq, key_chunk):
    nq = seq // tq
    w = MLA_HEADS * LANES
    full = pl.BlockSpec((seq, w), lambda b, i: (b, 0))
    bounds, layer = bound
    nw, nw_spec = _resident(nw)
    return pl.pallas_call(
        functools.partial(_mla_body, key_chunk=key_chunk, layer=layer),
        grid=(batch, nq),
        in_specs=[pl.BlockSpec(memory_space=pltpu.SMEM),
                  pl.BlockSpec((tq, w), lambda b, i: (b * nq + i, 0)), full, full, nw_spec],
        out_specs=pl.BlockSpec((tq, MLA_WIDTH), lambda b, i: (b * nq + i, 0)),
        out_shape=jax.ShapeDtypeStruct((batch * seq, MLA_WIDTH), BF16),
        compiler_params=pltpu.CompilerParams(
            dimension_semantics=("parallel", "parallel"),
            vmem_limit_bytes=48 * 1024 * 1024),
        name="mla",
    )(bounds, q, k, v, nw)


def _rope_tables(seq):
    def angles(dim):
        inv = 1.0 / np.power(ROPE_THETA, np.arange(0, dim, 2, dtype=np.float64) / dim)
        return np.arange(seq, dtype=np.float64)[:, None] * inv[None, :]

    a64 = angles(SWA_HEAD_DIM)
    c, s = np.cos(a64), np.sin(a64)
    zero = np.zeros_like(c)
    c64 = np.concatenate([c, c, c, c], axis=-1)
    sa64 = np.concatenate([-s, zero, -s, zero], axis=-1)
    sb64 = np.concatenate([zero, s, zero, s], axis=-1)
    a32 = angles(MLA_ROPE)
    c, s = np.cos(a32), np.sin(a32)
    zero = np.zeros_like(c)
    ones = np.ones((seq, MLA_NOPE))
    pad = np.zeros((seq, LANES - MLA_QK))
    zn = np.zeros((seq, MLA_NOPE))
    cm = np.concatenate([ones, c, c, pad], axis=-1)
    sam = np.concatenate([zn, -s, zero, pad], axis=-1)
    sbm = np.concatenate([zn, zero, s, pad], axis=-1)
    return jnp.asarray(np.stack([c64, sa64, sb64, cm, sam, sbm]).astype(np.float32))


def _swa_head_perm(t, axis):
    parts = jnp.split(t, SWA_HEADS, axis=axis)
    return jnp.concatenate([parts[0], parts[2], parts[1], parts[3]], axis=axis)


def _prep_params(ffn1_norm, mix_norm, w_in, ssd_conv_w, ssd_conv_b, ssd_dt_bias, ssd_a_log,
                 ssd_d, ssd_norm, swa_q_norm, swa_k_norm, swa_sink, swa_out_norm,
                 mla_q_lat_norm, mla_w_uq, mla_kv_norm, mla_w_ukv, mla_q_norm, mla_k_norm,
                 mla_out_norm, w_out, ffn2_norm):
    nl, d = w_in.shape[:2]
    row = lambda v: v.reshape(nl, 1, -1).astype(F32)
    pad_lanes = lambda v, n: jnp.pad(v, [(0, 0)] * (v.ndim - 1) + [(0, n - v.shape[-1])])

    w_in_p = _permute_w_in(jnp.swapaxes(w_in, 1, 2))

    wuq = mla_w_uq.reshape(nl, MLA_Q_RANK, MLA_HEADS, MLA_QK)
    wuq = pad_lanes(wuq, LANES).reshape(nl, MLA_Q_RANK, MLA_HEADS * LANES).astype(BF16)
    wukv = mla_w_ukv.reshape(nl, MLA_KV_RANK, MLA_HEADS, MLA_NOPE + MLA_V)
    knope = pad_lanes(wukv[..., :MLA_NOPE], LANES)
    vv = wukv[..., MLA_NOPE:]
    zv = jnp.zeros_like(vv)
    even = (np.arange(MLA_HEADS) % 2 == 0)[None, None, :, None]
    vpad = jnp.concatenate([jnp.where(even, vv, zv), jnp.where(even, zv, vv)], axis=-1)
    wukv_p = jnp.concatenate([knope.reshape(nl, MLA_KV_RANK, -1),
                              vpad.reshape(nl, MLA_KV_RANK, -1)], axis=-1).astype(BF16)

    a, b = SSD_INNER, SSD_INNER + SWA_WIDTH
    w_out_p = jnp.concatenate([w_out[:, :a], _swa_head_perm(w_out[:, a:b], 1), w_out[:, b:]],
                              axis=1).astype(BF16)

    peak = lambda g: jnp.max(jnp.abs(g.astype(F32)), axis=-1)
    mla_bound = (math.sqrt(MLA_QK) * LOG2E * MLA_BOUND_SLACK) * peak(mla_q_norm) * peak(mla_k_norm)

    two = lambda v: jnp.concatenate([v, v], axis=-1)
    return {
        "mla_bound": mla_bound,
        "ffn1_norm": row(ffn1_norm),
        "ffn2_norm": row(ffn2_norm),
        "w_in": w_in_p,
        "mix_norm": row(mix_norm),
        "conv_w": ssd_conv_w.astype(F32),
        "conv_b": row(ssd_conv_b),
        "dt_bias": pad_lanes(row(ssd_dt_bias), LANES),
        "swa_q_gain": row(two(swa_q_norm)),
        "swa_k_gain": row(two(swa_k_norm)),
        "mla_qlat_norm": row(mla_q_lat_norm),
        "w_uq": wuq,
        "mla_kv_norm": row(mla_kv_norm),
        "w_ukv": wukv_p,
        "mla_q_gain": pad_lanes(row(mla_q_norm), LANES),
        "mla_k_gain": pad_lanes(row(mla_k_norm), LANES),
        "ssd_a_log": pad_lanes(row(ssd_a_log), LANES),
        "ssd_d": row(jnp.repeat(ssd_d, SSD_HEAD_DIM, axis=-1)),
        "ssd_norm": row(ssd_norm),
        "swa_sink": swa_sink.astype(F32),
        "swa_out_norm": row(_swa_head_perm(swa_out_norm, 1)),
        "mla_out_norm": row(mla_out_norm),
        "w_out": w_out_p,
    }


def _tiles(seq):
    pick = lambda pref: math.gcd(seq, pref)
    return {"ffn": pick(1024), "proj": pick(512), "mla_q": pick(256), "mla_k": pick(1024),
            "swa_blocks": pick(16 * SWA_BLOCK) // SWA_BLOCK,
            "ssd_chunks": pick(4 * SSD_CHUNK) // SSD_CHUNK}


def kernel(x, ffn1_norm, ffn1_gate, ffn1_up, ffn1_down, mix_norm, w_in, ssd_conv_w, ssd_conv_b, ssd_dt_bias, ssd_a_log, ssd_d, ssd_norm, swa_q_norm, swa_k_norm, swa_sink, swa_out_norm, mla_q_lat_norm, mla_w_uq, mla_kv_norm, mla_w_ukv, mla_q_norm, mla_k_norm, mla_out_norm, w_out, ffn2_norm, ffn2_gate, ffn2_up, ffn2_down):
    batch, seq, d = x.shape
    assert seq % SSD_CHUNK == 0 and seq % SWA_BLOCK == 0
    assert ffn1_gate.shape[-1] % FFN_CHUNK == 0
    stacked = _prep_params(ffn1_norm, mix_norm, w_in, ssd_conv_w, ssd_conv_b, ssd_dt_bias,
                           ssd_a_log, ssd_d, ssd_norm, swa_q_norm, swa_k_norm, swa_sink,
                           swa_out_norm, mla_q_lat_norm, mla_w_uq, mla_kv_norm, mla_w_ukv,
                           mla_q_norm, mla_k_norm, mla_out_norm, w_out, ffn2_norm)
    tl = _tiles(seq)
    tabs = _rope_tables(seq)
    xf = x.reshape(batch * seq, d).astype(F32)
    ffn1_stacked = _cast_ffn_weights(ffn1_gate, ffn1_up, ffn1_down)
    ffn2_stacked = _cast_ffn_weights(ffn2_gate, ffn2_up, ffn2_down)
    for l in range(w_in.shape[0]):
        p = {k: (v, l) for k, v in stacked.items()}
        ffn1_w = [(w, l) for w in ffn1_stacked]
        ffn2_w = [(w, l) for w in ffn2_stacked]
        xf = _ffn(xf, p["ffn1_norm"], *ffn1_w, tm=tl["ffn"])
        z, xbc, dt, swq, swk, swv, mq, mk, mv = _in_proj(xf, tabs, p, seq, tl["proj"])
        y_ssd = _ssd(xbc, dt, z, p, batch, seq, tl["ssd_chunks"])
        y_swa = _swa(swq, swk, swv, p["swa_sink"], p["swa_out_norm"], batch, seq,
                     tl["swa_blocks"])
        y_mla = _mla(mq, mk, mv, p["mla_bound"], p["mla_out_norm"], batch, seq, tl["mla_q"],
                     tl["mla_k"])
        xf = _ffn(xf, p["ffn2_norm"], *ffn2_w, tm=tl["ffn"],
                  mixer=(y_ssd, y_swa, y_mla, p["w_out"]))
    return xf.reshape(batch, seq, d).astype(x.dtype)
```

```python
import functools
import math

import jax
import jax.numpy as jnp
import numpy as np
from jax import lax
from jax.experimental import pallas as pl
from jax.experimental.pallas import tpu as pltpu

F32 = jnp.float32
BF16 = jnp.bfloat16

LANES = 128
HALF = LANES // 2

SSD_HEADS = 8
SSD_HEAD_DIM = 64
SSD_INNER = SSD_HEADS * SSD_HEAD_DIM
SSD_GROUPS = 2
SSD_STATE = 128
SSD_CONV = 5
SSD_CHUNK = 256
SSD_CONV_DIM = SSD_INNER + 2 * SSD_GROUPS * SSD_STATE
SWA_HEADS = 4
SWA_KV_HEADS = 2
SWA_HEAD_DIM = 64
SWA_WINDOW = 128
SWA_BLOCK = 128
SWA_WIDTH = SWA_HEADS * SWA_HEAD_DIM
MLA_HEADS = 4
MLA_Q_RANK = 256
MLA_KV_RANK = 128
MLA_NOPE = 64
MLA_ROPE = 32
MLA_QK = MLA_NOPE + MLA_ROPE
MLA_V = 64
MLA_WIDTH = MLA_HEADS * MLA_V
ROPE_THETA = 10000.0
EPS = 1e-6

C_Z = 0
C_XBC = C_Z + SSD_INNER
C_SWQ = C_XBC + SSD_CONV_DIM
C_SWK = C_SWQ + SWA_WIDTH
C_SWV = C_SWK + LANES
C_MLQ = C_SWV + LANES
C_CKV = C_MLQ + MLA_Q_RANK
C_MISC = C_CKV + MLA_KV_RANK
C_END = C_MISC + LANES
CONV_HALO = 16
FFN_CHUNK = 256
FFN_RESIDUAL_SCALE = 0.5
NEG_INF = float("-inf")
LOG_FLOOR = -1e30
LOG2E = math.log2(math.e)
MLA_BOUND_LIMIT = 48.0
MLA_BOUND_SLACK = 1.02


def _rms(x, w):
    ms = jnp.mean(x * x, axis=-1, keepdims=True)
    return x * lax.rsqrt(ms + EPS) * w


def _silu(x):
    return x / (1.0 + jnp.exp(-x))


def _dot(a, b):
    return jnp.dot(a, b, preferred_element_type=F32)


def _dot_nt(a, b):
    return lax.dot_general(a, b, (((1,), (1,)), ((), ())), preferred_element_type=F32)


def _dot_tn(a, b):
    return lax.dot_general(a, b, (((0,), (0,)), ((), ())), preferred_element_type=F32)


def _lane_iota(shape):
    return lax.broadcasted_iota(jnp.int32, shape, len(shape) - 1)


def _resident(param, single_buffer=False):
    mode = {"pipeline_mode": pl.Buffered(1)} if single_buffer else {}
    if isinstance(param, tuple):
        arr, layer = param
        rest = (0,) * (arr.ndim - 1)
        return arr, pl.BlockSpec((None,) + arr.shape[1:], lambda *_: (layer,) + rest, **mode)
    zeros = (0,) * param.ndim
    return param, pl.BlockSpec(param.shape, lambda *_: zeros, **mode)


def _load_ffn_weights(layer, gate_hbm, up_hbm, down_hbm, wg_ref, wu_ref, wd_ref,
                      col_stage, row_stage, sems):
    jobs = []
    for c in range(wg_ref.shape[1] // FFN_CHUNK):
        lo, hi = c * FFN_CHUNK, (c + 1) * FFN_CHUNK
        jobs.append((gate_hbm.at[layer, :, lo:hi], col_stage, wg_ref, (slice(None), slice(lo, hi)), 1.0))
        jobs.append((up_hbm.at[layer, :, lo:hi], col_stage, wu_ref, (slice(None), slice(lo, hi)), 1.0))
        jobs.append((down_hbm.at[layer, lo:hi, :], row_stage, wd_ref, (slice(lo, hi), slice(None)),
                     FFN_RESIDUAL_SCALE))

    def copy(j):
        src, stage = jobs[j][:2]
        return pltpu.make_async_copy(src, stage.at[j % 2], sems.at[j % 2])

    copy(0).start()
    for j, (_, stage, dst, where, scale) in enumerate(jobs):
        if j + 1 < len(jobs):
            copy(j + 1).start()
        copy(j).wait()
        piece = stage[j % 2]
        dst[where] = (piece if scale == 1.0 else piece * scale).astype(BF16)


def _ffn_body(*refs, fuse_out_proj, layer):
    if fuse_out_proj:
        x_ref, ys_ref, yw_ref, ym_ref, wo_ref = refs[:5]
        refs = refs[5:]
    else:
        x_ref = refs[0]
        refs = refs[1:]
    (nw_ref, gate_hbm, up_hbm, down_hbm, o_ref,
     xn_ref, wg_ref, wu_ref, wd_ref, col_stage, row_stage, sems) = refs

    @pl.when(pl.program_id(0) == 0)
    def _():
        _load_ffn_weights(layer, gate_hbm, up_hbm, down_hbm, wg_ref, wu_ref, wd_ref,
                          col_stage, row_stage, sems)

    x = x_ref[...]
    if fuse_out_proj:
        a, b = SSD_INNER, SSD_INNER + SWA_WIDTH
        x = (x + _dot(ys_ref[...], wo_ref[0:a, :]) + _dot(yw_ref[...], wo_ref[a:b, :])
             + _dot(ym_ref[...], wo_ref[b:, :]))
    o_ref[...] = x
    xn_ref[...] = _rms(x, nw_ref[...]).astype(BF16)
    for c in range(wg_ref.shape[1] // FFN_CHUNK):
        cols = slice(c * FFN_CHUNK, (c + 1) * FFN_CHUNK)
        xn = xn_ref[...]
        g = _dot(xn, wg_ref[:, cols])
        u = _dot(xn, wu_ref[:, cols])
        h = (_silu(g) * u).astype(BF16)
        o_ref[...] += _dot(h, wd_ref[cols, :])


def _ffn(x, nw, layer, gate, up, down, tm, mixer=None):
    n, d = x.shape
    f = gate.shape[-1]
    row = lambda w: pl.BlockSpec((tm, w), lambda i: (i, 0))
    args, specs = [x], [row(d)]
    resident = [_resident(nw)]
    if mixer is not None:
        args += list(mixer[:3])
        specs += [row(a.shape[1]) for a in mixer[:3]]
        resident.insert(0, _resident(mixer[3]))
    args += [a for a, _ in resident] + [gate, up, down]
    specs += [s for _, s in resident] + [pl.BlockSpec(memory_space=pl.ANY)] * 3
    return pl.pallas_call(
        functools.partial(_ffn_body, fuse_out_proj=mixer is not None, layer=layer),
        grid=(n // tm,),
        in_specs=specs,
        out_specs=row(d),
        out_shape=jax.ShapeDtypeStruct((n, d), F32),
        scratch_shapes=[pltpu.VMEM((tm, d), BF16),
                        pltpu.VMEM((d, f), BF16), pltpu.VMEM((d, f), BF16),
                        pltpu.VMEM((f, d), BF16),
                        pltpu.VMEM((2, d, FFN_CHUNK), F32), pltpu.VMEM((2, FFN_CHUNK, d), F32),
                        pltpu.SemaphoreType.DMA((2,))],
        compiler_params=pltpu.CompilerParams(
            dimension_semantics=("arbitrary",), vmem_limit_bytes=56 * 1024 * 1024),
        name="ffn",
    )(*args)


def _permute_w_in_body(src_ref, dst_ref):
    n_src, cols = src_ref.shape
    hd = SWA_HEAD_DIM
    dt0 = C_SWQ
    q0 = dt0 + 2 * SSD_HEADS
    rest0 = q0 + SWA_WIDTH
    kr0 = n_src - MLA_ROPE
    zeros = lambda n: jnp.zeros((n, cols), F32)
    pieces = [src_ref[0:dt0, :]]
    pieces += [src_ref[q0 + h * hd:q0 + (h + 1) * hd, :] for h in (0, 2, 1, 3)]
    pieces += [src_ref[rest0:kr0, :],
               src_ref[dt0:q0, :], zeros(MLA_NOPE - 2 * SSD_HEADS),
               src_ref[kr0:n_src, :], zeros(LANES - MLA_QK)]
    dst_ref[...] = jnp.concatenate(pieces, axis=0).T.astype(BF16)


def _permute_w_in(w_in_t, col_block=256):
    nl, n_src, d = w_in_t.shape
    assert n_src + (MLA_NOPE - 2 * SSD_HEADS) + (LANES - MLA_QK) == C_END
    return pl.pallas_call(
        _permute_w_in_body,
        grid=(nl, d // col_block),
        in_specs=[pl.BlockSpec((None, n_src, col_block), lambda l, i: (l, 0, i))],
        out_specs=pl.BlockSpec((None, col_block, C_END), lambda l, i: (l, i, 0)),
        out_shape=jax.ShapeDtypeStruct((nl, d, C_END), BF16),
        compiler_params=pltpu.CompilerParams(dimension_semantics=("parallel", "parallel")),
        name="permute_w_in",
    )(w_in_t)


def _rope(y, c, sa, sb, shift):
    return (y * c + pltpu.roll(y, LANES - shift, 1) * sa + pltpu.roll(y, shift, 1) * sb)


def _head_sum_matrix(width):
    r = lax.broadcasted_iota(jnp.int32, (LANES, LANES), 0)
    c = lax.broadcasted_iota(jnp.int32, (LANES, LANES), 1)
    same = (r < width) if width > HALF else ((r < HALF) == (c < HALF))
    return jnp.where(same, 1.0, 0.0).astype(BF16)


def _head_norm_rope(x, sumsq, gain, c, sa, sb, width, rope_dim, post_scale):
    y = x * lax.rsqrt(sumsq / float(width) + EPS) * gain
    out = _rope(y, c, sa, sb, rope_dim // 2)
    return out if post_scale == 1.0 else out * post_scale


def _inproj_body(x_ref, xp_ref, xnx_ref, tab_ref, w_ref, mixw_ref, convw_ref, convb_ref,
                 dtb_ref, swqg_ref, swkg_ref, qlw_ref, wuq_ref, kvw_ref, wukv_ref,
                 mqg_ref, mkg_ref,
                 z_ref, xbc_ref, dt_ref, swq_ref, swk_ref, swv_ref, mq_ref, mk_ref, mv_ref,
                 *, tiles_per_seq):
    i = pl.program_id(0)
    tm = x_ref.shape[0]
    pos_tile = i % tiles_per_seq
    mixw = mixw_ref[...]
    hn = _rms(x_ref[...], mixw).astype(BF16)
    hp = _rms(xp_ref[...], mixw).astype(BF16)
    hx = _rms(xnx_ref[...], mixw).astype(BF16)
    groups = lambda v: [v[:, g * LANES:(g + 1) * LANES] for g in range(v.shape[1] // LANES)]

    z = _dot(hn, w_ref[:, C_Z:C_XBC])
    conv_w = 2 * LANES
    xbc_parts = []
    for c0 in range(C_XBC, C_SWQ, conv_w):
        wx = w_ref[:, c0:c0 + conv_w]
        xbc_parts.append((_dot(hp, wx), _dot(hn, wx), _dot(hx, wx)))
    ckv, misc = groups(_dot(hn, w_ref[:, C_CKV:C_END]))
    swa_q = groups(_dot(hn, w_ref[:, C_SWQ:C_SWK]))
    swa_k, swa_v = groups(_dot(hn, w_ref[:, C_SWK:C_MLQ]))
    ql = _dot(hn, w_ref[:, C_MLQ:C_CKV])

    q_up = groups(_dot(_rms(ql, qlw_ref[...]).astype(BF16), wuq_ref[...]))
    kv_up = groups(_dot(_rms(ckv, kvw_ref[...]).astype(BF16), wukv_ref[...]))
    lane = _lane_iota(misc.shape)
    kr = jnp.where((lane >= MLA_NOPE) & (lane < MLA_QK), misc, 0.0)
    mla_q = q_up
    mla_k = [k_nope + kr for k_nope in kv_up[:MLA_HEADS]]
    mla_v = kv_up[MLA_HEADS:]

    sum64, sum96 = _head_sum_matrix(SWA_HEAD_DIM), _head_sum_matrix(MLA_QK)
    sumsq = lambda xs, m: [_dot((v * v).astype(BF16), m) for v in xs]
    ss_swa_q, ss_swa_k = sumsq(swa_q, sum64), sumsq([swa_k], sum64)
    ss_mla_q, ss_mla_k = sumsq(mla_q, sum96), sumsq(mla_k, sum96)

    z_ref[...] = z.astype(BF16)

    keep_prev = (pos_tile > 0).astype(F32)
    keep_next = (pos_tile < tiles_per_seq - 1).astype(F32)
    for part, (xbc_prev, xbc_main, xbc_next) in enumerate(xbc_parts):
        cs = slice(part * conv_w, (part + 1) * conv_w)
        padded = jnp.concatenate([xbc_prev * keep_prev, xbc_main, xbc_next * keep_next], axis=0)
        n_pad = padded.shape[0]
        conv = jnp.broadcast_to(convb_ref[:, cs], (tm, conv_w))
        for k in range(SSD_CONV):
            shift = SSD_CONV // 2 - k
            tap = padded if shift == 0 else pltpu.roll(padded, shift % n_pad, 0)
            conv = conv + convw_ref[k:k + 1, cs] * tap[CONV_HALO:CONV_HALO + tm, :]
        xbc_ref[:, cs] = _silu(conv).astype(BF16)

    dtv = misc + dtb_ref[...]
    dt_ref[...] = jnp.maximum(dtv, 0.0) + jnp.log1p(jnp.exp(-jnp.abs(dtv)))

    swa_rope = (tab_ref[0], tab_ref[1], tab_ref[2], SWA_HEAD_DIM, SWA_HEAD_DIM)
    for g, (v, ss) in enumerate(zip(swa_q, ss_swa_q)):
        swq_ref[:, g * LANES:(g + 1) * LANES] = _head_norm_rope(
            v, ss, swqg_ref[...], *swa_rope, SWA_HEAD_DIM ** -0.5 * LOG2E).astype(BF16)
    swk_ref[...] = _head_norm_rope(swa_k, ss_swa_k[0], swkg_ref[...], *swa_rope, 1.0).astype(BF16)
    swv_ref[...] = swa_v.astype(BF16)

    mla_rope = (tab_ref[3], tab_ref[4], tab_ref[5], MLA_QK, MLA_ROPE)
    for h in range(MLA_HEADS):
        sl = slice(h * LANES, (h + 1) * LANES)
        mq_ref[:, sl] = _head_norm_rope(mla_q[h], ss_mla_q[h], mqg_ref[...], *mla_rope,
                                        MLA_QK ** -0.5 * LOG2E).astype(BF16)
        mk_ref[:, sl] = _head_norm_rope(mla_k[h], ss_mla_k[h], mkg_ref[...], *mla_rope,
                                        1.0).astype(BF16)
        ones_lane = HALF if h % 2 == 0 else 0
        mv_ref[:, sl] = jnp.where(lane == ones_lane, 1.0, mla_v[h]).astype(BF16)


def _in_proj(x, tabs, p, seq, tm):
    n, d = x.shape
    tiles_per_seq = seq // tm
    halo_per_tile = tm // CONV_HALO
    n_halo = n // CONV_HALO
    row = lambda w: pl.BlockSpec((tm, w), lambda i: (i, 0))
    in_specs = [
        row(d),
        pl.BlockSpec((CONV_HALO, d), lambda i: (jnp.maximum(i * halo_per_tile - 1, 0), 0)),
        pl.BlockSpec((CONV_HALO, d),
                     lambda i: (jnp.minimum((i + 1) * halo_per_tile, n_halo - 1), 0)),
        pl.BlockSpec((6, tm, LANES), lambda i: (0, i % tiles_per_seq, 0)),
    ]
    names = ("w_in", "mix_norm", "conv_w", "conv_b", "dt_bias", "swa_q_gain", "swa_k_gain",
             "mla_qlat_norm", "w_uq", "mla_kv_norm", "w_ukv", "mla_q_gain", "mla_k_gain")
    resident = [_resident(p[k]) for k in names]
    args = [a for a, _ in resident]
    in_specs += [s for _, s in resident]
    widths = (SSD_INNER, SSD_CONV_DIM, LANES, SWA_WIDTH, LANES, LANES,
              MLA_HEADS * LANES, MLA_HEADS * LANES, MLA_HEADS * LANES)
    dts = (BF16, BF16, F32, BF16, BF16, BF16, BF16, BF16, BF16)
    return pl.pallas_call(
        functools.partial(_inproj_body, tiles_per_seq=tiles_per_seq),
        grid=(n // tm,),
        in_specs=in_specs,
        out_specs=[row(w) for w in widths],
        out_shape=[jax.ShapeDtypeStruct((n, w), t) for w, t in zip(widths, dts)],
        compiler_params=pltpu.CompilerParams(
            dimension_semantics=("parallel",), vmem_limit_bytes=48 * 1024 * 1024),
        name="in_proj",
    )(x, x, x, tabs, *args)


def _split_bf16(v, n):
    pieces = []
    for _ in range(n - 1):
        p = v.astype(BF16)
        pieces.append(p)
        v = v - p.astype(F32)
    pieces.append(v.astype(BF16))
    return pieces


def _pack_lanes(pieces, width):
    lane = _lane_iota(pieces[0].shape)
    out = jnp.zeros(pieces[0].shape, F32)
    for k, p in enumerate(pieces):
        pf = p.astype(F32)
        if k:
            pf = pltpu.roll(pf, k * width, 1)
        out = jnp.where((lane >= k * width) & (lane < (k + 1) * width), pf, out)
    return out.astype(BF16)


def _ssd_expand_matrices():
    nd = 2 * SSD_HEADS
    rows = np.arange(LANES)[:, None]
    head = np.arange(SSD_INNER)[None, :] // SSD_HEAD_DIM
    mats = []
    for direction in range(2):
        for first_piece in (0, 2):
            d = direction * SSD_HEADS + head
            hit = (rows == first_piece * nd + d) | (rows == (first_piece + 1) * nd + d)
            mats.append(hit)
    return jnp.asarray(np.stack(mats).astype(np.float32), BF16)


def _ssd_body(xbc_ref, dt_ref, z_ref, alog_ref, dskip_ref, nw_ref, expand_ref, o_ref,
              yacc_ref, sf_ref, sb_ref, cols_ref, *, n_chunks):
    pas = pl.program_id(1)
    j = pl.program_id(2)
    t = SSD_CHUNK
    cps = xbc_ref.shape[0] // t
    nh = SSD_HEADS

    gw = SSD_INNER // SSD_GROUPS
    hpg = nh // SSD_GROUPS
    nd = 2 * nh
    lane = _lane_iota((t, LANES))

    def b_of(rows, g):
        return xbc_ref[rows, SSD_INNER + g * SSD_STATE:SSD_INNER + (g + 1) * SSD_STATE]

    def c_of(rows, g):
        return xbc_ref[rows, SSD_INNER + (SSD_GROUPS + g) * SSD_STATE:
                       SSD_INNER + (SSD_GROUPS + g + 1) * SSD_STATE]

    def forward_early(k):
        rows = slice(k * t, (k + 1) * t)
        fwd = lane < nh
        dt = dt_ref[rows, :]
        avec = jnp.where(_lane_iota((1, LANES)) < nd, -jnp.exp(alog_ref[...]), 0.0)
        a = dt * avec
        row = lax.broadcasted_iota(jnp.int32, (t, t), 0)
        col = lax.broadcasted_iota(jnp.int32, (t, t), 1)
        tri = jnp.where(col <= row, 1.0, 0.0).astype(BF16)
        cum = sum(_dot(tri, part) for part in _split_bf16(a, 3))
        excl = cum - a
        tot = cum[t - 1:t, :]

        e_in = jnp.exp(jnp.where(fwd, cum, tot - excl))
        w_st = jnp.exp(jnp.where(fwd, tot - cum, excl)) * dt
        e1, e2 = _split_bf16(e_in, 2)
        w1, w2 = _split_bf16(w_st, 2)
        cols = _pack_lanes([e1, e2, w1, w2], nd)
        cols_ref[j * cps + k] = cols

        r = jnp.where(fwd, cum, -excl) * LOG2E
        cc = r - jnp.maximum(jnp.log(dt), LOG_FLOOR) * LOG2E
        ones = jnp.ones((t, LANES), F32)
        lhs = jnp.where(lane < 3 * nd, _pack_lanes(_split_bf16(r, 3), nd),
                        jnp.where(lane < 6 * nd, ones, 0.0).astype(BF16))
        cc_t = cc.T[0:nd, :]
        c1, c2, c3 = _split_bf16(cc_t, 3)
        rhs = jnp.concatenate(
            [jnp.ones((3 * nd, t), BF16), -c1, -c2, -c3, jnp.zeros((LANES - 6 * nd, t), BF16)],
            axis=0)
        sub = lax.broadcasted_iota(jnp.int32, (LANES, t), 0) % nd

        e_x = _dot(cols, expand_ref[0])
        w_x = _dot(cols, expand_ref[1])
        cbs = [_dot_nt(c_of(rows, g), b_of(rows, g)) for g in range(SSD_GROUPS)]
        ys = []
        for g in range(SSD_GROUPS):
            gs = slice(g * gw, (g + 1) * gw)
            s_in = sf_ref[g]
            xgf = xbc_ref[rows, gs].astype(F32)
            ys.append(dskip_ref[:, gs] * xgf + _dot(c_of(rows, g), s_in.astype(BF16)) * e_x[:, gs])
            xw = (xgf * w_x[:, gs]).astype(BF16)
            sf_ref[g] = s_in * e_x[t - 1:t, gs] + _dot_tn(b_of(rows, g), xw)
        exps = [_dot(lhs, jnp.where(sub == idx, rhs, jnp.zeros_like(rhs))) for idx in range(nd)]
        return cbs, ys, exps

    def forward_late(k, cbs, ys, exps):
        rows = slice(k * t, (k + 1) * t)
        th = t // 2
        quad = lambda v, qi, qj: v[qi * th:(qi + 1) * th, qj * th:(qj + 1) * th]
        qrow = lax.broadcasted_iota(jnp.int32, (th, th), 0)
        qcol = lax.broadcasted_iota(jnp.int32, (th, th), 1)
        on_low = qrow >= qcol
        on_up = qcol >= qrow

        def mixing(h, cb):
            df, db = exps[h], exps[nh + h]
            diag = [quad(cb, q, q) * (jnp.exp2(jnp.where(on_low, quad(df, q, q), NEG_INF))
                                      + jnp.exp2(jnp.where(on_up, quad(db, q, q), NEG_INF)))
                    for q in range(2)]
            upper = quad(cb, 0, 1) * jnp.exp2(quad(db, 0, 1))
            lower = quad(cb, 1, 0) * jnp.exp2(quad(df, 1, 0))
            return jnp.concatenate([jnp.concatenate([diag[0], upper], axis=1),
                                    jnp.concatenate([lower, diag[1]], axis=1)], axis=0)

        glane = _lane_iota((t, gw))
        for g in range(SSD_GROUPS):
            gs = slice(g * gw, (g + 1) * gw)
            xg = xbc_ref[rows, gs]
            y = ys[g]
            for hh in range(hpg):
                m = mixing(g * hpg + hh, cbs[g]).astype(BF16)
                mine = (glane >= hh * SSD_HEAD_DIM) & (glane < (hh + 1) * SSD_HEAD_DIM)
                y = y + _dot(m, jnp.where(mine, xg, jnp.zeros_like(xg)))
            yacc_ref[j * cps + k, :, gs] = y

    @pl.when(pas == 0)
    def _forward():
        @pl.when(j == 0)
        def _():
            sf_ref[...] = jnp.zeros_like(sf_ref)

        early = [forward_early(k) for k in range(cps)]
        for k in range(cps):
            forward_late(k, *early[k])

    @pl.when(pas == 1)
    def _backward():
        @pl.when(j == 0)
        def _():
            sb_ref[...] = jnp.zeros_like(sb_ref)

        for k in reversed(range(cps)):
            rows = slice(k * t, (k + 1) * t)
            c = n_chunks - 1 - j * cps - (cps - 1 - k)
            cols = cols_ref[c]
            e_x = _dot(cols, expand_ref[2])
            w_x = _dot(cols, expand_ref[3])
            gate = _silu(z_ref[rows, :].astype(F32))
            for g in range(SSD_GROUPS):
                gs = slice(g * gw, (g + 1) * gw)
                s_in = sb_ref[g]
                y = yacc_ref[c, :, gs] + _dot(c_of(rows, g), s_in.astype(BF16)) * e_x[:, gs]
                o_ref[rows, gs] = _rms(y * gate[:, gs], nw_ref[:, gs]).astype(BF16)
                xw = (xbc_ref[rows, gs].astype(F32) * w_x[:, gs]).astype(BF16)
                sb_ref[g] = s_in * e_x[0:1, gs] + _dot_tn(b_of(rows, g), xw)


def _ssd(xbc, dt, z, p, batch, seq, cps):
    t = SSD_CHUNK
    nc = seq // t
    ns = nc // cps

    def chunk_idx(b, pas, j):
        return b * ns + j + pas * (ns - 1 - 2 * j)

    def out_idx(b, pas, j):
        return b * ns + (ns - 1) - pas * j

    blk = lambda w, f: pl.BlockSpec((cps * t, w), lambda b, pas, j: (f(b, pas, j), 0))
    resident = [_resident(a) for a in (p["ssd_a_log"], p["ssd_d"], p["ssd_norm"],
                                       _ssd_expand_matrices())]
    state = pltpu.VMEM((SSD_GROUPS, SSD_STATE, SSD_INNER // SSD_GROUPS), F32)
    return pl.pallas_call(
        functools.partial(_ssd_body, n_chunks=nc),
        grid=(batch, 2, ns),
        in_specs=[blk(SSD_CONV_DIM, chunk_idx), blk(LANES, chunk_idx), blk(SSD_INNER, chunk_idx)]
                 + [s for _, s in resident],
        out_specs=blk(SSD_INNER, out_idx),
        out_shape=jax.ShapeDtypeStruct((batch * seq, SSD_INNER), BF16),
        scratch_shapes=[pltpu.VMEM((nc, t, SSD_INNER), F32), state, state,
                        pltpu.VMEM((nc, t, LANES), BF16)],
        compiler_params=pltpu.CompilerParams(
            dimension_semantics=("arbitrary", "arbitrary", "arbitrary"),
            vmem_limit_bytes=40 * 1024 * 1024),
        name="ssd",
    )(xbc, dt, z, *[a for a, _ in resident])


def _swa_body(sink_ref, q_ref, kp_ref, kc_ref, kn_ref, vp_ref, vc_ref, vn_ref, nw_ref, o_ref,
              *, layer):
    assert SWA_WINDOW == SWA_BLOCK
    j = pl.program_id(1)
    blk = SWA_BLOCK
    n_blk = q_ref.shape[0] // blk
    kb = jnp.concatenate([kp_ref[...], kc_ref[...], kn_ref[...]], axis=0)
    vb = jnp.concatenate([vp_ref[...], vc_ref[...], vn_ref[...]], axis=0)
    qi = lax.broadcasted_iota(jnp.int32, (blk, blk), 0)
    kj = lax.broadcasted_iota(jnp.int32, (blk, blk), 1)
    lo = _lane_iota((blk, LANES)) < HALF
    lo_v = _lane_iota(vb.shape) < HALF
    zero_q = jnp.zeros((blk, LANES), BF16)
    lane_v = _lane_iota(vb.shape)
    unit = lambda at: jnp.where(lane_v == at, 1.0, 0.0).astype(BF16)
    v_half = (jnp.where(lo_v, vb, unit(HALF)), jnp.where(lo_v, unit(0), vb))
    first = jnp.where(j == 0, blk, 0)
    last = jnp.where(j == pl.num_programs(1) - 1, blk, 0)
    items = [(t, g, half) for t in range(n_blk) for g in range(SWA_WIDTH // LANES)
             for half in range(2)]
    logits = {}
    for t, g, half in items:
        qg = q_ref[t * blk:(t + 1) * blk, g * LANES:(g + 1) * LANES]
        qm = jnp.where(lo, qg, zero_q) if half == 0 else jnp.where(lo, zero_q, qg)
        logits[t, g, half] = _dot_nt(qm, kb[t * blk:(t + 3) * blk])
    scaled = {}
    for t, g, half in items:
        keep_prev = kj >= (qi + first if t == 0 else qi)
        keep_next = kj <= (qi - last if t == n_blk - 1 else qi)
        s = logits[t, g, half]
        s_prev = jnp.where(keep_prev, s[:, :blk], NEG_INF)
        s_own = s[:, blk:2 * blk]
        s_next = jnp.where(keep_next, s[:, 2 * blk:], NEG_INF)
        sk = sink_ref[layer, g + 2 * half] * LOG2E
        m = jnp.max(jnp.maximum(jnp.maximum(s_prev, s_own), s_next), axis=-1, keepdims=True)
        m = jnp.maximum(m, sk)
        pexp = jnp.concatenate([jnp.exp2(v - m) for v in (s_prev, s_own, s_next)],
                               axis=-1).astype(BF16)
        acc = _dot(pexp, v_half[half][t * blk:(t + 3) * blk])
        ones_lane = HALF if half == 0 else 0
        den = acc[:, ones_lane:ones_lane + 1] + jnp.exp2(sk - m)
        scaled[t, g, half] = acc / den
    for t in range(n_blk):
        y = jnp.concatenate([jnp.where(lo, scaled[t, g, 0], scaled[t, g, 1])
                             for g in range(SWA_WIDTH // LANES)], axis=-1)
        o_ref[t * blk:(t + 1) * blk, :] = _rms(y, nw_ref[...]).astype(BF16)


def _swa(q, k, v, sink, nw, batch, seq, n_blk):
    blk = SWA_BLOCK
    nb = seq // blk
    ns = nb // n_blk
    cur = lambda b, j: (b * ns + j, 0)
    prv = lambda b, j: (b * nb + jnp.maximum(j * n_blk - 1, 0), 0)
    nxt = lambda b, j: (b * nb + jnp.minimum((j + 1) * n_blk, nb - 1), 0)
    edge = lambda f: pl.BlockSpec((blk, LANES), f)
    main = pl.BlockSpec((n_blk * blk, LANES), cur)
    sinks, layer = sink
    nw_arr, nw_spec = _resident(nw)
    return pl.pallas_call(
        functools.partial(_swa_body, layer=layer),
        grid=(batch, ns),
        in_specs=[pl.BlockSpec(memory_space=pltpu.SMEM),
                  pl.BlockSpec((n_blk * blk, SWA_WIDTH), cur),
                  edge(prv), main, edge(nxt), edge(prv), main, edge(nxt), nw_spec],
        out_specs=pl.BlockSpec((n_blk * blk, SWA_WIDTH), cur),
        out_shape=jax.ShapeDtypeStruct((batch * seq, SWA_WIDTH), BF16),
        compiler_params=pltpu.CompilerParams(dimension_semantics=("parallel", "parallel")),
        name="swa",
    )(sinks, q, k, k, k, v, v, v, nw_arr)


def _mla_body(bound_ref, q_ref, k_ref, v_ref, nw_ref, o_ref, *, key_chunk, layer):
    tq = q_ref.shape[0]
    seq = k_ref.shape[0]
    heads = [slice(h * LANES, (h + 1) * LANES) for h in range(MLA_HEADS)]
    chunks = [slice(c * key_chunk, (c + 1) * key_chunk) for c in range(seq // key_chunk)]
    lo = _lane_iota((tq, LANES)) < HALF
    bound = bound_ref[layer]
    use_bound = bound <= MLA_BOUND_LIMIT

    def denom(acc, h):
        return acc[:, HALF:HALF + 1] if h % 2 == 0 else acc[:, 0:1]

    def finish(accs, dens):
        outs = [jnp.where(lo, accs[2 * p] / dens[2 * p], accs[2 * p + 1] / dens[2 * p + 1])
                for p in range(MLA_HEADS // 2)]
        y = jnp.concatenate(outs, axis=-1)
        o_ref[...] = _rms(y, nw_ref[...]).astype(BF16)

    @pl.when(use_bound)
    def _bounded():
        accs = []
        for h, sl in enumerate(heads):
            qh = q_ref[:, sl]
            acc = None
            for ks in chunks:
                pexp = jnp.exp2(_dot_nt(qh, k_ref[ks, sl]) - bound).astype(BF16)
                part = _dot(pexp, v_ref[ks, sl])
                acc = part if acc is None else acc + part
            accs.append(acc)
        finish(accs, [denom(a, h) for h, a in enumerate(accs)])

    @pl.when(jnp.logical_not(use_bound))
    def _online():
        accs, dens = [], []
        for h, sl in enumerate(heads):
            qh = q_ref[:, sl]
            m = jnp.full((tq, 1), NEG_INF, F32)
            acc = jnp.zeros((tq, LANES), F32)
            for ks in chunks:
                s = _dot_nt(qh, k_ref[ks, sl])
                m_new = jnp.maximum(m, jnp.max(s, axis=-1, keepdims=True))
                pexp = jnp.exp2(s - m_new).astype(BF16)
                acc = jnp.exp2(m - m_new) * acc + _dot(pexp, v_ref[ks, sl])
                m = m_new
            accs.append(acc)
            dens.append(denom(acc, h))
        finish(accs, dens)


def _mla(q, k, v, bound, nw, batch, seq, tq, key_chunk):
    nq = seq // tq
    w = MLA_HEADS * LANES
    full = pl.BlockSpec((seq, w), lambda b, i: (b, 0))
    bounds, layer = bound
    nw, nw_spec = _resident(nw)
    return pl.pallas_call(
        functools.partial(_mla_body, key_chunk=key_chunk, layer=layer),
        grid=(batch, nq),
        in_specs=[pl.BlockSpec(memory_space=pltpu.SMEM),
                  pl.BlockSpec((tq, w), lambda b, i: (b * nq + i, 0)), full, full, nw_spec],
        out_specs=pl.BlockSpec((tq, MLA_WIDTH), lambda b, i: (b * nq + i, 0)),
        out_shape=jax.ShapeDtypeStruct((batch * seq, MLA_WIDTH), BF16),
        compiler_params=pltpu.CompilerParams(
            dimension_semantics=("parallel", "parallel"),
            vmem_limit_bytes=48 * 1024 * 1024),
        name="mla",
    )(bounds, q, k, v, nw)


def _rope_tables(seq):
    def angles(dim):
        inv = 1.0 / np.power(ROPE_THETA, np.arange(0, dim, 2, dtype=np.float64) / dim)
        return np.arange(seq, dtype=np.float64)[:, None] * inv[None, :]

    a64 = angles(SWA_HEAD_DIM)
    c, s = np.cos(a64), np.sin(a64)
    zero = np.zeros_like(c)
    c64 = np.concatenate([c, c, c, c], axis=-1)
    sa64 = np.concatenate([-s, zero, -s, zero], axis=-1)
    sb64 = np.concatenate([zero, s, zero, s], axis=-1)
    a32 = angles(MLA_ROPE)
    c, s = np.cos(a32), np.sin(a32)
    zero = np.zeros_like(c)
    ones = np.ones((seq, MLA_NOPE))
    pad = np.zeros((seq, LANES - MLA_QK))
    zn = np.zeros((seq, MLA_NOPE))
    cm = np.concatenate([ones, c, c, pad], axis=-1)
    sam = np.concatenate([zn, -s, zero, pad], axis=-1)
    sbm = np.concatenate([zn, zero, s, pad], axis=-1)
    return jnp.asarray(np.stack([c64, sa64, sb64, cm, sam, sbm]).astype(np.float32))


def _swa_head_perm(t, axis):
    parts = jnp.split(t, SWA_HEADS, axis=axis)
    return jnp.concatenate([parts[0], parts[2], parts[1], parts[3]], axis=axis)


def _prep_params(ffn1_norm, mix_norm, w_in, ssd_conv_w, ssd_conv_b, ssd_dt_bias, ssd_a_log,
                 ssd_d, ssd_norm, swa_q_norm, swa_k_norm, swa_sink, swa_out_norm,
                 mla_q_lat_norm, mla_w_uq, mla_kv_norm, mla_w_ukv, mla_q_norm, mla_k_norm,
                 mla_out_norm, w_out, ffn2_norm):
    nl, d = w_in.shape[:2]
    row = lambda v: v.reshape(nl, 1, -1).astype(F32)
    pad_lanes = lambda v, n: jnp.pad(v, [(0, 0)] * (v.ndim - 1) + [(0, n - v.shape[-1])])

    w_in_p = _permute_w_in(jnp.swapaxes(w_in, 1, 2))

    wuq = mla_w_uq.reshape(nl, MLA_Q_RANK, MLA_HEADS, MLA_QK)
    wuq = pad_lanes(wuq, LANES).reshape(nl, MLA_Q_RANK, MLA_HEADS * LANES).astype(BF16)
    wukv = mla_w_ukv.reshape(nl, MLA_KV_RANK, MLA_HEADS, MLA_NOPE + MLA_V)
    knope = pad_lanes(wukv[..., :MLA_NOPE], LANES)
    vv = wukv[..., MLA_NOPE:]
    zv = jnp.zeros_like(vv)
    even = (np.arange(MLA_HEADS) % 2 == 0)[None, None, :, None]
    vpad = jnp.concatenate([jnp.where(even, vv, zv), jnp.where(even, zv, vv)], axis=-1)
    wukv_p = jnp.concatenate([knope.reshape(nl, MLA_KV_RANK, -1),
                              vpad.reshape(nl, MLA_KV_RANK, -1)], axis=-1).astype(BF16)

    a, b = SSD_INNER, SSD_INNER + SWA_WIDTH
    w_out_p = jnp.concatenate([w_out[:, :a], _swa_head_perm(w_out[:, a:b], 1), w_out[:, b:]],
                              axis=1).astype(BF16)

    peak = lambda g: jnp.max(jnp.abs(g.astype(F32)), axis=-1)
    mla_bound = (math.sqrt(MLA_QK) * LOG2E * MLA_BOUND_SLACK) * peak(mla_q_norm) * peak(mla_k_norm)

    two = lambda v: jnp.concatenate([v, v], axis=-1)
    return {
        "mla_bound": mla_bound,
        "ffn1_norm": row(ffn1_norm),
        "ffn2_norm": row(ffn2_norm),
        "w_in": w_in_p,
        "mix_norm": row(mix_norm),
        "conv_w": ssd_conv_w.astype(F32),
        "conv_b": row(ssd_conv_b),
        "dt_bias": pad_lanes(row(ssd_dt_bias), LANES),
        "swa_q_gain": row(two(swa_q_norm)),
        "swa_k_gain": row(two(swa_k_norm)),
        "mla_qlat_norm": row(mla_q_lat_norm),
        "w_uq": wuq,
        "mla_kv_norm": row(mla_kv_norm),
        "w_ukv": wukv_p,
        "mla_q_gain": pad_lanes(row(mla_q_norm), LANES),
        "mla_k_gain": pad_lanes(row(mla_k_norm), LANES),
        "ssd_a_log": pad_lanes(row(ssd_a_log), LANES),
        "ssd_d": row(jnp.repeat(ssd_d, SSD_HEAD_DIM, axis=-1)),
        "ssd_norm": row(ssd_norm),
        "swa_sink": swa_sink.astype(F32),
        "swa_out_norm": row(_swa_head_perm(swa_out_norm, 1)),
        "mla_out_norm": row(mla_out_norm),
        "w_out": w_out_p,
    }


def _tiles(seq):
    pick = lambda pref: math.gcd(seq, pref)
    return {"ffn": pick(1024), "proj": pick(512), "mla_q": pick(256), "mla_k": pick(1024),
            "swa_blocks": pick(16 * SWA_BLOCK) // SWA_BLOCK,
            "ssd_chunks": pick(4 * SSD_CHUNK) // SSD_CHUNK}


def kernel(x, ffn1_norm, ffn1_gate, ffn1_up, ffn1_down, mix_norm, w_in, ssd_conv_w, ssd_conv_b, ssd_dt_bias, ssd_a_log, ssd_d, ssd_norm, swa_q_norm, swa_k_norm, swa_sink, swa_out_norm, mla_q_lat_norm, mla_w_uq, mla_kv_norm, mla_w_ukv, mla_q_norm, mla_k_norm, mla_out_norm, w_out, ffn2_norm, ffn2_gate, ffn2_up, ffn2_down):
    batch, seq, d = x.shape
    assert seq % SSD_CHUNK == 0 and seq % SWA_BLOCK == 0
    assert ffn1_gate.shape[-1] % FFN_CHUNK == 0
    stacked = _prep_params(ffn1_norm, mix_norm, w_in, ssd_conv_w, ssd_conv_b, ssd_dt_bias,
                           ssd_a_log, ssd_d, ssd_norm, swa_q_norm, swa_k_norm, swa_sink,
                           swa_out_norm, mla_q_lat_norm, mla_w_uq, mla_kv_norm, mla_w_ukv,
                           mla_q_norm, mla_k_norm, mla_out_norm, w_out, ffn2_norm)
    tl = _tiles(seq)
    tabs = _rope_tables(seq)
    xf = x.reshape(batch * seq, d).astype(F32)
    for l in range(w_in.shape[0]):
        p = {k: (v, l) for k, v in stacked.items()}
        xf = _ffn(xf, p["ffn1_norm"], l, ffn1_gate, ffn1_up, ffn1_down, tm=tl["ffn"])
        z, xbc, dt, swq, swk, swv, mq, mk, mv = _in_proj(xf, tabs, p, seq, tl["proj"])
        y_ssd = _ssd(xbc, dt, z, p, batch, seq, tl["ssd_chunks"])
        y_swa = _swa(swq, swk, swv, p["swa_sink"], p["swa_out_norm"], batch, seq,
                     tl["swa_blocks"])
        y_mla = _mla(mq, mk, mv, p["mla_bound"], p["mla_out_norm"], batch, seq, tl["mla_q"],
                     tl["mla_k"])
        xf = _ffn(xf, p["ffn2_norm"], l, ffn2_gate, ffn2_up, ffn2_down, tm=tl["ffn"],
                  mixer=(y_ssd, y_swa, y_mla, p["w_out"]))
    return xf.reshape(batch, seq, d).astype(x.dtype)
```

```python
import functools
import math

import jax
import jax.numpy as jnp
import numpy as np
from jax import lax
from jax.experimental import pallas as pl
from jax.experimental.pallas import tpu as pltpu

F32 = jnp.float32
BF16 = jnp.bfloat16

LANES = 128
HALF = LANES // 2

SSD_HEADS = 8
SSD_HEAD_DIM = 64
SSD_INNER = SSD_HEADS * SSD_HEAD_DIM
SSD_GROUPS = 2
SSD_STATE = 128
SSD_CONV = 5
SSD_CHUNK = 256
SSD_CONV_DIM = SSD_INNER + 2 * SSD_GROUPS * SSD_STATE
SWA_HEADS = 4
SWA_KV_HEADS = 2
SWA_HEAD_DIM = 64
SWA_WINDOW = 128
SWA_BLOCK = 128
SWA_WIDTH = SWA_HEADS * SWA_HEAD_DIM
MLA_HEADS = 4
MLA_Q_RANK = 256
MLA_KV_RANK = 128
MLA_NOPE = 64
MLA_ROPE = 32
MLA_QK = MLA_NOPE + MLA_ROPE
MLA_V = 64
MLA_WIDTH = MLA_HEADS * MLA_V
ROPE_THETA = 10000.0
EPS = 1e-6

C_Z = 0
C_XBC = C_Z + SSD_INNER
C_SWQ = C_XBC + SSD_CONV_DIM
C_SWK = C_SWQ + SWA_WIDTH
C_SWV = C_SWK + LANES
C_MLQ = C_SWV + LANES
C_CKV = C_MLQ + MLA_Q_RANK
C_MISC = C_CKV + MLA_KV_RANK
C_END = C_MISC + LANES
CONV_HALO = 16
FFN_CHUNK = 256
FFN_RESIDUAL_SCALE = 0.5
FFN_LOAD_DEPTH = 4
NEG_INF = float("-inf")
LOG_FLOOR = -1e30
LOG2E = math.log2(math.e)
MLA_BOUND_LIMIT = 48.0
MLA_BOUND_SLACK = 1.02


def _rms(x, w):
    ms = jnp.mean(x * x, axis=-1, keepdims=True)
    return x * lax.rsqrt(ms + EPS) * w


def _silu(x):
    return x / (1.0 + jnp.exp(-x))


def _dot(a, b):
    return jnp.dot(a, b, preferred_element_type=F32)


def _dot_nt(a, b):
    return lax.dot_general(a, b, (((1,), (1,)), ((), ())), preferred_element_type=F32)


def _dot_tn(a, b):
    return lax.dot_general(a, b, (((0,), (0,)), ((), ())), preferred_element_type=F32)


def _lane_iota(shape):
    return lax.broadcasted_iota(jnp.int32, shape, len(shape) - 1)


def _resident(param, single_buffer=False):
    mode = {"pipeline_mode": pl.Buffered(1)} if single_buffer else {}
    if isinstance(param, tuple):
        arr, layer = param
        rest = (0,) * (arr.ndim - 1)
        return arr, pl.BlockSpec((None,) + arr.shape[1:], lambda *_: (layer,) + rest, **mode)
    zeros = (0,) * param.ndim
    return param, pl.BlockSpec(param.shape, lambda *_: zeros, **mode)


def _load_ffn_weights(layer, gate_hbm, up_hbm, down_hbm, wg_ref, wu_ref, wd_ref,
                      col_stage, row_stage, sems):
    jobs = []
    for c in range(wg_ref.shape[1] // FFN_CHUNK):
        lo, hi = c * FFN_CHUNK, (c + 1) * FFN_CHUNK
        jobs.append((gate_hbm.at[layer, :, lo:hi], col_stage, wg_ref, (slice(None), slice(lo, hi)), 1.0))
        jobs.append((up_hbm.at[layer, :, lo:hi], col_stage, wu_ref, (slice(None), slice(lo, hi)), 1.0))
        jobs.append((down_hbm.at[layer, lo:hi, :], row_stage, wd_ref, (slice(lo, hi), slice(None)),
                     FFN_RESIDUAL_SCALE))

    depth = col_stage.shape[0]

    def copy(j):
        src, stage = jobs[j][:2]
        return pltpu.make_async_copy(src, stage.at[j % depth], sems.at[j % depth])

    for j in range(depth - 1):
        copy(j).start()
    for j, (_, stage, dst, where, scale) in enumerate(jobs):
        ahead = j + depth - 1
        if ahead < len(jobs):
            copy(ahead).start()
        copy(j).wait()
        piece = stage[j % depth]
        dst[where] = (piece if scale == 1.0 else piece * scale).astype(BF16)


def _ffn_body(*refs, fuse_out_proj, layer):
    if fuse_out_proj:
        x_ref, ys_ref, yw_ref, ym_ref, wo_ref = refs[:5]
        refs = refs[5:]
    else:
        x_ref = refs[0]
        refs = refs[1:]
    (nw_ref, gate_hbm, up_hbm, down_hbm, o_ref,
     xn_ref, wg_ref, wu_ref, wd_ref, col_stage, row_stage, sems) = refs

    @pl.when(pl.program_id(0) == 0)
    def _():
        _load_ffn_weights(layer, gate_hbm, up_hbm, down_hbm, wg_ref, wu_ref, wd_ref,
                          col_stage, row_stage, sems)

    x = x_ref[...]
    if fuse_out_proj:
        a, b = SSD_INNER, SSD_INNER + SWA_WIDTH
        x = (x + _dot(ys_ref[...], wo_ref[0:a, :]) + _dot(yw_ref[...], wo_ref[a:b, :])
             + _dot(ym_ref[...], wo_ref[b:, :]))
    o_ref[...] = x
    xn_ref[...] = _rms(x, nw_ref[...]).astype(BF16)
    for c in range(wg_ref.shape[1] // FFN_CHUNK):
        cols = slice(c * FFN_CHUNK, (c + 1) * FFN_CHUNK)
        xn = xn_ref[...]
        g = _dot(xn, wg_ref[:, cols])
        u = _dot(xn, wu_ref[:, cols])
        h = (_silu(g) * u).astype(BF16)
        o_ref[...] += _dot(h, wd_ref[cols, :])


def _ffn(x, nw, layer, gate, up, down, tm, mixer=None):
    n, d = x.shape
    f = gate.shape[-1]
    row = lambda w: pl.BlockSpec((tm, w), lambda i: (i, 0))
    args, specs = [x], [row(d)]
    resident = [_resident(nw)]
    if mixer is not None:
        args += list(mixer[:3])
        specs += [row(a.shape[1]) for a in mixer[:3]]
        resident.insert(0, _resident(mixer[3]))
    args += [a for a, _ in resident] + [gate, up, down]
    specs += [s for _, s in resident] + [pl.BlockSpec(memory_space=pl.ANY)] * 3
    return pl.pallas_call(
        functools.partial(_ffn_body, fuse_out_proj=mixer is not None, layer=layer),
        grid=(n // tm,),
        in_specs=specs,
        out_specs=row(d),
        out_shape=jax.ShapeDtypeStruct((n, d), F32),
        scratch_shapes=[pltpu.VMEM((tm, d), BF16),
                        pltpu.VMEM((d, f), BF16), pltpu.VMEM((d, f), BF16),
                        pltpu.VMEM((f, d), BF16),
                        pltpu.VMEM((FFN_LOAD_DEPTH, d, FFN_CHUNK), F32),
                        pltpu.VMEM((FFN_LOAD_DEPTH, FFN_CHUNK, d), F32),
                        pltpu.SemaphoreType.DMA((FFN_LOAD_DEPTH,))],
        compiler_params=pltpu.CompilerParams(
            dimension_semantics=("arbitrary",), vmem_limit_bytes=56 * 1024 * 1024),
        name="ffn",
    )(*args)


def _permute_w_in_body(src_ref, dst_ref):
    n_src, cols = src_ref.shape
    hd = SWA_HEAD_DIM
    dt0 = C_SWQ
    q0 = dt0 + 2 * SSD_HEADS
    rest0 = q0 + SWA_WIDTH
    kr0 = n_src - MLA_ROPE
    zeros = lambda n: jnp.zeros((n, cols), F32)
    pieces = [src_ref[0:dt0, :]]
    pieces += [src_ref[q0 + h * hd:q0 + (h + 1) * hd, :] for h in (0, 2, 1, 3)]
    pieces += [src_ref[rest0:kr0, :],
               src_ref[dt0:q0, :], zeros(MLA_NOPE - 2 * SSD_HEADS),
               src_ref[kr0:n_src, :], zeros(LANES - MLA_QK)]
    dst_ref[...] = jnp.concatenate(pieces, axis=0).T.astype(BF16)


def _permute_w_in(w_in_t, col_block=256):
    nl, n_src, d = w_in_t.shape
    assert n_src + (MLA_NOPE - 2 * SSD_HEADS) + (LANES - MLA_QK) == C_END
    return pl.pallas_call(
        _permute_w_in_body,
        grid=(nl, d // col_block),
        in_specs=[pl.BlockSpec((None, n_src, col_block), lambda l, i: (l, 0, i))],
        out_specs=pl.BlockSpec((None, col_block, C_END), lambda l, i: (l, i, 0)),
        out_shape=jax.ShapeDtypeStruct((nl, d, C_END), BF16),
        compiler_params=pltpu.CompilerParams(dimension_semantics=("parallel", "parallel")),
        name="permute_w_in",
    )(w_in_t)


def _rope(y, c, sa, sb, shift):
    return (y * c + pltpu.roll(y, LANES - shift, 1) * sa + pltpu.roll(y, shift, 1) * sb)


def _head_sum_matrix(width):
    r = lax.broadcasted_iota(jnp.int32, (LANES, LANES), 0)
    c = lax.broadcasted_iota(jnp.int32, (LANES, LANES), 1)
    same = (r < width) if width > HALF else ((r < HALF) == (c < HALF))
    return jnp.where(same, 1.0, 0.0).astype(BF16)


def _head_norm_rope(x, sumsq, gain, c, sa, sb, width, rope_dim, post_scale):
    y = x * lax.rsqrt(sumsq / float(width) + EPS) * gain
    out = _rope(y, c, sa, sb, rope_dim // 2)
    return out if post_scale == 1.0 else out * post_scale


def _inproj_body(x_ref, xp_ref, xnx_ref, tab_ref, w_ref, mixw_ref, convw_ref, convb_ref,
                 dtb_ref, swqg_ref, swkg_ref, qlw_ref, wuq_ref, kvw_ref, wukv_ref,
                 mqg_ref, mkg_ref,
                 z_ref, xbc_ref, dt_ref, swq_ref, swk_ref, swv_ref, mq_ref, mk_ref, mv_ref,
                 *, tiles_per_seq):
    i = pl.program_id(0)
    tm = x_ref.shape[0]
    pos_tile = i % tiles_per_seq
    mixw = mixw_ref[...]
    hn = _rms(x_ref[...], mixw).astype(BF16)
    hp = _rms(xp_ref[...], mixw).astype(BF16)
    hx = _rms(xnx_ref[...], mixw).astype(BF16)
    groups = lambda v: [v[:, g * LANES:(g + 1) * LANES] for g in range(v.shape[1] // LANES)]

    z = _dot(hn, w_ref[:, C_Z:C_XBC])
    conv_w = 2 * LANES
    xbc_parts = []
    for c0 in range(C_XBC, C_SWQ, conv_w):
        wx = w_ref[:, c0:c0 + conv_w]
        xbc_parts.append((_dot(hp, wx), _dot(hn, wx), _dot(hx, wx)))
    ckv, misc = groups(_dot(hn, w_ref[:, C_CKV:C_END]))
    swa_q = groups(_dot(hn, w_ref[:, C_SWQ:C_SWK]))
    swa_k, swa_v = groups(_dot(hn, w_ref[:, C_SWK:C_MLQ]))
    ql = _dot(hn, w_ref[:, C_MLQ:C_CKV])

    q_up = groups(_dot(_rms(ql, qlw_ref[...]).astype(BF16), wuq_ref[...]))
    kv_up = groups(_dot(_rms(ckv, kvw_ref[...]).astype(BF16), wukv_ref[...]))
    lane = _lane_iota(misc.shape)
    kr = jnp.where((lane >= MLA_NOPE) & (lane < MLA_QK), misc, 0.0)
    mla_q = q_up
    mla_k = [k_nope + kr for k_nope in kv_up[:MLA_HEADS]]
    mla_v = kv_up[MLA_HEADS:]

    sum64, sum96 = _head_sum_matrix(SWA_HEAD_DIM), _head_sum_matrix(MLA_QK)
    sumsq = lambda xs, m: [_dot((v * v).astype(BF16), m) for v in xs]
    ss_swa_q, ss_swa_k = sumsq(swa_q, sum64), sumsq([swa_k], sum64)
    ss_mla_q, ss_mla_k = sumsq(mla_q, sum96), sumsq(mla_k, sum96)

    z_ref[...] = z.astype(BF16)

    keep_prev = (pos_tile > 0).astype(F32)
    keep_next = (pos_tile < tiles_per_seq - 1).astype(F32)
    for part, (xbc_prev, xbc_main, xbc_next) in enumerate(xbc_parts):
        cs = slice(part * conv_w, (part + 1) * conv_w)
        padded = jnp.concatenate([xbc_prev * keep_prev, xbc_main, xbc_next * keep_next], axis=0)
        n_pad = padded.shape[0]
        conv = jnp.broadcast_to(convb_ref[:, cs], (tm, conv_w))
        for k in range(SSD_CONV):
            shift = SSD_CONV // 2 - k
            tap = padded if shift == 0 else pltpu.roll(padded, shift % n_pad, 0)
            conv = conv + convw_ref[k:k + 1, cs] * tap[CONV_HALO:CONV_HALO + tm, :]
        xbc_ref[:, cs] = _silu(conv).astype(BF16)

    dtv = misc + dtb_ref[...]
    dt_ref[...] = jnp.maximum(dtv, 0.0) + jnp.log1p(jnp.exp(-jnp.abs(dtv)))

    swa_rope = (tab_ref[0], tab_ref[1], tab_ref[2], SWA_HEAD_DIM, SWA_HEAD_DIM)
    for g, (v, ss) in enumerate(zip(swa_q, ss_swa_q)):
        swq_ref[:, g * LANES:(g + 1) * LANES] = _head_norm_rope(
            v, ss, swqg_ref[...], *swa_rope, SWA_HEAD_DIM ** -0.5 * LOG2E).astype(BF16)
    swk_ref[...] = _head_norm_rope(swa_k, ss_swa_k[0], swkg_ref[...], *swa_rope, 1.0).astype(BF16)
    swv_ref[...] = swa_v.astype(BF16)

    mla_rope = (tab_ref[3], tab_ref[4], tab_ref[5], MLA_QK, MLA_ROPE)
    for h in range(MLA_HEADS):
        sl = slice(h * LANES, (h + 1) * LANES)
        mq_ref[:, sl] = _head_norm_rope(mla_q[h], ss_mla_q[h], mqg_ref[...], *mla_rope,
                                        MLA_QK ** -0.5 * LOG2E).astype(BF16)
        mk_ref[:, sl] = _head_norm_rope(mla_k[h], ss_mla_k[h], mkg_ref[...], *mla_rope,
                                        1.0).astype(BF16)
        ones_lane = HALF if h % 2 == 0 else 0
        mv_ref[:, sl] = jnp.where(lane == ones_lane, 1.0, mla_v[h]).astype(BF16)


def _in_proj(x, tabs, p, seq, tm):
    n, d = x.shape
    tiles_per_seq = seq // tm
    halo_per_tile = tm // CONV_HALO
    n_halo = n // CONV_HALO
    row = lambda w: pl.BlockSpec((tm, w), lambda i: (i, 0))
    in_specs = [
        row(d),
        pl.BlockSpec((CONV_HALO, d), lambda i: (jnp.maximum(i * halo_per_tile - 1, 0), 0)),
        pl.BlockSpec((CONV_HALO, d),
                     lambda i: (jnp.minimum((i + 1) * halo_per_tile, n_halo - 1), 0)),
        pl.BlockSpec((6, tm, LANES), lambda i: (0, i % tiles_per_seq, 0)),
    ]
    names = ("w_in", "mix_norm", "conv_w", "conv_b", "dt_bias", "swa_q_gain", "swa_k_gain",
             "mla_qlat_norm", "w_uq", "mla_kv_norm", "w_ukv", "mla_q_gain", "mla_k_gain")
    resident = [_resident(p[k]) for k in names]
    args = [a for a, _ in resident]
    in_specs += [s for _, s in resident]
    widths = (SSD_INNER, SSD_CONV_DIM, LANES, SWA_WIDTH, LANES, LANES,
              MLA_HEADS * LANES, MLA_HEADS * LANES, MLA_HEADS * LANES)
    dts = (BF16, BF16, F32, BF16, BF16, BF16, BF16, BF16, BF16)
    return pl.pallas_call(
        functools.partial(_inproj_body, tiles_per_seq=tiles_per_seq),
        grid=(n // tm,),
        in_specs=in_specs,
        out_specs=[row(w) for w in widths],
        out_shape=[jax.ShapeDtypeStruct((n, w), t) for w, t in zip(widths, dts)],
        compiler_params=pltpu.CompilerParams(
            dimension_semantics=("parallel",), vmem_limit_bytes=48 * 1024 * 1024),
        name="in_proj",
    )(x, x, x, tabs, *args)


def _split_bf16(v, n):
    pieces = []
    for _ in range(n - 1):
        p = v.astype(BF16)
        pieces.append(p)
        v = v - p.astype(F32)
    pieces.append(v.astype(BF16))
    return pieces


def _pack_lanes(pieces, width):
    lane = _lane_iota(pieces[0].shape)
    out = jnp.zeros(pieces[0].shape, F32)
    for k, p in enumerate(pieces):
        pf = p.astype(F32)
        if k:
            pf = pltpu.roll(pf, k * width, 1)
        out = jnp.where((lane >= k * width) & (lane < (k + 1) * width), pf, out)
    return out.astype(BF16)


def _ssd_expand_matrices():
    nd = 2 * SSD_HEADS
    rows = np.arange(LANES)[:, None]
    head = np.arange(SSD_INNER)[None, :] // SSD_HEAD_DIM
    mats = []
    for direction in range(2):
        for first_piece in (0, 2):
            d = direction * SSD_HEADS + head
            hit = (rows == first_piece * nd + d) | (rows == (first_piece + 1) * nd + d)
            mats.append(hit)
    return jnp.asarray(np.stack(mats).astype(np.float32), BF16)


def _ssd_body(xbc_ref, dt_ref, z_ref, alog_ref, dskip_ref, nw_ref, expand_ref, o_ref,
              yacc_ref, sf_ref, sb_ref, cols_ref, *, n_chunks):
    pas = pl.program_id(1)
    j = pl.program_id(2)
    t = SSD_CHUNK
    cps = xbc_ref.shape[0] // t
    nh = SSD_HEADS

    gw = SSD_INNER // SSD_GROUPS
    hpg = nh // SSD_GROUPS
    nd = 2 * nh
    lane = _lane_iota((t, LANES))

    def b_of(rows, g):
        return xbc_ref[rows, SSD_INNER + g * SSD_STATE:SSD_INNER + (g + 1) * SSD_STATE]

    def c_of(rows, g):
        return xbc_ref[rows, SSD_INNER + (SSD_GROUPS + g) * SSD_STATE:
                       SSD_INNER + (SSD_GROUPS + g + 1) * SSD_STATE]

    def forward_early(k):
        rows = slice(k * t, (k + 1) * t)
        fwd = lane < nh
        dt = dt_ref[rows, :]
        avec = jnp.where(_lane_iota((1, LANES)) < nd, -jnp.exp(alog_ref[...]), 0.0)
        a = dt * avec
        row = lax.broadcasted_iota(jnp.int32, (t, t), 0)
        col = lax.broadcasted_iota(jnp.int32, (t, t), 1)
        tri = jnp.where(col <= row, 1.0, 0.0).astype(BF16)
        cum = sum(_dot(tri, part) for part in _split_bf16(a, 3))
        excl = cum - a
        tot = cum[t - 1:t, :]

        e_in = jnp.exp(jnp.where(fwd, cum, tot - excl))
        w_st = jnp.exp(jnp.where(fwd, tot - cum, excl)) * dt
        e1, e2 = _split_bf16(e_in, 2)
        w1, w2 = _split_bf16(w_st, 2)
        cols = _pack_lanes([e1, e2, w1, w2], nd)
        cols_ref[j * cps + k] = cols

        r = jnp.where(fwd, cum, -excl) * LOG2E
        cc = r - jnp.maximum(jnp.log(dt), LOG_FLOOR) * LOG2E
        ones = jnp.ones((t, LANES), F32)
        lhs = jnp.where(lane < 3 * nd, _pack_lanes(_split_bf16(r, 3), nd),
                        jnp.where(lane < 6 * nd, ones, 0.0).astype(BF16))
        cc_t = cc.T[0:nd, :]
        c1, c2, c3 = _split_bf16(cc_t, 3)
        rhs = jnp.concatenate(
            [jnp.ones((3 * nd, t), BF16), -c1, -c2, -c3, jnp.zeros((LANES - 6 * nd, t), BF16)],
            axis=0)
        sub = lax.broadcasted_iota(jnp.int32, (LANES, t), 0) % nd

        e_x = _dot(cols, expand_ref[0])
        w_x = _dot(cols, expand_ref[1])
        cbs = [_dot_nt(c_of(rows, g), b_of(rows, g)) for g in range(SSD_GROUPS)]
        ys = []
        for g in range(SSD_GROUPS):
            gs = slice(g * gw, (g + 1) * gw)
            s_in = sf_ref[g]
            xgf = xbc_ref[rows, gs].astype(F32)
            ys.append(dskip_ref[:, gs] * xgf + _dot(c_of(rows, g), s_in.astype(BF16)) * e_x[:, gs])
            xw = (xgf * w_x[:, gs]).astype(BF16)
            sf_ref[g] = s_in * e_x[t - 1:t, gs] + _dot_tn(b_of(rows, g), xw)
        exps = [_dot(lhs, jnp.where(sub == idx, rhs, jnp.zeros_like(rhs))) for idx in range(nd)]
        return cbs, ys, exps

    def forward_late(k, cbs, ys, exps):
        rows = slice(k * t, (k + 1) * t)
        th = t // 2
        quad = lambda v, qi, qj: v[qi * th:(qi + 1) * th, qj * th:(qj + 1) * th]
        qrow = lax.broadcasted_iota(jnp.int32, (th, th), 0)
        qcol = lax.broadcasted_iota(jnp.int32, (th, th), 1)
        on_low = qrow >= qcol
        on_up = qcol >= qrow

        def mixing(h, cb):
            df, db = exps[h], exps[nh + h]
            diag = [quad(cb, q, q) * (jnp.exp2(jnp.where(on_low, quad(df, q, q), NEG_INF))
                                      + jnp.exp2(jnp.where(on_up, quad(db, q, q), NEG_INF)))
                    for q in range(2)]
            upper = quad(cb, 0, 1) * jnp.exp2(quad(db, 0, 1))
            lower = quad(cb, 1, 0) * jnp.exp2(quad(df, 1, 0))
            return jnp.concatenate([jnp.concatenate([diag[0], upper], axis=1),
                                    jnp.concatenate([lower, diag[1]], axis=1)], axis=0)

        glane = _lane_iota((t, gw))
        for g in range(SSD_GROUPS):
            gs = slice(g * gw, (g + 1) * gw)
            xg = xbc_ref[rows, gs]
            y = ys[g]
            for hh in range(hpg):
                m = mixing(g * hpg + hh, cbs[g]).astype(BF16)
                mine = (glane >= hh * SSD_HEAD_DIM) & (glane < (hh + 1) * SSD_HEAD_DIM)
                y = y + _dot(m, jnp.where(mine, xg, jnp.zeros_like(xg)))
            yacc_ref[j * cps + k, :, gs] = y

    @pl.when(pas == 0)
    def _forward():
        @pl.when(j == 0)
        def _():
            sf_ref[...] = jnp.zeros_like(sf_ref)

        early = [forward_early(k) for k in range(cps)]
        for k in range(cps):
            forward_late(k, *early[k])

    @pl.when(pas == 1)
    def _backward():
        @pl.when(j == 0)
        def _():
            sb_ref[...] = jnp.zeros_like(sb_ref)

        for k in reversed(range(cps)):
            rows = slice(k * t, (k + 1) * t)
            c = n_chunks - 1 - j * cps - (cps - 1 - k)
            cols = cols_ref[c]
            e_x = _dot(cols, expand_ref[2])
            w_x = _dot(cols, expand_ref[3])
            gate = _silu(z_ref[rows, :].astype(F32))
            for g in range(SSD_GROUPS):
                gs = slice(g * gw, (g + 1) * gw)
                s_in = sb_ref[g]
                y = yacc_ref[c, :, gs] + _dot(c_of(rows, g), s_in.astype(BF16)) * e_x[:, gs]
                o_ref[rows, gs] = _rms(y * gate[:, gs], nw_ref[:, gs]).astype(BF16)
                xw = (xbc_ref[rows, gs].astype(F32) * w_x[:, gs]).astype(BF16)
                sb_ref[g] = s_in * e_x[0:1, gs] + _dot_tn(b_of(rows, g), xw)


def _ssd(xbc, dt, z, p, batch, seq, cps):
    t = SSD_CHUNK
    nc = seq // t
    ns = nc // cps

    def chunk_idx(b, pas, j):
        return b * ns + j + pas * (ns - 1 - 2 * j)

    def out_idx(b, pas, j):
        return b * ns + (ns - 1) - pas * j

    blk = lambda w, f: pl.BlockSpec((cps * t, w), lambda b, pas, j: (f(b, pas, j), 0))
    resident = [_resident(a) for a in (p["ssd_a_log"], p["ssd_d"], p["ssd_norm"],
                                       _ssd_expand_matrices())]
    state = pltpu.VMEM((SSD_GROUPS, SSD_STATE, SSD_INNER // SSD_GROUPS), F32)
    return pl.pallas_call(
        functools.partial(_ssd_body, n_chunks=nc),
        grid=(batch, 2, ns),
        in_specs=[blk(SSD_CONV_DIM, chunk_idx), blk(LANES, chunk_idx), blk(SSD_INNER, chunk_idx)]
                 + [s for _, s in resident],
        out_specs=blk(SSD_INNER, out_idx),
        out_shape=jax.ShapeDtypeStruct((batch * seq, SSD_INNER), BF16),
        scratch_shapes=[pltpu.VMEM((nc, t, SSD_INNER), F32), state, state,
                        pltpu.VMEM((nc, t, LANES), BF16)],
        compiler_params=pltpu.CompilerParams(
            dimension_semantics=("arbitrary", "arbitrary", "arbitrary"),
            vmem_limit_bytes=40 * 1024 * 1024),
        name="ssd",
    )(xbc, dt, z, *[a for a, _ in resident])


def _swa_body(sink_ref, q_ref, kp_ref, kc_ref, kn_ref, vp_ref, vc_ref, vn_ref, nw_ref, o_ref,
              *, layer):
    assert SWA_WINDOW == SWA_BLOCK
    j = pl.program_id(1)
    blk = SWA_BLOCK
    n_blk = q_ref.shape[0] // blk
    kb = jnp.concatenate([kp_ref[...], kc_ref[...], kn_ref[...]], axis=0)
    vb = jnp.concatenate([vp_ref[...], vc_ref[...], vn_ref[...]], axis=0)
    qi = lax.broadcasted_iota(jnp.int32, (blk, blk), 0)
    kj = lax.broadcasted_iota(jnp.int32, (blk, blk), 1)
    lo = _lane_iota((blk, LANES)) < HALF
    lo_v = _lane_iota(vb.shape) < HALF
    zero_q = jnp.zeros((blk, LANES), BF16)
    lane_v = _lane_iota(vb.shape)
    unit = lambda at: jnp.where(lane_v == at, 1.0, 0.0).astype(BF16)
    v_half = (jnp.where(lo_v, vb, unit(HALF)), jnp.where(lo_v, unit(0), vb))
    first = jnp.where(j == 0, blk, 0)
    last = jnp.where(j == pl.num_programs(1) - 1, blk, 0)
    items = [(t, g, half) for t in range(n_blk) for g in range(SWA_WIDTH // LANES)
             for half in range(2)]
    logits = {}
    for t, g, half in items:
        qg = q_ref[t * blk:(t + 1) * blk, g * LANES:(g + 1) * LANES]
        qm = jnp.where(lo, qg, zero_q) if half == 0 else jnp.where(lo, zero_q, qg)
        logits[t, g, half] = _dot_nt(qm, kb[t * blk:(t + 3) * blk])
    scaled = {}
    for t, g, half in items:
        keep_prev = kj >= (qi + first if t == 0 else qi)
        keep_next = kj <= (qi - last if t == n_blk - 1 else qi)
        s = logits[t, g, half]
        s_prev = jnp.where(keep_prev, s[:, :blk], NEG_INF)
        s_own = s[:, blk:2 * blk]
        s_next = jnp.where(keep_next, s[:, 2 * blk:], NEG_INF)
        sk = sink_ref[layer, g + 2 * half] * LOG2E
        m = jnp.max(jnp.maximum(jnp.maximum(s_prev, s_own), s_next), axis=-1, keepdims=True)
        m = jnp.maximum(m, sk)
        pexp = jnp.concatenate([jnp.exp2(v - m) for v in (s_prev, s_own, s_next)],
                               axis=-1).astype(BF16)
        acc = _dot(pexp, v_half[half][t * blk:(t + 3) * blk])
        ones_lane = HALF if half == 0 else 0
        den = acc[:, ones_lane:ones_lane + 1] + jnp.exp2(sk - m)
        scaled[t, g, half] = acc / den
    for t in range(n_blk):
        y = jnp.concatenate([jnp.where(lo, scaled[t, g, 0], scaled[t, g, 1])
                             for g in range(SWA_WIDTH // LANES)], axis=-1)
        o_ref[t * blk:(t + 1) * blk, :] = _rms(y, nw_ref[...]).astype(BF16)


def _swa(q, k, v, sink, nw, batch, seq, n_blk):
    blk = SWA_BLOCK
    nb = seq // blk
    ns = nb // n_blk
    cur = lambda b, j: (b * ns + j, 0)
    prv = lambda b, j: (b * nb + jnp.maximum(j * n_blk - 1, 0), 0)
    nxt = lambda b, j: (b * nb + jnp.minimum((j + 1) * n_blk, nb - 1), 0)
    edge = lambda f: pl.BlockSpec((blk, LANES), f)
    main = pl.BlockSpec((n_blk * blk, LANES), cur)
    sinks, layer = sink
    nw_arr, nw_spec = _resident(nw)
    return pl.pallas_call(
        functools.partial(_swa_body, layer=layer),
        grid=(batch, ns),
        in_specs=[pl.BlockSpec(memory_space=pltpu.SMEM),
                  pl.BlockSpec((n_blk * blk, SWA_WIDTH), cur),
                  edge(prv), main, edge(nxt), edge(prv), main, edge(nxt), nw_spec],
        out_specs=pl.BlockSpec((n_blk * blk, SWA_WIDTH), cur),
        out_shape=jax.ShapeDtypeStruct((batch * seq, SWA_WIDTH), BF16),
        compiler_params=pltpu.CompilerParams(dimension_semantics=("parallel", "parallel")),
        name="swa",
    )(sinks, q, k, k, k, v, v, v, nw_arr)


def _mla_body(bound_ref, q_ref, k_ref, v_ref, nw_ref, o_ref, *, key_chunk, layer):
    tq = q_ref.shape[0]
    seq = k_ref.shape[0]
    heads = [slice(h * LANES, (h + 1) * LANES) for h in range(MLA_HEADS)]
    chunks = [slice(c * key_chunk, (c + 1) * key_chunk) for c in range(seq // key_chunk)]
    lo = _lane_iota((tq, LANES)) < HALF
    bound = bound_ref[layer]
    use_bound = bound <= MLA_BOUND_LIMIT

    def denom(acc, h):
        return acc[:, HALF:HALF + 1] if h % 2 == 0 else acc[:, 0:1]

    def finish(accs, dens):
        outs = [jnp.where(lo, accs[2 * p] / dens[2 * p], accs[2 * p + 1] / dens[2 * p + 1])
                for p in range(MLA_HEADS // 2)]
        y = jnp.concatenate(outs, axis=-1)
        o_ref[...] = _rms(y, nw_ref[...]).astype(BF16)

    @pl.when(use_bound)
    def _bounded():
        accs = []
        for h, sl in enumerate(heads):
            qh = q_ref[:, sl]
            acc = None
            for ks in chunks:
                pexp = jnp.exp2(_dot_nt(qh, k_ref[ks, sl]) - bound).astype(BF16)
                part = _dot(pexp, v_ref[ks, sl])
                acc = part if acc is None else acc + part
            accs.append(acc)
        finish(accs, [denom(a, h) for h, a in enumerate(accs)])

    @pl.when(jnp.logical_not(use_bound))
    def _online():
        accs, dens = [], []
        for h, sl in enumerate(heads):
            qh = q_ref[:, sl]
            m = jnp.full((tq, 1), NEG_INF, F32)
            acc = jnp.zeros((tq, LANES), F32)
            for ks in chunks:
                s = _dot_nt(qh, k_ref[ks, sl])
                m_new = jnp.maximum(m, jnp.max(s, axis=-1, keepdims=True))
                pexp = jnp.exp2(s - m_new).astype(BF16)
                acc = jnp.exp2(m - m_new) * acc + _dot(pexp, v_ref[ks, sl])
                m = m_new
            accs.append(acc)
            dens.append(denom(acc, h))
        finish(accs, dens)


def _mla(q, k, v, bound, nw, batch, seq, tq, key_chunk):
    nq = seq // tq
    w = MLA_HEADS * LANES
    full = pl.BlockSpec((seq, w), lambda b, i: (b, 0))
    bounds, layer = bound
    nw, nw_spec = _resident(nw)
    return pl.pallas_call(
        functools.partial(_mla_body, key_chunk=key_chunk, layer=layer),
        grid=(batch, nq),
        in_specs=[pl.BlockSpec(memory_space=pltpu.SMEM),
                  pl.BlockSpec((tq, w), lambda b, i: (b * nq + i, 0)), full, full, nw_spec],
        out_specs=pl.BlockSpec((tq, MLA_WIDTH), lambda b, i: (b * nq + i, 0)),
        out_shape=jax.ShapeDtypeStruct((batch * seq, MLA_WIDTH), BF16),
        compiler_params=pltpu.CompilerParams(
            dimension_semantics=("parallel", "parallel"),
            vmem_limit_bytes=48 * 1024 * 1024),
        name="mla",
    )(bounds, q, k, v, nw)


def _rope_tables(seq):
    def angles(dim):
        inv = 1.0 / np.power(ROPE_THETA, np.arange(0, dim, 2, dtype=np.float64) / dim)
        return np.arange(seq, dtype=np.float64)[:, None] * inv[None, :]

    a64 = angles(SWA_HEAD_DIM)
    c, s = np.cos(a64), np.sin(a64)
    zero = np.zeros_like(c)
    c64 = np.concatenate([c, c, c, c], axis=-1)
    sa64 = np.concatenate([-s, zero, -s, zero], axis=-1)
    sb64 = np.concatenate([zero, s, zero, s], axis=-1)
    a32 = angles(MLA_ROPE)
    c, s = np.cos(a32), np.sin(a32)
    zero = np.zeros_like(c)
    ones = np.ones((seq, MLA_NOPE))
    pad = np.zeros((seq, LANES - MLA_QK))
    zn = np.zeros((seq, MLA_NOPE))
    cm = np.concatenate([ones, c, c, pad], axis=-1)
    sam = np.concatenate([zn, -s, zero, pad], axis=-1)
    sbm = np.concatenate([zn, zero, s, pad], axis=-1)
    return jnp.asarray(np.stack([c64, sa64, sb64, cm, sam, sbm]).astype(np.float32))


def _swa_head_perm(t, axis):
    parts = jnp.split(t, SWA_HEADS, axis=axis)
    return jnp.concatenate([parts[0], parts[2], parts[1], parts[3]], axis=axis)


def _prep_params(ffn1_norm, mix_norm, w_in, ssd_conv_w, ssd_conv_b, ssd_dt_bias, ssd_a_log,
                 ssd_d, ssd_norm, swa_q_norm, swa_k_norm, swa_sink, swa_out_norm,
                 mla_q_lat_norm, mla_w_uq, mla_kv_norm, mla_w_ukv, mla_q_norm, mla_k_norm,
                 mla_out_norm, w_out, ffn2_norm):
    nl, d = w_in.shape[:2]
    row = lambda v: v.reshape(nl, 1, -1).astype(F32)
    pad_lanes = lambda v, n: jnp.pad(v, [(0, 0)] * (v.ndim - 1) + [(0, n - v.shape[-1])])

    w_in_p = _permute_w_in(jnp.swapaxes(w_in, 1, 2))

    wuq = mla_w_uq.reshape(nl, MLA_Q_RANK, MLA_HEADS, MLA_QK)
    wuq = pad_lanes(wuq, LANES).reshape(nl, MLA_Q_RANK, MLA_HEADS * LANES).astype(BF16)
    wukv = mla_w_ukv.reshape(nl, MLA_KV_RANK, MLA_HEADS, MLA_NOPE + MLA_V)
    knope = pad_lanes(wukv[..., :MLA_NOPE], LANES)
    vv = wukv[..., MLA_NOPE:]
    zv = jnp.zeros_like(vv)
    even = (np.arange(MLA_HEADS) % 2 == 0)[None, None, :, None]
    vpad = jnp.concatenate([jnp.where(even, vv, zv), jnp.where(even, zv, vv)], axis=-1)
    wukv_p = jnp.concatenate([knope.reshape(nl, MLA_KV_RANK, -1),
                              vpad.reshape(nl, MLA_KV_RANK, -1)], axis=-1).astype(BF16)

    a, b = SSD_INNER, SSD_INNER + SWA_WIDTH
    w_out_p = jnp.concatenate([w_out[:, :a], _swa_head_perm(w_out[:, a:b], 1), w_out[:, b:]],
                              axis=1).astype(BF16)

    peak = lambda g: jnp.max(jnp.abs(g.astype(F32)), axis=-1)
    mla_bound = (math.sqrt(MLA_QK) * LOG2E * MLA_BOUND_SLACK) * peak(mla_q_norm) * peak(mla_k_norm)

    two = lambda v: jnp.concatenate([v, v], axis=-1)
    return {
        "mla_bound": mla_bound,
        "ffn1_norm": row(ffn1_norm),
        "ffn2_norm": row(ffn2_norm),
        "w_in": w_in_p,
        "mix_norm": row(mix_norm),
        "conv_w": ssd_conv_w.astype(F32),
        "conv_b": row(ssd_conv_b),
        "dt_bias": pad_lanes(row(ssd_dt_bias), LANES),
        "swa_q_gain": row(two(swa_q_norm)),
        "swa_k_gain": row(two(swa_k_norm)),
        "mla_qlat_norm": row(mla_q_lat_norm),
        "w_uq": wuq,
        "mla_kv_norm": row(mla_kv_norm),
        "w_ukv": wukv_p,
        "mla_q_gain": pad_lanes(row(mla_q_norm), LANES),
        "mla_k_gain": pad_lanes(row(mla_k_norm), LANES),
        "ssd_a_log": pad_lanes(row(ssd_a_log), LANES),
        "ssd_d": row(jnp.repeat(ssd_d, SSD_HEAD_DIM, axis=-1)),
        "ssd_norm": row(ssd_norm),
        "swa_sink": swa_sink.astype(F32),
        "swa_out_norm": row(_swa_head_perm(swa_out_norm, 1)),
        "mla_out_norm": row(mla_out_norm),
        "w_out": w_out_p,
    }


def _tiles(seq):
    pick = lambda pref: math.gcd(seq, pref)
    return {"ffn": pick(1024), "proj": pick(512), "mla_q": pick(256), "mla_k": pick(1024),
            "swa_blocks": pick(16 * SWA_BLOCK) // SWA_BLOCK,
            "ssd_chunks": pick(4 * SSD_CHUNK) // SSD_CHUNK}


def kernel(x, ffn1_norm, ffn1_gate, ffn1_up, ffn1_down, mix_norm, w_in, ssd_conv_w, ssd_conv_b, ssd_dt_bias, ssd_a_log, ssd_d, ssd_norm, swa_q_norm, swa_k_norm, swa_sink, swa_out_norm, mla_q_lat_norm, mla_w_uq, mla_kv_norm, mla_w_ukv, mla_q_norm, mla_k_norm, mla_out_norm, w_out, ffn2_norm, ffn2_gate, ffn2_up, ffn2_down):
    batch, seq, d = x.shape
    assert seq % SSD_CHUNK == 0 and seq % SWA_BLOCK == 0
    assert ffn1_gate.shape[-1] % FFN_CHUNK == 0
    stacked = _prep_params(ffn1_norm, mix_norm, w_in, ssd_conv_w, ssd_conv_b, ssd_dt_bias,
                           ssd_a_log, ssd_d, ssd_norm, swa_q_norm, swa_k_norm, swa_sink,
                           swa_out_norm, mla_q_lat_norm, mla_w_uq, mla_kv_norm, mla_w_ukv,
                           mla_q_norm, mla_k_norm, mla_out_norm, w_out, ffn2_norm)
    tl = _tiles(seq)
    tabs = _rope_tables(seq)
    xf = x.reshape(batch * seq, d).astype(F32)
    for l in range(w_in.shape[0]):
        p = {k: (v, l) for k, v in stacked.items()}
        xf = _ffn(xf, p["ffn1_norm"], l, ffn1_gate, ffn1_up, ffn1_down, tm=tl["ffn"])
        z, xbc, dt, swq, swk, swv, mq, mk, mv = _in_proj(xf, tabs, p, seq, tl["proj"])
        y_ssd = _ssd(xbc, dt, z, p, batch, seq, tl["ssd_chunks"])
        y_swa = _swa(swq, swk, swv, p["swa_sink"], p["swa_out_norm"], batch, seq,
                     tl["swa_blocks"])
        y_mla = _mla(mq, mk, mv, p["mla_bound"], p["mla_out_norm"], batch, seq, tl["mla_q"],
                     tl["mla_k"])
        xf = _ffn(xf, p["ffn2_norm"], l, ffn2_gate, ffn2_up, ffn2_down, tm=tl["ffn"],
                  mixer=(y_ssd, y_swa, y_mla, p["w_out"]))
    return xf.reshape(batch, seq, d).astype(x.dtype)
```

```python
import functools
import math

import jax
import jax.numpy as jnp
import numpy as np
from jax import lax
from jax.experimental import pallas as pl
from jax.experimental.pallas import tpu as pltpu

F32 = jnp.float32
BF16 = jnp.bfloat16

LANES = 128
HALF = LANES // 2
MIB = 1024 * 1024
VMEM_FFN = 56 * MIB
VMEM_IN_PROJ = 48 * MIB
VMEM_SSD = 40 * MIB
VMEM_MLA = 48 * MIB

SSD_HEADS = 8
SSD_HEAD_DIM = 64
SSD_INNER = SSD_HEADS * SSD_HEAD_DIM
SSD_GROUPS = 2
SSD_STATE = 128
SSD_CONV = 5
SSD_CHUNK = 256
SSD_CONV_DIM = SSD_INNER + 2 * SSD_GROUPS * SSD_STATE
SWA_HEADS = 4
SWA_KV_HEADS = 2
SWA_HEAD_DIM = 64
SWA_WINDOW = 128
SWA_BLOCK = 128
SWA_WIDTH = SWA_HEADS * SWA_HEAD_DIM
MLA_HEADS = 4
MLA_Q_RANK = 256
MLA_KV_RANK = 128
MLA_NOPE = 64
MLA_ROPE = 32
MLA_QK = MLA_NOPE + MLA_ROPE
MLA_V = 64
MLA_WIDTH = MLA_HEADS * MLA_V
ROPE_THETA = 10000.0
EPS = 1e-6

C_Z = 0
C_XBC = C_Z + SSD_INNER
C_SWQ = C_XBC + SSD_CONV_DIM
C_SWK = C_SWQ + SWA_WIDTH
C_SWV = C_SWK + LANES
C_MLQ = C_SWV + LANES
C_CKV = C_MLQ + MLA_Q_RANK
C_MISC = C_CKV + MLA_KV_RANK
C_END = C_MISC + LANES
CONV_HALO = 16
FFN_CHUNK = 256
FFN_RESIDUAL_SCALE = 0.5
FFN_LOAD_DEPTH = 4
NEG_INF = float("-inf")
LOG_FLOOR = -1e30
LOG2E = math.log2(math.e)
MLA_BOUND_LIMIT = 48.0
MLA_BOUND_SLACK = 1.02


def _rms(x, w):
    ms = jnp.mean(x * x, axis=-1, keepdims=True)
    return x * lax.rsqrt(ms + EPS) * w


def _silu(x):
    return x / (1.0 + jnp.exp(-x))


def _dot(a, b):
    return jnp.dot(a, b, preferred_element_type=F32)


def _dot_nt(a, b):
    return lax.dot_general(a, b, (((1,), (1,)), ((), ())), preferred_element_type=F32)


def _dot_tn(a, b):
    return lax.dot_general(a, b, (((0,), (0,)), ((), ())), preferred_element_type=F32)


def _lane_iota(shape):
    return lax.broadcasted_iota(jnp.int32, shape, len(shape) - 1)


def _resident(param, single_buffer=False):
    mode = {"pipeline_mode": pl.Buffered(1)} if single_buffer else {}
    if isinstance(param, tuple):
        arr, layer = param
        rest = (0,) * (arr.ndim - 1)
        return arr, pl.BlockSpec((None,) + arr.shape[1:], lambda *_: (layer,) + rest, **mode)
    zeros = (0,) * param.ndim
    return param, pl.BlockSpec(param.shape, lambda *_: zeros, **mode)


def _load_ffn_weights(layer, gate_hbm, up_hbm, down_hbm, wg_ref, wu_ref, wd_ref,
                      col_stage, row_stage, sems):
    jobs = []
    for c in range(wg_ref.shape[1] // FFN_CHUNK):
        lo, hi = c * FFN_CHUNK, (c + 1) * FFN_CHUNK
        jobs.append((gate_hbm.at[layer, :, lo:hi], col_stage, wg_ref, (slice(None), slice(lo, hi)), 1.0))
        jobs.append((up_hbm.at[layer, :, lo:hi], col_stage, wu_ref, (slice(None), slice(lo, hi)), 1.0))
        jobs.append((down_hbm.at[layer, lo:hi, :], row_stage, wd_ref, (slice(lo, hi), slice(None)),
                     FFN_RESIDUAL_SCALE))

    depth = col_stage.shape[0]

    def copy(j):
        src, stage = jobs[j][:2]
        return pltpu.make_async_copy(src, stage.at[j % depth], sems.at[j % depth])

    for j in range(depth - 1):
        copy(j).start()
    for j, (_, stage, dst, where, scale) in enumerate(jobs):
        ahead = j + depth - 1
        if ahead < len(jobs):
            copy(ahead).start()
        copy(j).wait()
        piece = stage[j % depth]
        dst[where] = (piece if scale == 1.0 else piece * scale).astype(BF16)


def _ffn_body(*refs, fuse_out_proj, layer):
    if fuse_out_proj:
        x_ref, ys_ref, yw_ref, ym_ref, wo_ref = refs[:5]
        refs = refs[5:]
    else:
        x_ref = refs[0]
        refs = refs[1:]
    (nw_ref, gate_hbm, up_hbm, down_hbm, o_ref,
     xn_ref, wg_ref, wu_ref, wd_ref, col_stage, row_stage, sems) = refs

    @pl.when(pl.program_id(0) == 0)
    def _():
        _load_ffn_weights(layer, gate_hbm, up_hbm, down_hbm, wg_ref, wu_ref, wd_ref,
                          col_stage, row_stage, sems)

    x = x_ref[...]
    if fuse_out_proj:
        a, b = SSD_INNER, SSD_INNER + SWA_WIDTH
        x = (x + _dot(ys_ref[...], wo_ref[0:a, :]) + _dot(yw_ref[...], wo_ref[a:b, :])
             + _dot(ym_ref[...], wo_ref[b:, :]))
    o_ref[...] = x
    xn_ref[...] = _rms(x, nw_ref[...]).astype(BF16)
    for c in range(wg_ref.shape[1] // FFN_CHUNK):
        cols = slice(c * FFN_CHUNK, (c + 1) * FFN_CHUNK)
        xn = xn_ref[...]
        g = _dot(xn, wg_ref[:, cols])
        u = _dot(xn, wu_ref[:, cols])
        h = (_silu(g) * u).astype(BF16)
        o_ref[...] += _dot(h, wd_ref[cols, :])


def _ffn(x, nw, layer, gate, up, down, tm, mixer=None):
    n, d = x.shape
    f = gate.shape[-1]
    row = lambda w: pl.BlockSpec((tm, w), lambda i: (i, 0))
    args, specs = [x], [row(d)]
    resident = [_resident(nw)]
    if mixer is not None:
        args += list(mixer[:3])
        specs += [row(a.shape[1]) for a in mixer[:3]]
        resident.insert(0, _resident(mixer[3]))
    args += [a for a, _ in resident] + [gate, up, down]
    specs += [s for _, s in resident] + [pl.BlockSpec(memory_space=pl.ANY)] * 3
    return pl.pallas_call(
        functools.partial(_ffn_body, fuse_out_proj=mixer is not None, layer=layer),
        grid=(n // tm,),
        in_specs=specs,
        out_specs=row(d),
        out_shape=jax.ShapeDtypeStruct((n, d), F32),
        scratch_shapes=[pltpu.VMEM((tm, d), BF16),
                        pltpu.VMEM((d, f), BF16), pltpu.VMEM((d, f), BF16),
                        pltpu.VMEM((f, d), BF16),
                        pltpu.VMEM((FFN_LOAD_DEPTH, d, FFN_CHUNK), F32),
                        pltpu.VMEM((FFN_LOAD_DEPTH, FFN_CHUNK, d), F32),
                        pltpu.SemaphoreType.DMA((FFN_LOAD_DEPTH,))],
        compiler_params=pltpu.CompilerParams(
            dimension_semantics=("arbitrary",), vmem_limit_bytes=VMEM_FFN),
        name="ffn",
    )(*args)


def _permute_w_in_body(src_ref, dst_ref):
    n_src, cols = src_ref.shape
    hd = SWA_HEAD_DIM
    dt0 = C_SWQ
    q0 = dt0 + 2 * SSD_HEADS
    rest0 = q0 + SWA_WIDTH
    kr0 = n_src - MLA_ROPE
    zeros = lambda n: jnp.zeros((n, cols), F32)
    pieces = [src_ref[0:dt0, :]]
    pieces += [src_ref[q0 + h * hd:q0 + (h + 1) * hd, :] for h in (0, 2, 1, 3)]
    pieces += [src_ref[rest0:kr0, :],
               src_ref[dt0:q0, :], zeros(MLA_NOPE - 2 * SSD_HEADS),
               src_ref[kr0:n_src, :], zeros(LANES - MLA_QK)]
    dst_ref[...] = jnp.concatenate(pieces, axis=0).T.astype(BF16)


def _permute_w_in(w_in_t, col_block=256):
    nl, n_src, d = w_in_t.shape
    assert n_src + (MLA_NOPE - 2 * SSD_HEADS) + (LANES - MLA_QK) == C_END
    return pl.pallas_call(
        _permute_w_in_body,
        grid=(nl, d // col_block),
        in_specs=[pl.BlockSpec((None, n_src, col_block), lambda l, i: (l, 0, i))],
        out_specs=pl.BlockSpec((None, col_block, C_END), lambda l, i: (l, i, 0)),
        out_shape=jax.ShapeDtypeStruct((nl, d, C_END), BF16),
        compiler_params=pltpu.CompilerParams(dimension_semantics=("parallel", "parallel")),
        name="permute_w_in",
    )(w_in_t)


def _rope(y, c, sa, sb, shift):
    return (y * c + pltpu.roll(y, LANES - shift, 1) * sa + pltpu.roll(y, shift, 1) * sb)


def _head_sum_matrix(width):
    r = lax.broadcasted_iota(jnp.int32, (LANES, LANES), 0)
    c = lax.broadcasted_iota(jnp.int32, (LANES, LANES), 1)
    same = (r < width) if width > HALF else ((r < HALF) == (c < HALF))
    return jnp.where(same, 1.0, 0.0).astype(BF16)


def _head_norm_rope(x, sumsq, gain, c, sa, sb, width, rope_dim, post_scale):
    y = x * lax.rsqrt(sumsq / float(width) + EPS) * gain
    out = _rope(y, c, sa, sb, rope_dim // 2)
    return out if post_scale == 1.0 else out * post_scale


def _inproj_body(x_ref, xp_ref, xnx_ref, tab_ref, w_ref, mixw_ref, convw_ref, convb_ref,
                 dtb_ref, swqg_ref, swkg_ref, qlw_ref, wuq_ref, kvw_ref, wukv_ref,
                 mqg_ref, mkg_ref,
                 z_ref, xbc_ref, dt_ref, swq_ref, swk_ref, swv_ref, mq_ref, mk_ref, mv_ref,
                 *, tiles_per_seq):
    i = pl.program_id(0)
    tm = x_ref.shape[0]
    pos_tile = i % tiles_per_seq
    mixw = mixw_ref[...]
    hn = _rms(x_ref[...], mixw).astype(BF16)
    hp = _rms(xp_ref[...], mixw).astype(BF16)
    hx = _rms(xnx_ref[...], mixw).astype(BF16)
    groups = lambda v: [v[:, g * LANES:(g + 1) * LANES] for g in range(v.shape[1] // LANES)]

    z = _dot(hn, w_ref[:, C_Z:C_XBC])
    conv_w = 2 * LANES
    xbc_parts = []
    for c0 in range(C_XBC, C_SWQ, conv_w):
        wx = w_ref[:, c0:c0 + conv_w]
        xbc_parts.append((_dot(hp, wx), _dot(hn, wx), _dot(hx, wx)))
    ckv, misc = groups(_dot(hn, w_ref[:, C_CKV:C_END]))
    swa_q = groups(_dot(hn, w_ref[:, C_SWQ:C_SWK]))
    swa_k, swa_v = groups(_dot(hn, w_ref[:, C_SWK:C_MLQ]))
    ql = _dot(hn, w_ref[:, C_MLQ:C_CKV])

    q_up = groups(_dot(_rms(ql, qlw_ref[...]).astype(BF16), wuq_ref[...]))
    kv_up = groups(_dot(_rms(ckv, kvw_ref[...]).astype(BF16), wukv_ref[...]))
    lane = _lane_iota(misc.shape)
    kr = jnp.where((lane >= MLA_NOPE) & (lane < MLA_QK), misc, 0.0)
    mla_q = q_up
    mla_k = [k_nope + kr for k_nope in kv_up[:MLA_HEADS]]
    mla_v = kv_up[MLA_HEADS:]

    sum64, sum96 = _head_sum_matrix(SWA_HEAD_DIM), _head_sum_matrix(MLA_QK)
    sumsq = lambda xs, m: [_dot((v * v).astype(BF16), m) for v in xs]
    ss_swa_q, ss_swa_k = sumsq(swa_q, sum64), sumsq([swa_k], sum64)
    ss_mla_q, ss_mla_k = sumsq(mla_q, sum96), sumsq(mla_k, sum96)

    z_ref[...] = z.astype(BF16)

    keep_prev = (pos_tile > 0).astype(F32)
    keep_next = (pos_tile < tiles_per_seq - 1).astype(F32)
    for part, (xbc_prev, xbc_main, xbc_next) in enumerate(xbc_parts):
        cs = slice(part * conv_w, (part + 1) * conv_w)
        padded = jnp.concatenate([xbc_prev * keep_prev, xbc_main, xbc_next * keep_next], axis=0)
        n_pad = padded.shape[0]
        conv = jnp.broadcast_to(convb_ref[:, cs], (tm, conv_w))
        for k in range(SSD_CONV):
            shift = SSD_CONV // 2 - k
            tap = padded if shift == 0 else pltpu.roll(padded, shift % n_pad, 0)
            conv = conv + convw_ref[k:k + 1, cs] * tap[CONV_HALO:CONV_HALO + tm, :]
        xbc_ref[:, cs] = _silu(conv).astype(BF16)

    dtv = misc + dtb_ref[...]
    dt_ref[...] = jnp.maximum(dtv, 0.0) + jnp.log1p(jnp.exp(-jnp.abs(dtv)))

    swa_rope = (tab_ref[0], tab_ref[1], tab_ref[2], SWA_HEAD_DIM, SWA_HEAD_DIM)
    for g, (v, ss) in enumerate(zip(swa_q, ss_swa_q)):
        swq_ref[:, g * LANES:(g + 1) * LANES] = _head_norm_rope(
            v, ss, swqg_ref[...], *swa_rope, SWA_HEAD_DIM ** -0.5 * LOG2E).astype(BF16)
    swk_ref[...] = _head_norm_rope(swa_k, ss_swa_k[0], swkg_ref[...], *swa_rope, 1.0).astype(BF16)
    swv_ref[...] = swa_v.astype(BF16)

    mla_rope = (tab_ref[3], tab_ref[4], tab_ref[5], MLA_QK, MLA_ROPE)
    for h in range(MLA_HEADS):
        sl = slice(h * LANES, (h + 1) * LANES)
        mq_ref[:, sl] = _head_norm_rope(mla_q[h], ss_mla_q[h], mqg_ref[...], *mla_rope,
                                        MLA_QK ** -0.5 * LOG2E).astype(BF16)
        mk_ref[:, sl] = _head_norm_rope(mla_k[h], ss_mla_k[h], mkg_ref[...], *mla_rope,
                                        1.0).astype(BF16)
        ones_lane = HALF if h % 2 == 0 else 0
        mv_ref[:, sl] = jnp.where(lane == ones_lane, 1.0, mla_v[h]).astype(BF16)


def _in_proj(x, tabs, p, seq, tm):
    n, d = x.shape
    tiles_per_seq = seq // tm
    halo_per_tile = tm // CONV_HALO
    n_halo = n // CONV_HALO
    row = lambda w: pl.BlockSpec((tm, w), lambda i: (i, 0))
    in_specs = [
        row(d),
        pl.BlockSpec((CONV_HALO, d), lambda i: (jnp.maximum(i * halo_per_tile - 1, 0), 0)),
        pl.BlockSpec((CONV_HALO, d),
                     lambda i: (jnp.minimum((i + 1) * halo_per_tile, n_halo - 1), 0)),
        pl.BlockSpec((6, tm, LANES), lambda i: (0, i % tiles_per_seq, 0)),
    ]
    names = ("w_in", "mix_norm", "conv_w", "conv_b", "dt_bias", "swa_q_gain", "swa_k_gain",
             "mla_qlat_norm", "w_uq", "mla_kv_norm", "w_ukv", "mla_q_gain", "mla_k_gain")
    resident = [_resident(p[k]) for k in names]
    args = [a for a, _ in resident]
    in_specs += [s for _, s in resident]
    widths = (SSD_INNER, SSD_CONV_DIM, LANES, SWA_WIDTH, LANES, LANES,
              MLA_HEADS * LANES, MLA_HEADS * LANES, MLA_HEADS * LANES)
    dts = (BF16, BF16, F32, BF16, BF16, BF16, BF16, BF16, BF16)
    return pl.pallas_call(
        functools.partial(_inproj_body, tiles_per_seq=tiles_per_seq),
        grid=(n // tm,),
        in_specs=in_specs,
        out_specs=[row(w) for w in widths],
        out_shape=[jax.ShapeDtypeStruct((n, w), t) for w, t in zip(widths, dts)],
        compiler_params=pltpu.CompilerParams(
            dimension_semantics=("parallel",), vmem_limit_bytes=VMEM_IN_PROJ),
        name="in_proj",
    )(x, x, x, tabs, *args)


def _split_bf16(v, n):
    pieces = []
    for _ in range(n - 1):
        p = v.astype(BF16)
        pieces.append(p)
        v = v - p.astype(F32)
    pieces.append(v.astype(BF16))
    return pieces


def _pack_lanes(pieces, width):
    lane = _lane_iota(pieces[0].shape)
    out = jnp.zeros(pieces[0].shape, F32)
    for k, p in enumerate(pieces):
        pf = p.astype(F32)
        if k:
            pf = pltpu.roll(pf, k * width, 1)
        out = jnp.where((lane >= k * width) & (lane < (k + 1) * width), pf, out)
    return out.astype(BF16)


def _ssd_expand_matrices():
    nd = 2 * SSD_HEADS
    rows = np.arange(LANES)[:, None]
    head = np.arange(SSD_INNER)[None, :] // SSD_HEAD_DIM
    mats = []
    for direction in range(2):
        for first_piece in (0, 2):
            d = direction * SSD_HEADS + head
            hit = (rows == first_piece * nd + d) | (rows == (first_piece + 1) * nd + d)
            mats.append(hit)
    return jnp.asarray(np.stack(mats).astype(np.float32), BF16)


def _ssd_body(xbc_ref, dt_ref, z_ref, alog_ref, dskip_ref, nw_ref, expand_ref, o_ref,
              yacc_ref, sf_ref, sb_ref, cols_ref, *, n_chunks):
    pas = pl.program_id(1)
    j = pl.program_id(2)
    t = SSD_CHUNK
    cps = xbc_ref.shape[0] // t
    nh = SSD_HEADS

    gw = SSD_INNER // SSD_GROUPS
    hpg = nh // SSD_GROUPS
    nd = 2 * nh
    lane = _lane_iota((t, LANES))

    def b_of(rows, g):
        return xbc_ref[rows, SSD_INNER + g * SSD_STATE:SSD_INNER + (g + 1) * SSD_STATE]

    def c_of(rows, g):
        return xbc_ref[rows, SSD_INNER + (SSD_GROUPS + g) * SSD_STATE:
                       SSD_INNER + (SSD_GROUPS + g + 1) * SSD_STATE]

    def forward_early(k):
        rows = slice(k * t, (k + 1) * t)
        fwd = lane < nh
        dt = dt_ref[rows, :]
        avec = jnp.where(_lane_iota((1, LANES)) < nd, -jnp.exp(alog_ref[...]), 0.0)
        a = dt * avec
        row = lax.broadcasted_iota(jnp.int32, (t, t), 0)
        col = lax.broadcasted_iota(jnp.int32, (t, t), 1)
        tri = jnp.where(col <= row, 1.0, 0.0).astype(BF16)
        cum = sum(_dot(tri, part) for part in _split_bf16(a, 3))
        excl = cum - a
        tot = cum[t - 1:t, :]

        e_in = jnp.exp(jnp.where(fwd, cum, tot - excl))
        w_st = jnp.exp(jnp.where(fwd, tot - cum, excl)) * dt
        e1, e2 = _split_bf16(e_in, 2)
        w1, w2 = _split_bf16(w_st, 2)
        cols = _pack_lanes([e1, e2, w1, w2], nd)
        cols_ref[j * cps + k] = cols

        r = jnp.where(fwd, cum, -excl) * LOG2E
        cc = r - jnp.maximum(jnp.log(dt), LOG_FLOOR) * LOG2E
        ones = jnp.ones((t, LANES), F32)
        lhs = jnp.where(lane < 3 * nd, _pack_lanes(_split_bf16(r, 3), nd),
                        jnp.where(lane < 6 * nd, ones, 0.0).astype(BF16))
        cc_t = cc.T[0:nd, :]
        c1, c2, c3 = _split_bf16(cc_t, 3)
        rhs = jnp.concatenate(
            [jnp.ones((3 * nd, t), BF16), -c1, -c2, -c3, jnp.zeros((LANES - 6 * nd, t), BF16)],
            axis=0)
        sub = lax.broadcasted_iota(jnp.int32, (LANES, t), 0) % nd

        e_x = _dot(cols, expand_ref[0])
        w_x = _dot(cols, expand_ref[1])
        cbs = [_dot_nt(c_of(rows, g), b_of(rows, g)) for g in range(SSD_GROUPS)]
        ys = []
        for g in range(SSD_GROUPS):
            gs = slice(g * gw, (g + 1) * gw)
            s_in = sf_ref[g]
            xgf = xbc_ref[rows, gs].astype(F32)
            ys.append(dskip_ref[:, gs] * xgf + _dot(c_of(rows, g), s_in.astype(BF16)) * e_x[:, gs])
            xw = (xgf * w_x[:, gs]).astype(BF16)
            sf_ref[g] = s_in * e_x[t - 1:t, gs] + _dot_tn(b_of(rows, g), xw)
        exps = [_dot(lhs, jnp.where(sub == idx, rhs, jnp.zeros_like(rhs))) for idx in range(nd)]
        return cbs, ys, exps

    def forward_late(k, cbs, ys, exps):
        rows = slice(k * t, (k + 1) * t)
        th = t // 2
        quad = lambda v, qi, qj: v[qi * th:(qi + 1) * th, qj * th:(qj + 1) * th]
        qrow = lax.broadcasted_iota(jnp.int32, (th, th), 0)
        qcol = lax.broadcasted_iota(jnp.int32, (th, th), 1)
        on_low = qrow >= qcol
        on_up = qcol >= qrow

        def mixing(h, cb):
            df, db = exps[h], exps[nh + h]
            diag = [quad(cb, q, q) * (jnp.exp2(jnp.where(on_low, quad(df, q, q), NEG_INF))
                                      + jnp.exp2(jnp.where(on_up, quad(db, q, q), NEG_INF)))
                    for q in range(2)]
            upper = quad(cb, 0, 1) * jnp.exp2(quad(db, 0, 1))
            lower = quad(cb, 1, 0) * jnp.exp2(quad(df, 1, 0))
            return jnp.concatenate([jnp.concatenate([diag[0], upper], axis=1),
                                    jnp.concatenate([lower, diag[1]], axis=1)], axis=0)

        glane = _lane_iota((t, gw))
        for g in range(SSD_GROUPS):
            gs = slice(g * gw, (g + 1) * gw)
            xg = xbc_ref[rows, gs]
            y = ys[g]
            for hh in range(hpg):
                m = mixing(g * hpg + hh, cbs[g]).astype(BF16)
                mine = (glane >= hh * SSD_HEAD_DIM) & (glane < (hh + 1) * SSD_HEAD_DIM)
                y = y + _dot(m, jnp.where(mine, xg, jnp.zeros_like(xg)))
            yacc_ref[j * cps + k, :, gs] = y

    @pl.when(pas == 0)
    def _forward():
        @pl.when(j == 0)
        def _():
            sf_ref[...] = jnp.zeros_like(sf_ref)

        early = [forward_early(k) for k in range(cps)]
        for k in range(cps):
            forward_late(k, *early[k])

    @pl.when(pas == 1)
    def _backward():
        @pl.when(j == 0)
        def _():
            sb_ref[...] = jnp.zeros_like(sb_ref)

        for k in reversed(range(cps)):
            rows = slice(k * t, (k + 1) * t)
            c = n_chunks - 1 - j * cps - (cps - 1 - k)
            cols = cols_ref[c]
            e_x = _dot(cols, expand_ref[2])
            w_x = _dot(cols, expand_ref[3])
            gate = _silu(z_ref[rows, :].astype(F32))
            for g in range(SSD_GROUPS):
                gs = slice(g * gw, (g + 1) * gw)
                s_in = sb_ref[g]
                y = yacc_ref[c, :, gs] + _dot(c_of(rows, g), s_in.astype(BF16)) * e_x[:, gs]
                o_ref[rows, gs] = _rms(y * gate[:, gs], nw_ref[:, gs]).astype(BF16)
                xw = (xbc_ref[rows, gs].astype(F32) * w_x[:, gs]).astype(BF16)
                sb_ref[g] = s_in * e_x[0:1, gs] + _dot_tn(b_of(rows, g), xw)


def _ssd(xbc, dt, z, p, batch, seq, cps):
    t = SSD_CHUNK
    nc = seq // t
    ns = nc // cps

    def chunk_idx(b, pas, j):
        return b * ns + j + pas * (ns - 1 - 2 * j)

    def out_idx(b, pas, j):
        return b * ns + (ns - 1) - pas * j

    blk = lambda w, f: pl.BlockSpec((cps * t, w), lambda b, pas, j: (f(b, pas, j), 0))
    resident = [_resident(a) for a in (p["ssd_a_log"], p["ssd_d"], p["ssd_norm"],
                                       _ssd_expand_matrices())]
    state = pltpu.VMEM((SSD_GROUPS, SSD_STATE, SSD_INNER // SSD_GROUPS), F32)
    return pl.pallas_call(
        functools.partial(_ssd_body, n_chunks=nc),
        grid=(batch, 2, ns),
        in_specs=[blk(SSD_CONV_DIM, chunk_idx), blk(LANES, chunk_idx), blk(SSD_INNER, chunk_idx)]
                 + [s for _, s in resident],
        out_specs=blk(SSD_INNER, out_idx),
        out_shape=jax.ShapeDtypeStruct((batch * seq, SSD_INNER), BF16),
        scratch_shapes=[pltpu.VMEM((nc, t, SSD_INNER), F32), state, state,
                        pltpu.VMEM((nc, t, LANES), BF16)],
        compiler_params=pltpu.CompilerParams(
            dimension_semantics=("arbitrary", "arbitrary", "arbitrary"),
            vmem_limit_bytes=VMEM_SSD),
        name="ssd",
    )(xbc, dt, z, *[a for a, _ in resident])


def _swa_body(sink_ref, q_ref, kp_ref, kc_ref, kn_ref, vp_ref, vc_ref, vn_ref, nw_ref, o_ref,
              *, layer):
    assert SWA_WINDOW == SWA_BLOCK
    j = pl.program_id(1)
    blk = SWA_BLOCK
    n_blk = q_ref.shape[0] // blk
    kb = jnp.concatenate([kp_ref[...], kc_ref[...], kn_ref[...]], axis=0)
    vb = jnp.concatenate([vp_ref[...], vc_ref[...], vn_ref[...]], axis=0)
    qi = lax.broadcasted_iota(jnp.int32, (blk, blk), 0)
    kj = lax.broadcasted_iota(jnp.int32, (blk, blk), 1)
    lo = _lane_iota((blk, LANES)) < HALF
    lo_v = _lane_iota(vb.shape) < HALF
    zero_q = jnp.zeros((blk, LANES), BF16)
    lane_v = _lane_iota(vb.shape)
    unit = lambda at: jnp.where(lane_v == at, 1.0, 0.0).astype(BF16)
    v_half = (jnp.where(lo_v, vb, unit(HALF)), jnp.where(lo_v, unit(0), vb))
    first = jnp.where(j == 0, blk, 0)
    last = jnp.where(j == pl.num_programs(1) - 1, blk, 0)
    items = [(t, g, half) for t in range(n_blk) for g in range(SWA_WIDTH // LANES)
             for half in range(2)]
    logits = {}
    for t, g, half in items:
        qg = q_ref[t * blk:(t + 1) * blk, g * LANES:(g + 1) * LANES]
        qm = jnp.where(lo, qg, zero_q) if half == 0 else jnp.where(lo, zero_q, qg)
        logits[t, g, half] = _dot_nt(qm, kb[t * blk:(t + 3) * blk])
    scaled = {}
    for t, g, half in items:
        keep_prev = kj >= (qi + first if t == 0 else qi)
        keep_next = kj <= (qi - last if t == n_blk - 1 else qi)
        s = logits[t, g, half]
        s_prev = jnp.where(keep_prev, s[:, :blk], NEG_INF)
        s_own = s[:, blk:2 * blk]
        s_next = jnp.where(keep_next, s[:, 2 * blk:], NEG_INF)
        sk = sink_ref[layer, g + 2 * half] * LOG2E
        m = jnp.max(jnp.maximum(jnp.maximum(s_prev, s_own), s_next), axis=-1, keepdims=True)
        m = jnp.maximum(m, sk)
        pexp = jnp.concatenate([jnp.exp2(v - m) for v in (s_prev, s_own, s_next)],
                               axis=-1).astype(BF16)
        acc = _dot(pexp, v_half[half][t * blk:(t + 3) * blk])
        ones_lane = HALF if half == 0 else 0
        den = acc[:, ones_lane:ones_lane + 1] + jnp.exp2(sk - m)
        scaled[t, g, half] = acc / den
    for t in range(n_blk):
        y = jnp.concatenate([jnp.where(lo, scaled[t, g, 0], scaled[t, g, 1])
                             for g in range(SWA_WIDTH // LANES)], axis=-1)
        o_ref[t * blk:(t + 1) * blk, :] = _rms(y, nw_ref[...]).astype(BF16)


def _swa(q, k, v, sink, nw, batch, seq, n_blk):
    blk = SWA_BLOCK
    nb = seq // blk
    ns = nb // n_blk
    cur = lambda b, j: (b * ns + j, 0)
    prv = lambda b, j: (b * nb + jnp.maximum(j * n_blk - 1, 0), 0)
    nxt = lambda b, j: (b * nb + jnp.minimum((j + 1) * n_blk, nb - 1), 0)
    edge = lambda f: pl.BlockSpec((blk, LANES), f)
    main = pl.BlockSpec((n_blk * blk, LANES), cur)
    sinks, layer = sink
    nw_arr, nw_spec = _resident(nw)
    return pl.pallas_call(
        functools.partial(_swa_body, layer=layer),
        grid=(batch, ns),
        in_specs=[pl.BlockSpec(memory_space=pltpu.SMEM),
                  pl.BlockSpec((n_blk * blk, SWA_WIDTH), cur),
                  edge(prv), main, edge(nxt), edge(prv), main, edge(nxt), nw_spec],
        out_specs=pl.BlockSpec((n_blk * blk, SWA_WIDTH), cur),
        out_shape=jax.ShapeDtypeStruct((batch * seq, SWA_WIDTH), BF16),
        compiler_params=pltpu.CompilerParams(dimension_semantics=("parallel", "parallel")),
        name="swa",
    )(sinks, q, k, k, k, v, v, v, nw_arr)


def _mla_body(bound_ref, q_ref, k_ref, v_ref, nw_ref, o_ref, *, key_chunk, layer):
    tq = q_ref.shape[0]
    seq = k_ref.shape[0]
    heads = [slice(h * LANES, (h + 1) * LANES) for h in range(MLA_HEADS)]
    chunks = [slice(c * key_chunk, (c + 1) * key_chunk) for c in range(seq // key_chunk)]
    lo = _lane_iota((tq, LANES)) < HALF
    bound = bound_ref[layer]
    use_bound = bound <= MLA_BOUND_LIMIT

    def denom(acc, h):
        return acc[:, HALF:HALF + 1] if h % 2 == 0 else acc[:, 0:1]

    def finish(accs, dens):
        outs = [jnp.where(lo, accs[2 * p] / dens[2 * p], accs[2 * p + 1] / dens[2 * p + 1])
                for p in range(MLA_HEADS // 2)]
        y = jnp.concatenate(outs, axis=-1)
        o_ref[...] = _rms(y, nw_ref[...]).astype(BF16)

    @pl.when(use_bound)
    def _bounded():
        accs = []
        for h, sl in enumerate(heads):
            qh = q_ref[:, sl]
            acc = None
            for ks in chunks:
                pexp = jnp.exp2(_dot_nt(qh, k_ref[ks, sl]) - bound).astype(BF16)
                part = _dot(pexp, v_ref[ks, sl])
                acc = part if acc is None else acc + part
            accs.append(acc)
        finish(accs, [denom(a, h) for h, a in enumerate(accs)])

    @pl.when(jnp.logical_not(use_bound))
    def _online():
        accs, dens = [], []
        for h, sl in enumerate(heads):
            qh = q_ref[:, sl]
            m = jnp.full((tq, 1), NEG_INF, F32)
            acc = jnp.zeros((tq, LANES), F32)
            for ks in chunks:
                s = _dot_nt(qh, k_ref[ks, sl])
                m_new = jnp.maximum(m, jnp.max(s, axis=-1, keepdims=True))
                pexp = jnp.exp2(s - m_new).astype(BF16)
                acc = jnp.exp2(m - m_new) * acc + _dot(pexp, v_ref[ks, sl])
                m = m_new
            accs.append(acc)
            dens.append(denom(acc, h))
        finish(accs, dens)


def _mla(q, k, v, bound, nw, batch, seq, tq, key_chunk):
    nq = seq // tq
    w = MLA_HEADS * LANES
    full = pl.BlockSpec((seq, w), lambda b, i: (b, 0))
    bounds, layer = bound
    nw, nw_spec = _resident(nw)
    return pl.pallas_call(
        functools.partial(_mla_body, key_chunk=key_chunk, layer=layer),
        grid=(batch, nq),
        in_specs=[pl.BlockSpec(memory_space=pltpu.SMEM),
                  pl.BlockSpec((tq, w), lambda b, i: (b * nq + i, 0)), full, full, nw_spec],
        out_specs=pl.BlockSpec((tq, MLA_WIDTH), lambda b, i: (b * nq + i, 0)),
        out_shape=jax.ShapeDtypeStruct((batch * seq, MLA_WIDTH), BF16),
        compiler_params=pltpu.CompilerParams(
            dimension_semantics=("parallel", "parallel"),
            vmem_limit_bytes=VMEM_MLA),
        name="mla",
    )(bounds, q, k, v, nw)


def _rope_tables(seq):
    def angles(dim):
        inv = 1.0 / np.power(ROPE_THETA, np.arange(0, dim, 2, dtype=np.float64) / dim)
        return np.arange(seq, dtype=np.float64)[:, None] * inv[None, :]

    a64 = angles(SWA_HEAD_DIM)
    c, s = np.cos(a64), np.sin(a64)
    zero = np.zeros_like(c)
    c64 = np.concatenate([c, c, c, c], axis=-1)
    sa64 = np.concatenate([-s, zero, -s, zero], axis=-1)
    sb64 = np.concatenate([zero, s, zero, s], axis=-1)
    a32 = angles(MLA_ROPE)
    c, s = np.cos(a32), np.sin(a32)
    zero = np.zeros_like(c)
    ones = np.ones((seq, MLA_NOPE))
    pad = np.zeros((seq, LANES - MLA_QK))
    zn = np.zeros((seq, MLA_NOPE))
    cm = np.concatenate([ones, c, c, pad], axis=-1)
    sam = np.concatenate([zn, -s, zero, pad], axis=-1)
    sbm = np.concatenate([zn, zero, s, pad], axis=-1)
    return jnp.asarray(np.stack([c64, sa64, sb64, cm, sam, sbm]).astype(np.float32))


def _swa_head_perm(t, axis):
    parts = jnp.split(t, SWA_HEADS, axis=axis)
    return jnp.concatenate([parts[0], parts[2], parts[1], parts[3]], axis=axis)


def _prep_params(ffn1_norm, mix_norm, w_in, ssd_conv_w, ssd_conv_b, ssd_dt_bias, ssd_a_log,
                 ssd_d, ssd_norm, swa_q_norm, swa_k_norm, swa_sink, swa_out_norm,
                 mla_q_lat_norm, mla_w_uq, mla_kv_norm, mla_w_ukv, mla_q_norm, mla_k_norm,
                 mla_out_norm, w_out, ffn2_norm):
    nl, d = w_in.shape[:2]
    row = lambda v: v.reshape(nl, 1, -1).astype(F32)
    pad_lanes = lambda v, n: jnp.pad(v, [(0, 0)] * (v.ndim - 1) + [(0, n - v.shape[-1])])

    w_in_p = _permute_w_in(jnp.swapaxes(w_in, 1, 2))

    wuq = mla_w_uq.reshape(nl, MLA_Q_RANK, MLA_HEADS, MLA_QK)
    wuq = pad_lanes(wuq, LANES).reshape(nl, MLA_Q_RANK, MLA_HEADS * LANES).astype(BF16)
    wukv = mla_w_ukv.reshape(nl, MLA_KV_RANK, MLA_HEADS, MLA_NOPE + MLA_V)
    knope = pad_lanes(wukv[..., :MLA_NOPE], LANES)
    vv = wukv[..., MLA_NOPE:]
    zv = jnp.zeros_like(vv)
    even = (np.arange(MLA_HEADS) % 2 == 0)[None, None, :, None]
    vpad = jnp.concatenate([jnp.where(even, vv, zv), jnp.where(even, zv, vv)], axis=-1)
    wukv_p = jnp.concatenate([knope.reshape(nl, MLA_KV_RANK, -1),
                              vpad.reshape(nl, MLA_KV_RANK, -1)], axis=-1).astype(BF16)

    a, b = SSD_INNER, SSD_INNER + SWA_WIDTH
    w_out_p = jnp.concatenate([w_out[:, :a], _swa_head_perm(w_out[:, a:b], 1), w_out[:, b:]],
                              axis=1).astype(BF16)

    peak = lambda g: jnp.max(jnp.abs(g.astype(F32)), axis=-1)
    mla_bound = (math.sqrt(MLA_QK) * LOG2E * MLA_BOUND_SLACK) * peak(mla_q_norm) * peak(mla_k_norm)

    two = lambda v: jnp.concatenate([v, v], axis=-1)
    return {
        "mla_bound": mla_bound,
        "ffn1_norm": row(ffn1_norm),
        "ffn2_norm": row(ffn2_norm),
        "w_in": w_in_p,
        "mix_norm": row(mix_norm),
        "conv_w": ssd_conv_w.astype(F32),
        "conv_b": row(ssd_conv_b),
        "dt_bias": pad_lanes(row(ssd_dt_bias), LANES),
        "swa_q_gain": row(two(swa_q_norm)),
        "swa_k_gain": row(two(swa_k_norm)),
        "mla_qlat_norm": row(mla_q_lat_norm),
        "w_uq": wuq,
        "mla_kv_norm": row(mla_kv_norm),
        "w_ukv": wukv_p,
        "mla_q_gain": pad_lanes(row(mla_q_norm), LANES),
        "mla_k_gain": pad_lanes(row(mla_k_norm), LANES),
        "ssd_a_log": pad_lanes(row(ssd_a_log), LANES),
        "ssd_d": row(jnp.repeat(ssd_d, SSD_HEAD_DIM, axis=-1)),
        "ssd_norm": row(ssd_norm),
        "swa_sink": swa_sink.astype(F32),
        "swa_out_norm": row(_swa_head_perm(swa_out_norm, 1)),
        "mla_out_norm": row(mla_out_norm),
        "w_out": w_out_p,
    }


def _tiles(seq):
    pick = lambda pref: math.gcd(seq, pref)
    return {"ffn": pick(1024), "proj": pick(512), "mla_q": pick(256), "mla_k": pick(1024),
            "swa_blocks": pick(16 * SWA_BLOCK) // SWA_BLOCK,
            "ssd_chunks": pick(4 * SSD_CHUNK) // SSD_CHUNK}


def kernel(x, ffn1_norm, ffn1_gate, ffn1_up, ffn1_down, mix_norm, w_in, ssd_conv_w, ssd_conv_b, ssd_dt_bias, ssd_a_log, ssd_d, ssd_norm, swa_q_norm, swa_k_norm, swa_sink, swa_out_norm, mla_q_lat_norm, mla_w_uq, mla_kv_norm, mla_w_ukv, mla_q_norm, mla_k_norm, mla_out_norm, w_out, ffn2_norm, ffn2_gate, ffn2_up, ffn2_down):
    batch, seq, d = x.shape
    assert seq % SSD_CHUNK == 0 and seq % SWA_BLOCK == 0
    assert ffn1_gate.shape[-1] % FFN_CHUNK == 0
    stacked = _prep_params(ffn1_norm, mix_norm, w_in, ssd_conv_w, ssd_conv_b, ssd_dt_bias,
                           ssd_a_log, ssd_d, ssd_norm, swa_q_norm, swa_k_norm, swa_sink,
                           swa_out_norm, mla_q_lat_norm, mla_w_uq, mla_kv_norm, mla_w_ukv,
                           mla_q_norm, mla_k_norm, mla_out_norm, w_out, ffn2_norm)
    tl = _tiles(seq)
    tabs = _rope_tables(seq)
    xf = x.reshape(batch * seq, d).astype(F32)
    for l in range(w_in.shape[0]):
        p = {k: (v, l) for k, v in stacked.items()}
        xf = _ffn(xf, p["ffn1_norm"], l, ffn1_gate, ffn1_up, ffn1_down, tm=tl["ffn"])
        z, xbc, dt, swq, swk, swv, mq, mk, mv = _in_proj(xf, tabs, p, seq, tl["proj"])
        y_ssd = _ssd(xbc, dt, z, p, batch, seq, tl["ssd_chunks"])
        y_swa = _swa(swq, swk, swv, p["swa_sink"], p["swa_out_norm"], batch, seq,
                     tl["swa_blocks"])
        y_mla = _mla(mq, mk, mv, p["mla_bound"], p["mla_out_norm"], batch, seq, tl["mla_q"],
                     tl["mla_k"])
        xf = _ffn(xf, p["ffn2_norm"], l, ffn2_gate, ffn2_up, ffn2_down, tm=tl["ffn"],
                  mixer=(y_ssd, y_swa, y_mla, p["w_out"]))
    return xf.reshape(batch, seq, d).astype(x.dtype)
```

```python
import functools
import math

import jax
import jax.numpy as jnp
import numpy as np
from jax import lax
from jax.experimental import pallas as pl
from jax.experimental.pallas import tpu as pltpu

F32 = jnp.float32
BF16 = jnp.bfloat16

LANES = 128
HALF = LANES // 2
MIB = 1024 * 1024
VMEM_FFN = 56 * MIB
VMEM_IN_PROJ = 48 * MIB
VMEM_SSD = 40 * MIB
VMEM_MLA = 48 * MIB

SSD_HEADS = 8
SSD_HEAD_DIM = 64
SSD_INNER = SSD_HEADS * SSD_HEAD_DIM
SSD_GROUPS = 2
SSD_STATE = 128
SSD_CONV = 5
SSD_CHUNK = 256
SSD_CONV_DIM = SSD_INNER + 2 * SSD_GROUPS * SSD_STATE
SWA_HEADS = 4
SWA_KV_HEADS = 2
SWA_HEAD_DIM = 64
SWA_WINDOW = 128
SWA_BLOCK = 128
SWA_WIDTH = SWA_HEADS * SWA_HEAD_DIM
MLA_HEADS = 4
MLA_Q_RANK = 256
MLA_KV_RANK = 128
MLA_NOPE = 64
MLA_ROPE = 32
MLA_QK = MLA_NOPE + MLA_ROPE
MLA_V = 64
MLA_WIDTH = MLA_HEADS * MLA_V
ROPE_THETA = 10000.0
EPS = 1e-6

C_Z = 0
C_XBC = C_Z + SSD_INNER
C_SWQ = C_XBC + SSD_CONV_DIM
C_SWK = C_SWQ + SWA_WIDTH
C_SWV = C_SWK + LANES
C_MLQ = C_SWV + LANES
C_CKV = C_MLQ + MLA_Q_RANK
C_MISC = C_CKV + MLA_KV_RANK
C_END = C_MISC + LANES
CONV_HALO = 16
FFN_CHUNK = 256
FFN_RESIDUAL_SCALE = 0.5
FFN_LOAD_DEPTH = 4
NEG_INF = float("-inf")
LOG_FLOOR = -1e30
LOG2E = math.log2(math.e)
MLA_BOUND_LIMIT = 48.0
MLA_BOUND_SLACK = 1.02


def _rms(x, w):
    ms = jnp.mean(x * x, axis=-1, keepdims=True)
    return x * lax.rsqrt(ms + EPS) * w


def _silu(x):
    return x / (1.0 + jnp.exp(-x))


def _dot(a, b):
    return jnp.dot(a, b, preferred_element_type=F32)


def _dot_nt(a, b):
    return lax.dot_general(a, b, (((1,), (1,)), ((), ())), preferred_element_type=F32)


def _dot_tn(a, b):
    return lax.dot_general(a, b, (((0,), (0,)), ((), ())), preferred_element_type=F32)


def _lane_iota(shape):
    return lax.broadcasted_iota(jnp.int32, shape, len(shape) - 1)


def _resident(param, single_buffer=False):
    mode = {"pipeline_mode": pl.Buffered(1)} if single_buffer else {}
    if isinstance(param, tuple):
        arr, layer = param
        rest = (0,) * (arr.ndim - 1)
        return arr, pl.BlockSpec((None,) + arr.shape[1:], lambda *_: (layer,) + rest, **mode)
    zeros = (0,) * param.ndim
    return param, pl.BlockSpec(param.shape, lambda *_: zeros, **mode)


def _load_ffn_weights(layer, gate_hbm, up_hbm, down_hbm, wg_ref, wu_ref, wd_ref,
                      col_stage, row_stage, sems):
    jobs = []
    for c in range(wg_ref.shape[1] // FFN_CHUNK):
        lo, hi = c * FFN_CHUNK, (c + 1) * FFN_CHUNK
        jobs.append((gate_hbm.at[layer, :, lo:hi], col_stage, wg_ref, (slice(None), slice(lo, hi)), 1.0))
        jobs.append((up_hbm.at[layer, :, lo:hi], col_stage, wu_ref, (slice(None), slice(lo, hi)), 1.0))
        jobs.append((down_hbm.at[layer, lo:hi, :], row_stage, wd_ref, (slice(lo, hi), slice(None)),
                     FFN_RESIDUAL_SCALE))

    depth = col_stage.shape[0]

    def copy(j):
        src, stage = jobs[j][:2]
        return pltpu.make_async_copy(src, stage.at[j % depth], sems.at[j % depth])

    for j in range(depth - 1):
        copy(j).start()
    for j, (_, stage, dst, where, scale) in enumerate(jobs):
        ahead = j + depth - 1
        if ahead < len(jobs):
            copy(ahead).start()
        copy(j).wait()
        piece = stage[j % depth]
        dst[where] = (piece if scale == 1.0 else piece * scale).astype(BF16)


def _ffn_body(*refs, fuse_out_proj, layer):
    if fuse_out_proj:
        x_ref, ys_ref, yw_ref, ym_ref, wo_ref = refs[:5]
        refs = refs[5:]
    else:
        x_ref = refs[0]
        refs = refs[1:]
    (nw_ref, gate_hbm, up_hbm, down_hbm, o_ref,
     xn_ref, wg_ref, wu_ref, wd_ref, col_stage, row_stage, sems) = refs

    @pl.when(pl.program_id(0) == 0)
    def _():
        _load_ffn_weights(layer, gate_hbm, up_hbm, down_hbm, wg_ref, wu_ref, wd_ref,
                          col_stage, row_stage, sems)

    x = x_ref[...]
    if fuse_out_proj:
        a, b = SSD_INNER, SSD_INNER + SWA_WIDTH
        x = (x + _dot(ys_ref[...], wo_ref[0:a, :]) + _dot(yw_ref[...], wo_ref[a:b, :])
             + _dot(ym_ref[...], wo_ref[b:, :]))
    o_ref[...] = x
    xn_ref[...] = _rms(x, nw_ref[...]).astype(BF16)
    for c in range(wg_ref.shape[1] // FFN_CHUNK):
        cols = slice(c * FFN_CHUNK, (c + 1) * FFN_CHUNK)
        xn = xn_ref[...]
        g = _dot(xn, wg_ref[:, cols])
        u = _dot(xn, wu_ref[:, cols])
        h = (_silu(g) * u).astype(BF16)
        o_ref[...] += _dot(h, wd_ref[cols, :])


def _ffn(x, nw, layer, gate, up, down, tm, mixer=None):
    n, d = x.shape
    f = gate.shape[-1]
    row = lambda w: pl.BlockSpec((tm, w), lambda i: (i, 0))
    args, specs = [x], [row(d)]
    resident = [_resident(nw)]
    if mixer is not None:
        args += list(mixer[:3])
        specs += [row(a.shape[1]) for a in mixer[:3]]
        resident.insert(0, _resident(mixer[3]))
    args += [a for a, _ in resident] + [gate, up, down]
    specs += [s for _, s in resident] + [pl.BlockSpec(memory_space=pl.ANY)] * 3
    return pl.pallas_call(
        functools.partial(_ffn_body, fuse_out_proj=mixer is not None, layer=layer),
        grid=(n // tm,),
        in_specs=specs,
        out_specs=row(d),
        out_shape=jax.ShapeDtypeStruct((n, d), F32),
        scratch_shapes=[pltpu.VMEM((tm, d), BF16),
                        pltpu.VMEM((d, f), BF16), pltpu.VMEM((d, f), BF16),
                        pltpu.VMEM((f, d), BF16),
                        pltpu.VMEM((FFN_LOAD_DEPTH, d, FFN_CHUNK), F32),
                        pltpu.VMEM((FFN_LOAD_DEPTH, FFN_CHUNK, d), F32),
                        pltpu.SemaphoreType.DMA((FFN_LOAD_DEPTH,))],
        compiler_params=pltpu.CompilerParams(
            dimension_semantics=("arbitrary",), vmem_limit_bytes=VMEM_FFN),
        name="ffn",
    )(*args)


def _permute_w_in_body(src_ref, dst_ref):
    n_src, cols = src_ref.shape
    hd = SWA_HEAD_DIM
    dt0 = C_SWQ
    q0 = dt0 + 2 * SSD_HEADS
    rest0 = q0 + SWA_WIDTH
    kr0 = n_src - MLA_ROPE
    zeros = lambda n: jnp.zeros((n, cols), F32)
    pieces = [src_ref[0:dt0, :]]
    pieces += [src_ref[q0 + h * hd:q0 + (h + 1) * hd, :] for h in (0, 2, 1, 3)]
    pieces += [src_ref[rest0:kr0, :],
               src_ref[dt0:q0, :], zeros(MLA_NOPE - 2 * SSD_HEADS),
               src_ref[kr0:n_src, :], zeros(LANES - MLA_QK)]
    dst_ref[...] = jnp.concatenate(pieces, axis=0).T.astype(BF16)


def _permute_w_in(w_in_t, col_block=256):
    nl, n_src, d = w_in_t.shape
    assert n_src + (MLA_NOPE - 2 * SSD_HEADS) + (LANES - MLA_QK) == C_END
    return pl.pallas_call(
        _permute_w_in_body,
        grid=(nl, d // col_block),
        in_specs=[pl.BlockSpec((None, n_src, col_block), lambda l, i: (l, 0, i))],
        out_specs=pl.BlockSpec((None, col_block, C_END), lambda l, i: (l, i, 0)),
        out_shape=jax.ShapeDtypeStruct((nl, d, C_END), BF16),
        compiler_params=pltpu.CompilerParams(dimension_semantics=("parallel", "parallel")),
        name="permute_w_in",
    )(w_in_t)


def _rope(y, c, sa, sb, shift):
    return (y * c + pltpu.roll(y, LANES - shift, 1) * sa + pltpu.roll(y, shift, 1) * sb)


def _head_sum_matrix(width):
    r = lax.broadcasted_iota(jnp.int32, (LANES, LANES), 0)
    c = lax.broadcasted_iota(jnp.int32, (LANES, LANES), 1)
    same = (r < width) if width > HALF else ((r < HALF) == (c < HALF))
    return jnp.where(same, 1.0, 0.0).astype(BF16)


def _head_norm_rope(x, sumsq, gain, c, sa, sb, width, rope_dim, post_scale):
    y = x * lax.rsqrt(sumsq / float(width) + EPS) * gain
    out = _rope(y, c, sa, sb, rope_dim // 2)
    return out if post_scale == 1.0 else out * post_scale


def _inproj_body(x_ref, xp_ref, xnx_ref, tab_ref, w_ref, mixw_ref, convw_ref, convb_ref,
                 dtb_ref, swqg_ref, swkg_ref, qlw_ref, wuq_ref, kvw_ref, wukv_ref,
                 mqg_ref, mkg_ref, kb_ref,
                 z_ref, xbc_ref, dt_ref, swq_ref, swk_ref, swv_ref, mq_ref, mk_ref, mv_ref,
                 *, tiles_per_seq):
    i = pl.program_id(0)
    tm = x_ref.shape[0]
    pos_tile = i % tiles_per_seq
    mixw = mixw_ref[...]
    hn = _rms(x_ref[...], mixw).astype(BF16)
    hp = _rms(xp_ref[...], mixw).astype(BF16)
    hx = _rms(xnx_ref[...], mixw).astype(BF16)
    groups = lambda v: [v[:, g * LANES:(g + 1) * LANES] for g in range(v.shape[1] // LANES)]

    z = _dot(hn, w_ref[:, C_Z:C_XBC])
    conv_w = 2 * LANES
    xbc_parts = []
    for c0 in range(C_XBC, C_SWQ, conv_w):
        wx = w_ref[:, c0:c0 + conv_w]
        xbc_parts.append((_dot(hp, wx), _dot(hn, wx), _dot(hx, wx)))
    ckv, misc = groups(_dot(hn, w_ref[:, C_CKV:C_END]))
    swa_q = groups(_dot(hn, w_ref[:, C_SWQ:C_SWK]))
    swa_k, swa_v = groups(_dot(hn, w_ref[:, C_SWK:C_MLQ]))
    ql = _dot(hn, w_ref[:, C_MLQ:C_CKV])

    q_up = groups(_dot(_rms(ql, qlw_ref[...]).astype(BF16), wuq_ref[...]))
    kv_up = groups(_dot(_rms(ckv, kvw_ref[...]).astype(BF16), wukv_ref[...]))
    lane = _lane_iota(misc.shape)
    kr = jnp.where((lane >= MLA_NOPE) & (lane < MLA_QK), misc, 0.0)
    mla_q = q_up
    mla_k = [k_nope + kr for k_nope in kv_up[:MLA_HEADS]]
    mla_v = kv_up[MLA_HEADS:]

    sum64, sum96 = _head_sum_matrix(SWA_HEAD_DIM), _head_sum_matrix(MLA_QK)
    sumsq = lambda xs, m: [_dot((v * v).astype(BF16), m) for v in xs]
    ss_swa_q, ss_swa_k = sumsq(swa_q, sum64), sumsq([swa_k], sum64)
    ss_mla_q, ss_mla_k = sumsq(mla_q, sum96), sumsq(mla_k, sum96)

    z_ref[...] = z.astype(BF16)

    keep_prev = (pos_tile > 0).astype(F32)
    keep_next = (pos_tile < tiles_per_seq - 1).astype(F32)
    for part, (xbc_prev, xbc_main, xbc_next) in enumerate(xbc_parts):
        cs = slice(part * conv_w, (part + 1) * conv_w)
        padded = jnp.concatenate([xbc_prev * keep_prev, xbc_main, xbc_next * keep_next], axis=0)
        n_pad = padded.shape[0]
        conv = jnp.broadcast_to(convb_ref[:, cs], (tm, conv_w))
        for k in range(SSD_CONV):
            shift = SSD_CONV // 2 - k
            tap = padded if shift == 0 else pltpu.roll(padded, shift % n_pad, 0)
            conv = conv + convw_ref[k:k + 1, cs] * tap[CONV_HALO:CONV_HALO + tm, :]
        xbc_ref[:, cs] = _silu(conv).astype(BF16)

    dtv = misc + dtb_ref[...]
    dt_ref[...] = jnp.maximum(dtv, 0.0) + jnp.log1p(jnp.exp(-jnp.abs(dtv)))

    swa_rope = (tab_ref[0], tab_ref[1], tab_ref[2], SWA_HEAD_DIM, SWA_HEAD_DIM)
    for g, (v, ss) in enumerate(zip(swa_q, ss_swa_q)):
        swq_ref[:, g * LANES:(g + 1) * LANES] = _head_norm_rope(
            v, ss, swqg_ref[...], *swa_rope, SWA_HEAD_DIM ** -0.5 * LOG2E).astype(BF16)
    swk_ref[...] = _head_norm_rope(swa_k, ss_swa_k[0], swkg_ref[...], *swa_rope, 1.0).astype(BF16)
    swv_ref[...] = swa_v.astype(BF16)

    mla_rope = (tab_ref[3], tab_ref[4], tab_ref[5], MLA_QK, MLA_ROPE)
    for h in range(MLA_HEADS):
        sl = slice(h * LANES, (h + 1) * LANES)
        qh = _head_norm_rope(mla_q[h], ss_mla_q[h], mqg_ref[...], *mla_rope,
                             MLA_QK ** -0.5 * LOG2E)
        mq_ref[:, sl] = jnp.where(lane == MLA_QK, 1.0, qh).astype(BF16)
        kh = _head_norm_rope(mla_k[h], ss_mla_k[h], mkg_ref[...], *mla_rope, 1.0)
        mk_ref[:, sl] = (kh + kb_ref[...]).astype(BF16)
        ones_lane = HALF if h % 2 == 0 else 0
        mv_ref[:, sl] = jnp.where(lane == ones_lane, 1.0, mla_v[h]).astype(BF16)


def _in_proj(x, tabs, p, seq, tm):
    n, d = x.shape
    tiles_per_seq = seq // tm
    halo_per_tile = tm // CONV_HALO
    n_halo = n // CONV_HALO
    row = lambda w: pl.BlockSpec((tm, w), lambda i: (i, 0))
    in_specs = [
        row(d),
        pl.BlockSpec((CONV_HALO, d), lambda i: (jnp.maximum(i * halo_per_tile - 1, 0), 0)),
        pl.BlockSpec((CONV_HALO, d),
                     lambda i: (jnp.minimum((i + 1) * halo_per_tile, n_halo - 1), 0)),
        pl.BlockSpec((6, tm, LANES), lambda i: (0, i % tiles_per_seq, 0)),
    ]
    names = ("w_in", "mix_norm", "conv_w", "conv_b", "dt_bias", "swa_q_gain", "swa_k_gain",
             "mla_qlat_norm", "w_uq", "mla_kv_norm", "w_ukv", "mla_q_gain", "mla_k_gain", "mla_k_bias")
    resident = [_resident(p[k]) for k in names]
    args = [a for a, _ in resident]
    in_specs += [s for _, s in resident]
    widths = (SSD_INNER, SSD_CONV_DIM, LANES, SWA_WIDTH, LANES, LANES,
              MLA_HEADS * LANES, MLA_HEADS * LANES, MLA_HEADS * LANES)
    dts = (BF16, BF16, F32, BF16, BF16, BF16, BF16, BF16, BF16)
    return pl.pallas_call(
        functools.partial(_inproj_body, tiles_per_seq=tiles_per_seq),
        grid=(n // tm,),
        in_specs=in_specs,
        out_specs=[row(w) for w in widths],
        out_shape=[jax.ShapeDtypeStruct((n, w), t) for w, t in zip(widths, dts)],
        compiler_params=pltpu.CompilerParams(
            dimension_semantics=("parallel",), vmem_limit_bytes=VMEM_IN_PROJ),
        name="in_proj",
    )(x, x, x, tabs, *args)


def _split_bf16(v, n):
    pieces = []
    for _ in range(n - 1):
        p = v.astype(BF16)
        pieces.append(p)
        v = v - p.astype(F32)
    pieces.append(v.astype(BF16))
    return pieces


def _pack_lanes(pieces, width):
    lane = _lane_iota(pieces[0].shape)
    out = jnp.zeros(pieces[0].shape, F32)
    for k, p in enumerate(pieces):
        pf = p.astype(F32)
        if k:
            pf = pltpu.roll(pf, k * width, 1)
        out = jnp.where((lane >= k * width) & (lane < (k + 1) * width), pf, out)
    return out.astype(BF16)


def _ssd_expand_matrices():
    nd = 2 * SSD_HEADS
    rows = np.arange(LANES)[:, None]
    head = np.arange(SSD_INNER)[None, :] // SSD_HEAD_DIM
    mats = []
    for direction in range(2):
        for first_piece in (0, 2):
            d = direction * SSD_HEADS + head
            hit = (rows == first_piece * nd + d) | (rows == (first_piece + 1) * nd + d)
            mats.append(hit)
    return jnp.asarray(np.stack(mats).astype(np.float32), BF16)


def _ssd_body(xbc_ref, dt_ref, z_ref, alog_ref, dskip_ref, nw_ref, expand_ref, o_ref,
              yacc_ref, sf_ref, sb_ref, cols_ref, *, n_chunks):
    pas = pl.program_id(1)
    j = pl.program_id(2)
    t = SSD_CHUNK
    cps = xbc_ref.shape[0] // t
    nh = SSD_HEADS

    gw = SSD_INNER // SSD_GROUPS
    hpg = nh // SSD_GROUPS
    nd = 2 * nh
    lane = _lane_iota((t, LANES))

    def b_of(rows, g):
        return xbc_ref[rows, SSD_INNER + g * SSD_STATE:SSD_INNER + (g + 1) * SSD_STATE]

    def c_of(rows, g):
        return xbc_ref[rows, SSD_INNER + (SSD_GROUPS + g) * SSD_STATE:
                       SSD_INNER + (SSD_GROUPS + g + 1) * SSD_STATE]

    def forward_early(k):
        rows = slice(k * t, (k + 1) * t)
        fwd = lane < nh
        dt = dt_ref[rows, :]
        avec = jnp.where(_lane_iota((1, LANES)) < nd, -jnp.exp(alog_ref[...]), 0.0)
        a = dt * avec
        row = lax.broadcasted_iota(jnp.int32, (t, t), 0)
        col = lax.broadcasted_iota(jnp.int32, (t, t), 1)
        tri = jnp.where(col <= row, 1.0, 0.0).astype(BF16)
        cum = sum(_dot(tri, part) for part in _split_bf16(a, 3))
        excl = cum - a
        tot = cum[t - 1:t, :]

        e_in = jnp.exp(jnp.where(fwd, cum, tot - excl))
        w_st = jnp.exp(jnp.where(fwd, tot - cum, excl)) * dt
        e1, e2 = _split_bf16(e_in, 2)
        w1, w2 = _split_bf16(w_st, 2)
        cols = _pack_lanes([e1, e2, w1, w2], nd)
        cols_ref[j * cps + k] = cols

        r = jnp.where(fwd, cum, -excl) * LOG2E
        cc = r - jnp.maximum(jnp.log(dt), LOG_FLOOR) * LOG2E
        ones = jnp.ones((t, LANES), F32)
        lhs = jnp.where(lane < 3 * nd, _pack_lanes(_split_bf16(r, 3), nd),
                        jnp.where(lane < 6 * nd, ones, 0.0).astype(BF16))
        cc_t = cc.T[0:nd, :]
        c1, c2, c3 = _split_bf16(cc_t, 3)
        rhs = jnp.concatenate(
            [jnp.ones((3 * nd, t), BF16), -c1, -c2, -c3, jnp.zeros((LANES - 6 * nd, t), BF16)],
            axis=0)
        sub = lax.broadcasted_iota(jnp.int32, (LANES, t), 0) % nd

        e_x = _dot(cols, expand_ref[0])
        w_x = _dot(cols, expand_ref[1])
        cbs = [_dot_nt(c_of(rows, g), b_of(rows, g)) for g in range(SSD_GROUPS)]
        ys = []
        for g in range(SSD_GROUPS):
            gs = slice(g * gw, (g + 1) * gw)
            s_in = sf_ref[g]
            xgf = xbc_ref[rows, gs].astype(F32)
            ys.append(dskip_ref[:, gs] * xgf + _dot(c_of(rows, g), s_in.astype(BF16)) * e_x[:, gs])
            xw = (xgf * w_x[:, gs]).astype(BF16)
            sf_ref[g] = s_in * e_x[t - 1:t, gs] + _dot_tn(b_of(rows, g), xw)
        exps = [_dot(lhs, jnp.where(sub == idx, rhs, jnp.zeros_like(rhs))) for idx in range(nd)]
        return cbs, ys, exps

    def forward_late(k, cbs, ys, exps):
        rows = slice(k * t, (k + 1) * t)
        th = t // 2
        quad = lambda v, qi, qj: v[qi * th:(qi + 1) * th, qj * th:(qj + 1) * th]
        qrow = lax.broadcasted_iota(jnp.int32, (th, th), 0)
        qcol = lax.broadcasted_iota(jnp.int32, (th, th), 1)
        on_low = qrow >= qcol
        on_up = qcol >= qrow

        def mixing(h, cb):
            df, db = exps[h], exps[nh + h]
            diag = [quad(cb, q, q) * (jnp.exp2(jnp.where(on_low, quad(df, q, q), NEG_INF))
                                      + jnp.exp2(jnp.where(on_up, quad(db, q, q), NEG_INF)))
                    for q in range(2)]
            upper = quad(cb, 0, 1) * jnp.exp2(quad(db, 0, 1))
            lower = quad(cb, 1, 0) * jnp.exp2(quad(df, 1, 0))
            return jnp.concatenate([jnp.concatenate([diag[0], upper], axis=1),
                                    jnp.concatenate([lower, diag[1]], axis=1)], axis=0)

        glane = _lane_iota((t, gw))
        for g in range(SSD_GROUPS):
            gs = slice(g * gw, (g + 1) * gw)
            xg = xbc_ref[rows, gs]
            y = ys[g]
            for hh in range(hpg):
                m = mixing(g * hpg + hh, cbs[g]).astype(BF16)
                mine = (glane >= hh * SSD_HEAD_DIM) & (glane < (hh + 1) * SSD_HEAD_DIM)
                y = y + _dot(m, jnp.where(mine, xg, jnp.zeros_like(xg)))
            yacc_ref[j * cps + k, :, gs] = y

    @pl.when(pas == 0)
    def _forward():
        @pl.when(j == 0)
        def _():
            sf_ref[...] = jnp.zeros_like(sf_ref)

        early = [forward_early(k) for k in range(cps)]
        for k in range(cps):
            forward_late(k, *early[k])

    @pl.when(pas == 1)
    def _backward():
        @pl.when(j == 0)
        def _():
            sb_ref[...] = jnp.zeros_like(sb_ref)

        for k in reversed(range(cps)):
            rows = slice(k * t, (k + 1) * t)
            c = n_chunks - 1 - j * cps - (cps - 1 - k)
            cols = cols_ref[c]
            e_x = _dot(cols, expand_ref[2])
            w_x = _dot(cols, expand_ref[3])
            gate = _silu(z_ref[rows, :].astype(F32))
            for g in range(SSD_GROUPS):
                gs = slice(g * gw, (g + 1) * gw)
                s_in = sb_ref[g]
                y = yacc_ref[c, :, gs] + _dot(c_of(rows, g), s_in.astype(BF16)) * e_x[:, gs]
                o_ref[rows, gs] = _rms(y * gate[:, gs], nw_ref[:, gs]).astype(BF16)
                xw = (xbc_ref[rows, gs].astype(F32) * w_x[:, gs]).astype(BF16)
                sb_ref[g] = s_in * e_x[0:1, gs] + _dot_tn(b_of(rows, g), xw)


def _ssd(xbc, dt, z, p, batch, seq, cps):
    t = SSD_CHUNK
    nc = seq // t
    ns = nc // cps

    def chunk_idx(b, pas, j):
        return b * ns + j + pas * (ns - 1 - 2 * j)

    def out_idx(b, pas, j):
        return b * ns + (ns - 1) - pas * j

    blk = lambda w, f: pl.BlockSpec((cps * t, w), lambda b, pas, j: (f(b, pas, j), 0))
    resident = [_resident(a) for a in (p["ssd_a_log"], p["ssd_d"], p["ssd_norm"],
                                       _ssd_expand_matrices())]
    state = pltpu.VMEM((SSD_GROUPS, SSD_STATE, SSD_INNER // SSD_GROUPS), F32)
    return pl.pallas_call(
        functools.partial(_ssd_body, n_chunks=nc),
        grid=(batch, 2, ns),
        in_specs=[blk(SSD_CONV_DIM, chunk_idx), blk(LANES, chunk_idx), blk(SSD_INNER, chunk_idx)]
                 + [s for _, s in resident],
        out_specs=blk(SSD_INNER, out_idx),
        out_shape=jax.ShapeDtypeStruct((batch * seq, SSD_INNER), BF16),
        scratch_shapes=[pltpu.VMEM((nc, t, SSD_INNER), F32), state, state,
                        pltpu.VMEM((nc, t, LANES), BF16)],
        compiler_params=pltpu.CompilerParams(
            dimension_semantics=("arbitrary", "arbitrary", "arbitrary"),
            vmem_limit_bytes=VMEM_SSD),
        name="ssd",
    )(xbc, dt, z, *[a for a, _ in resident])


def _swa_body(sink_ref, q_ref, kp_ref, kc_ref, kn_ref, vp_ref, vc_ref, vn_ref, nw_ref, o_ref,
              *, layer):
    assert SWA_WINDOW == SWA_BLOCK
    j = pl.program_id(1)
    blk = SWA_BLOCK
    n_blk = q_ref.shape[0] // blk
    kb = jnp.concatenate([kp_ref[...], kc_ref[...], kn_ref[...]], axis=0)
    vb = jnp.concatenate([vp_ref[...], vc_ref[...], vn_ref[...]], axis=0)
    qi = lax.broadcasted_iota(jnp.int32, (blk, blk), 0)
    kj = lax.broadcasted_iota(jnp.int32, (blk, blk), 1)
    lo = _lane_iota((blk, LANES)) < HALF
    lo_v = _lane_iota(vb.shape) < HALF
    zero_q = jnp.zeros((blk, LANES), BF16)
    lane_v = _lane_iota(vb.shape)
    unit = lambda at: jnp.where(lane_v == at, 1.0, 0.0).astype(BF16)
    v_half = (jnp.where(lo_v, vb, unit(HALF)), jnp.where(lo_v, unit(0), vb))
    first = jnp.where(j == 0, blk, 0)
    last = jnp.where(j == pl.num_programs(1) - 1, blk, 0)
    items = [(t, g, half) for t in range(n_blk) for g in range(SWA_WIDTH // LANES)
             for half in range(2)]
    logits = {}
    for t, g, half in items:
        qg = q_ref[t * blk:(t + 1) * blk, g * LANES:(g + 1) * LANES]
        qm = jnp.where(lo, qg, zero_q) if half == 0 else jnp.where(lo, zero_q, qg)
        logits[t, g, half] = _dot_nt(qm, kb[t * blk:(t + 3) * blk])
    scaled = {}
    for t, g, half in items:
        keep_prev = kj >= (qi + first if t == 0 else qi)
        keep_next = kj <= (qi - last if t == n_blk - 1 else qi)
        s = logits[t, g, half]
        s_prev = jnp.where(keep_prev, s[:, :blk], NEG_INF)
        s_own = s[:, blk:2 * blk]
        s_next = jnp.where(keep_next, s[:, 2 * blk:], NEG_INF)
        sk = sink_ref[layer, g + 2 * half] * LOG2E
        m = jnp.max(jnp.maximum(jnp.maximum(s_prev, s_own), s_next), axis=-1, keepdims=True)
        m = jnp.maximum(m, sk)
        pexp = jnp.concatenate([jnp.exp2(v - m) for v in (s_prev, s_own, s_next)],
                               axis=-1).astype(BF16)
        acc = _dot(pexp, v_half[half][t * blk:(t + 3) * blk])
        ones_lane = HALF if half == 0 else 0
        den = acc[:, ones_lane:ones_lane + 1] + jnp.exp2(sk - m)
        scaled[t, g, half] = acc / den
    for t in range(n_blk):
        y = jnp.concatenate([jnp.where(lo, scaled[t, g, 0], scaled[t, g, 1])
                             for g in range(SWA_WIDTH // LANES)], axis=-1)
        o_ref[t * blk:(t + 1) * blk, :] = _rms(y, nw_ref[...]).astype(BF16)


def _swa(q, k, v, sink, nw, batch, seq, n_blk):
    blk = SWA_BLOCK
    nb = seq // blk
    ns = nb // n_blk
    cur = lambda b, j: (b * ns + j, 0)
    prv = lambda b, j: (b * nb + jnp.maximum(j * n_blk - 1, 0), 0)
    nxt = lambda b, j: (b * nb + jnp.minimum((j + 1) * n_blk, nb - 1), 0)
    edge = lambda f: pl.BlockSpec((blk, LANES), f)
    main = pl.BlockSpec((n_blk * blk, LANES), cur)
    sinks, layer = sink
    nw_arr, nw_spec = _resident(nw)
    return pl.pallas_call(
        functools.partial(_swa_body, layer=layer),
        grid=(batch, ns),
        in_specs=[pl.BlockSpec(memory_space=pltpu.SMEM),
                  pl.BlockSpec((n_blk * blk, SWA_WIDTH), cur),
                  edge(prv), main, edge(nxt), edge(prv), main, edge(nxt), nw_spec],
        out_specs=pl.BlockSpec((n_blk * blk, SWA_WIDTH), cur),
        out_shape=jax.ShapeDtypeStruct((batch * seq, SWA_WIDTH), BF16),
        compiler_params=pltpu.CompilerParams(dimension_semantics=("parallel", "parallel")),
        name="swa",
    )(sinks, q, k, k, k, v, v, v, nw_arr)


def _mla_body(bound_ref, q_ref, k_ref, v_ref, nw_ref, o_ref, *, key_chunk, layer):
    tq = q_ref.shape[0]
    seq = k_ref.shape[0]
    heads = [slice(h * LANES, (h + 1) * LANES) for h in range(MLA_HEADS)]
    chunks = [slice(c * key_chunk, (c + 1) * key_chunk) for c in range(seq // key_chunk)]
    lo = _lane_iota((tq, LANES)) < HALF
    bound = bound_ref[layer]
    use_bound = bound <= MLA_BOUND_LIMIT

    def denom(acc, h):
        return acc[:, HALF:HALF + 1] if h % 2 == 0 else acc[:, 0:1]

    def finish(accs, dens):
        outs = [jnp.where(lo, accs[2 * p] / dens[2 * p], accs[2 * p + 1] / dens[2 * p + 1])
                for p in range(MLA_HEADS // 2)]
        y = jnp.concatenate(outs, axis=-1)
        o_ref[...] = _rms(y, nw_ref[...]).astype(BF16)

    @pl.when(use_bound)
    def _bounded():
        accs = []
        for h, sl in enumerate(heads):
            qh = q_ref[:, sl]
            acc = None
            for ks in chunks:
                pexp = jnp.exp2(_dot_nt(qh, k_ref[ks, sl])).astype(BF16)
                part = _dot(pexp, v_ref[ks, sl])
                acc = part if acc is None else acc + part
            accs.append(acc)
        finish(accs, [denom(a, h) for h, a in enumerate(accs)])

    @pl.when(jnp.logical_not(use_bound))
    def _online():
        accs, dens = [], []
        for h, sl in enumerate(heads):
            qh = q_ref[:, sl]
            m = jnp.full((tq, 1), NEG_INF, F32)
            acc = jnp.zeros((tq, LANES), F32)
            for ks in chunks:
                s = _dot_nt(qh, k_ref[ks, sl])
                m_new = jnp.maximum(m, jnp.max(s, axis=-1, keepdims=True))
                pexp = jnp.exp2(s - m_new).astype(BF16)
                acc = jnp.exp2(m - m_new) * acc + _dot(pexp, v_ref[ks, sl])
                m = m_new
            accs.append(acc)
            dens.append(denom(acc, h))
        finish(accs, dens)


def _mla(q, k, v, bound, nw, batch, seq, tq, key_chunk):
    nq = seq // tq
    w = MLA_HEADS * LANES
    full = pl.BlockSpec((seq, w), lambda b, i: (b, 0))
    bounds, layer = bound
    nw, nw_spec = _resident(nw)
    return pl.pallas_call(
        functools.partial(_mla_body, key_chunk=key_chunk, layer=layer),
        grid=(batch, nq),
        in_specs=[pl.BlockSpec(memory_space=pltpu.SMEM),
                  pl.BlockSpec((tq, w), lambda b, i: (b * nq + i, 0)), full, full, nw_spec],
        out_specs=pl.BlockSpec((tq, MLA_WIDTH), lambda b, i: (b * nq + i, 0)),
        out_shape=jax.ShapeDtypeStruct((batch * seq, MLA_WIDTH), BF16),
        compiler_params=pltpu.CompilerParams(
            dimension_semantics=("parallel", "parallel"),
            vmem_limit_bytes=VMEM_MLA),
        name="mla",
    )(bounds, q, k, v, nw)


def _rope_tables(seq):
    def angles(dim):
        inv = 1.0 / np.power(ROPE_THETA, np.arange(0, dim, 2, dtype=np.float64) / dim)
        return np.arange(seq, dtype=np.float64)[:, None] * inv[None, :]

    a64 = angles(SWA_HEAD_DIM)
    c, s = np.cos(a64), np.sin(a64)
    zero = np.zeros_like(c)
    c64 = np.concatenate([c, c, c, c], axis=-1)
    sa64 = np.concatenate([-s, zero, -s, zero], axis=-1)
    sb64 = np.concatenate([zero, s, zero, s], axis=-1)
    a32 = angles(MLA_ROPE)
    c, s = np.cos(a32), np.sin(a32)
    zero = np.zeros_like(c)
    ones = np.ones((seq, MLA_NOPE))
    pad = np.zeros((seq, LANES - MLA_QK))
    zn = np.zeros((seq, MLA_NOPE))
    cm = np.concatenate([ones, c, c, pad], axis=-1)
    sam = np.concatenate([zn, -s, zero, pad], axis=-1)
    sbm = np.concatenate([zn, zero, s, pad], axis=-1)
    return jnp.asarray(np.stack([c64, sa64, sb64, cm, sam, sbm]).astype(np.float32))


def _swa_head_perm(t, axis):
    parts = jnp.split(t, SWA_HEADS, axis=axis)
    return jnp.concatenate([parts[0], parts[2], parts[1], parts[3]], axis=axis)


def _prep_params(ffn1_norm, mix_norm, w_in, ssd_conv_w, ssd_conv_b, ssd_dt_bias, ssd_a_log,
                 ssd_d, ssd_norm, swa_q_norm, swa_k_norm, swa_sink, swa_out_norm,
                 mla_q_lat_norm, mla_w_uq, mla_kv_norm, mla_w_ukv, mla_q_norm, mla_k_norm,
                 mla_out_norm, w_out, ffn2_norm):
    nl, d = w_in.shape[:2]
    row = lambda v: v.reshape(nl, 1, -1).astype(F32)
    pad_lanes = lambda v, n: jnp.pad(v, [(0, 0)] * (v.ndim - 1) + [(0, n - v.shape[-1])])

    w_in_p = _permute_w_in(jnp.swapaxes(w_in, 1, 2))

    wuq = mla_w_uq.reshape(nl, MLA_Q_RANK, MLA_HEADS, MLA_QK)
    wuq = pad_lanes(wuq, LANES).reshape(nl, MLA_Q_RANK, MLA_HEADS * LANES).astype(BF16)
    wukv = mla_w_ukv.reshape(nl, MLA_KV_RANK, MLA_HEADS, MLA_NOPE + MLA_V)
    knope = pad_lanes(wukv[..., :MLA_NOPE], LANES)
    vv = wukv[..., MLA_NOPE:]
    zv = jnp.zeros_like(vv)
    even = (np.arange(MLA_HEADS) % 2 == 0)[None, None, :, None]
    vpad = jnp.concatenate([jnp.where(even, vv, zv), jnp.where(even, zv, vv)], axis=-1)
    wukv_p = jnp.concatenate([knope.reshape(nl, MLA_KV_RANK, -1),
                              vpad.reshape(nl, MLA_KV_RANK, -1)], axis=-1).astype(BF16)

    a, b = SSD_INNER, SSD_INNER + SWA_WIDTH
    w_out_p = jnp.concatenate([w_out[:, :a], _swa_head_perm(w_out[:, a:b], 1), w_out[:, b:]],
                              axis=1).astype(BF16)

    peak = lambda g: jnp.max(jnp.abs(g.astype(F32)), axis=-1)
    mla_bound = (math.sqrt(MLA_QK) * LOG2E * MLA_BOUND_SLACK) * peak(mla_q_norm) * peak(mla_k_norm)

    two = lambda v: jnp.concatenate([v, v], axis=-1)
    return {
        "mla_bound": mla_bound,
        "mla_k_bias": -mla_bound[:, None, None] * (np.arange(LANES) == MLA_QK).astype(np.float32),
        "ffn1_norm": row(ffn1_norm),
        "ffn2_norm": row(ffn2_norm),
        "w_in": w_in_p,
        "mix_norm": row(mix_norm),
        "conv_w": ssd_conv_w.astype(F32),
        "conv_b": row(ssd_conv_b),
        "dt_bias": pad_lanes(row(ssd_dt_bias), LANES),
        "swa_q_gain": row(two(swa_q_norm)),
        "swa_k_gain": row(two(swa_k_norm)),
        "mla_qlat_norm": row(mla_q_lat_norm),
        "w_uq": wuq,
        "mla_kv_norm": row(mla_kv_norm),
        "w_ukv": wukv_p,
        "mla_q_gain": pad_lanes(row(mla_q_norm), LANES),
        "mla_k_gain": pad_lanes(row(mla_k_norm), LANES),
        "ssd_a_log": pad_lanes(row(ssd_a_log), LANES),
        "ssd_d": row(jnp.repeat(ssd_d, SSD_HEAD_DIM, axis=-1)),
        "ssd_norm": row(ssd_norm),
        "swa_sink": swa_sink.astype(F32),
        "swa_out_norm": row(_swa_head_perm(swa_out_norm, 1)),
        "mla_out_norm": row(mla_out_norm),
        "w_out": w_out_p,
    }


def _tiles(seq):
    pick = lambda pref: math.gcd(seq, pref)
    return {"ffn": pick(1024), "proj": pick(512), "mla_q": pick(256), "mla_k": pick(1024),
            "swa_blocks": pick(16 * SWA_BLOCK) // SWA_BLOCK,
            "ssd_chunks": pick(4 * SSD_CHUNK) // SSD_CHUNK}


def kernel(x, ffn1_norm, ffn1_gate, ffn1_up, ffn1_down, mix_norm, w_in, ssd_conv_w, ssd_conv_b, ssd_dt_bias, ssd_a_log, ssd_d, ssd_norm, swa_q_norm, swa_k_norm, swa_sink, swa_out_norm, mla_q_lat_norm, mla_w_uq, mla_kv_norm, mla_w_ukv, mla_q_norm, mla_k_norm, mla_out_norm, w_out, ffn2_norm, ffn2_gate, ffn2_up, ffn2_down):
    batch, seq, d = x.shape
    assert seq % SSD_CHUNK == 0 and seq % SWA_BLOCK == 0
    assert ffn1_gate.shape[-1] % FFN_CHUNK == 0
    stacked = _prep_params(ffn1_norm, mix_norm, w_in, ssd_conv_w, ssd_conv_b, ssd_dt_bias,
                           ssd_a_log, ssd_d, ssd_norm, swa_q_norm, swa_k_norm, swa_sink,
                           swa_out_norm, mla_q_lat_norm, mla_w_uq, mla_kv_norm, mla_w_ukv,
                           mla_q_norm, mla_k_norm, mla_out_norm, w_out, ffn2_norm)
    tl = _tiles(seq)
    tabs = _rope_tables(seq)
    xf = x.reshape(batch * seq, d).astype(F32)
    for l in range(w_in.shape[0]):
        p = {k: (v, l) for k, v in stacked.items()}
        xf = _ffn(xf, p["ffn1_norm"], l, ffn1_gate, ffn1_up, ffn1_down, tm=tl["ffn"])
        z, xbc, dt, swq, swk, swv, mq, mk, mv = _in_proj(xf, tabs, p, seq, tl["proj"])
        y_ssd = _ssd(xbc, dt, z, p, batch, seq, tl["ssd_chunks"])
        y_swa = _swa(swq, swk, swv, p["swa_sink"], p["swa_out_norm"], batch, seq,
                     tl["swa_blocks"])
        y_mla = _mla(mq, mk, mv, p["mla_bound"], p["mla_out_norm"], batch, seq, tl["mla_q"],
                     tl["mla_k"])
        xf = _ffn(xf, p["ffn2_norm"], l, ffn2_gate, ffn2_up, ffn2_down, tm=tl["ffn"],
                  mixer=(y_ssd, y_swa, y_mla, p["w_out"]))
    return xf.reshape(batch, seq, d).astype(x.dtype)
```
